```python
import jax, jax.numpy as jnp
from jax import lax
import numpy as np

D_MODEL = 2048
BATCH = 2
SEQ = 4096
DEPTH = 1
DEC_BATCH = 128
DEC_SEQ = 4
PAST_LEN = 2048
PAGE_SIZE = 128

POOL_WINDOWS = (2, 4, 8, 16)
N_POOL_GROUPS = 4
POOL_WIDTH = D_MODEL // 2
POOL_GROUP = POOL_WIDTH // N_POOL_GROUPS
POOL_BUF = max(POOL_WINDOWS) - 1
HEAD_DIM = 128
N_HEADS = D_MODEL // HEAD_DIM
N_KV_HEADS = 4
N_GROUP = N_HEADS // N_KV_HEADS
ATTN_WIDTH = N_HEADS * HEAD_DIM
KV_WIDTH = N_KV_HEADS * HEAD_DIM
N_IDX_HEADS = 16
IDX_DIM = 64
TOPK_MAX = 256
Q_BLOCK = 128
D_FF = -(-8 * D_MODEL // (3 * 256)) * 256
RMS_EPS = 1e-6
IN_SIZES = (POOL_WIDTH, ATTN_WIDTH, KV_WIDTH, KV_WIDTH, N_IDX_HEADS * IDX_DIM, IDX_DIM, N_IDX_HEADS, D_MODEL, D_MODEL)
IN_WIDTH = sum(IN_SIZES)

kernel_name = 'cond_pool_dsa_hybrid_step'


def rmsnorm(x, g):
    xf = x.astype(jnp.float32)
    y = xf * lax.rsqrt(jnp.mean(xf * xf, axis=-1, keepdims=True) + RMS_EPS)
    return (y * g.astype(jnp.float32)).astype(x.dtype)


def modulate(x, g, shift, scale):
    return rmsnorm(x, g) * (1 + scale[:, None]) + shift[:, None]


def adaln(c, w_ada, b_ada):
    mod = jax.nn.silu(c) @ w_ada + b_ada
    return jnp.split(mod, 6, axis=-1)


def split_proj(p):
    outs, o = [], 0
    for s in IN_SIZES:
        outs.append(p[..., o:o + s])
        o += s
    return outs


def pool_mix(seq, start_pos, n_out, w_grp, scale):
    N, R, _ = seq.shape
    cs = jnp.cumsum(jnp.pad(seq.astype(jnp.float32), ((0, 0), (1, 0), (0, 0))), axis=1)
    out_idx = jnp.arange(R - n_out, R)
    pos = start_pos + out_idx
    cur = seq[:, R - n_out:].astype(jnp.float32)
    outs = []
    for g, w in enumerate(POOL_WINDOWS):
        sl = slice(g * POOL_GROUP, (g + 1) * POOL_GROUP)
        cs_g = cs[..., sl]
        lo = jnp.maximum(out_idx + 1 - w, 0)
        win_sum = cs_g[:, out_idx + 1] - cs_g[:, lo]
        cnt = jnp.minimum(pos + 1, w).astype(jnp.float32)
        outs.append(win_sum / cnt[None, :, None] - cur[..., sl])
    d = jnp.stack(outs, axis=2).astype(seq.dtype)
    y = jnp.einsum('ntgc,gce->ntge', d, w_grp).reshape(N, n_out, POOL_WIDTH)
    return y * scale


def indexer_scores(qi, ki, wi):
    dots = jnp.einsum('nthd,nsd->nths', qi, ki, preferred_element_type=jnp.float32) * (IDX_DIM ** -0.5)
    return jnp.einsum('nths,nth->nts', jax.nn.relu(dots), wi.astype(jnp.float32) * (N_IDX_HEADS ** -0.5))


def select_keys(scores, q_pos, topk):
    S = scores.shape[-1]
    vis = jnp.arange(S)[None, :] <= q_pos[:, None]
    masked = jnp.where(vis[None], scores, -jnp.inf)
    _, idx = lax.top_k(masked, topk)
    valid = idx <= q_pos[None, :, None]
    return idx, valid


def sparse_attend(q, k_sel, v_sel, valid):
    N, T = q.shape[:2]
    qg = q.reshape(N, T, N_KV_HEADS, N_GROUP, HEAD_DIM)
    s = jnp.einsum('ntjgd,ntkjd->ntjgk', qg, k_sel, preferred_element_type=jnp.float32) * (HEAD_DIM ** -0.5)
    s = jnp.where(valid[:, :, None, None, :], s, -jnp.inf)
    p = jax.nn.softmax(s, axis=-1).astype(v_sel.dtype)
    o = jnp.einsum('ntjgk,ntkjd->ntjgd', p, v_sel)
    return o.reshape(N, T, ATTN_WIDTH)


def gather_rows(a, i):
    return jax.vmap(lambda ab, ib: ab[ib])(a, i)


def front(x, c, lw):
    mods = adaln(c, lw['w_ada'], lw['b_ada'])
    u = modulate(x, lw['g_norm1'], mods[0], mods[1])
    return mods, split_proj(u @ lw['w_in'])


def back(x, mods, pool_out, attn_out, ga, gb, lw):
    mix = jax.nn.sigmoid(ga) * (pool_out @ lw['w_up_pool']) + jax.nn.sigmoid(gb) * (attn_out @ lw['w_up_attn'])
    h = x + mods[2][:, None] * (mix @ lw['w_out'])
    hn = modulate(h, lw['g_norm2'], mods[3], mods[4])
    gate, up = jnp.split(hn @ lw['w_ffn_in'], 2, axis=-1)
    return h + mods[5][:, None] * ((jax.nn.silu(gate) * up) @ lw['w_ffn_out'])


def layer_prompt(x, c, lw):
    B, T, _ = x.shape
    mods, (pin, q, k, v, qi, ki, wi, ga, gb) = front(x, c, lw)
    seq = jnp.concatenate([jnp.zeros((B, POOL_BUF, POOL_WIDTH), pin.dtype), pin], axis=1)
    pool_out = pool_mix(seq, -POOL_BUF, T, lw['w_pool_grp'], lw['pool_scale'])
    pool_tail = seq[:, -POOL_BUF:]
    q = q.reshape(B, T, N_HEADS, HEAD_DIM)
    k = k.reshape(B, T, N_KV_HEADS, HEAD_DIM)
    v = v.reshape(B, T, N_KV_HEADS, HEAD_DIM)
    qi = qi.reshape(B, T, N_IDX_HEADS, IDX_DIM)
    topk = min(TOPK_MAX, T // 4)
    nb = T // Q_BLOCK

    def blk(args):
        qb, qib, wib, t0 = args
        q_pos = t0 + jnp.arange(Q_BLOCK)
        idx, valid = select_keys(indexer_scores(qib, ki, wib), q_pos, topk)
        return sparse_attend(qb, gather_rows(k, idx), gather_rows(v, idx), valid)

    to_blocks = lambda a: jnp.swapaxes(a.reshape(B, nb, Q_BLOCK, *a.shape[2:]), 0, 1)
    out = lax.map(blk, (to_blocks(q), to_blocks(qi), to_blocks(wi), jnp.arange(nb) * Q_BLOCK))
    attn = jnp.swapaxes(out, 0, 1).reshape(B, T, ATTN_WIDTH)
    y = back(x, mods, pool_out, attn, ga, gb, lw)
    return y, k, v, ki, pool_tail


def layer_sample(x, c, cache_k, cache_v, cache_ik, st_pool, page_table, lw):
    Bn, Tn, _ = x.shape
    page = cache_k.shape[1]
    past = page_table.shape[1] * page
    mods, (pin, q, k, v, qi, ki, wi, ga, gb) = front(x, c, lw)
    seq = jnp.concatenate([st_pool.astype(pin.dtype), pin], axis=1)
    pool_out = pool_mix(seq, past - POOL_BUF, Tn, lw['w_pool_grp'], lw['pool_scale'])
    pool_tail = seq[:, -POOL_BUF:]
    q = q.reshape(Bn, Tn, N_HEADS, HEAD_DIM)
    k = k.reshape(Bn, Tn, N_KV_HEADS, HEAD_DIM)
    v = v.reshape(Bn, Tn, N_KV_HEADS, HEAD_DIM)
    qi = qi.reshape(Bn, Tn, N_IDX_HEADS, IDX_DIM)
    ik_all = jnp.concatenate([cache_ik[page_table].reshape(Bn, past, IDX_DIM).astype(ki.dtype), ki], axis=1)
    topk = min(TOPK_MAX, (past + Tn) // 4)
    q_pos = past + jnp.arange(Tn)
    idx, valid = select_keys(indexer_scores(qi, ik_all, wi), q_pos, topk)
    is_past = (idx < past)[..., None, None]
    ip = jnp.clip(idx, 0, past - 1)
    phys = jax.vmap(lambda pt, i: pt[i])(page_table, ip // page)
    flat = phys * page + ip % page
    inew = jnp.clip(idx - past, 0, Tn - 1)
    k_sel = jnp.where(is_past, cache_k.reshape(-1, N_KV_HEADS, HEAD_DIM)[flat].astype(k.dtype), gather_rows(k, inew))
    v_sel = jnp.where(is_past, cache_v.reshape(-1, N_KV_HEADS, HEAD_DIM)[flat].astype(v.dtype), gather_rows(v, inew))
    attn = sparse_attend(q, k_sel, v_sel, valid)
    y = back(x, mods, pool_out, attn, ga, gb, lw)
    return y, k, v, ki, pool_tail


def setup_inputs(seed: int = 0) -> dict:
    key = jax.random.key(seed)
    ks = jax.random.split(key, 24)
    n_pages = PAST_LEN // PAGE_SIZE
    n_used = DEC_BATCH * n_pages
    n_phys = n_used + n_used // 4
    nrm = lambda k, shape, s=1.0: s * jax.random.normal(k, shape, jnp.float32)
    page_table = jax.random.permutation(ks[0], n_phys)[:n_used].reshape(DEC_BATCH, n_pages).astype(jnp.int32)
    return {
        'x_prompt': nrm(ks[1], (BATCH, SEQ, D_MODEL)),
        'x_sample': nrm(ks[2], (DEC_BATCH, DEC_SEQ, D_MODEL)),
        'cache_k': nrm(ks[3], (DEPTH, n_phys, PAGE_SIZE, N_KV_HEADS, HEAD_DIM)),
        'cache_v': nrm(ks[4], (DEPTH, n_phys, PAGE_SIZE, N_KV_HEADS, HEAD_DIM)),
        'cache_idx_k': nrm(ks[5], (DEPTH, n_phys, PAGE_SIZE, IDX_DIM)),
        'state_pool': nrm(ks[6], (DEPTH, DEC_BATCH, POOL_BUF, POOL_WIDTH)),
        'page_table': page_table,
        'c_prompt': nrm(ks[7], (BATCH, D_MODEL)),
        'c_sample': nrm(ks[8], (DEC_BATCH, D_MODEL)),
        'w_ada': nrm(ks[9], (DEPTH, D_MODEL, 6 * D_MODEL), 0.5 * D_MODEL ** -0.5),
        'b_ada': nrm(ks[10], (DEPTH, 6 * D_MODEL), 0.01),
        'g_norm1': 1.0 + nrm(ks[11], (DEPTH, D_MODEL), 0.1),
        'w_in': nrm(ks[12], (DEPTH, D_MODEL, IN_WIDTH), D_MODEL ** -0.5),
        'w_pool_grp': nrm(ks[13], (DEPTH, N_POOL_GROUPS, POOL_GROUP, POOL_GROUP), POOL_GROUP ** -0.5),
        'pool_scale': 1.0 + nrm(ks[14], (DEPTH, POOL_WIDTH), 0.1),
        'w_up_pool': nrm(ks[15], (DEPTH, POOL_WIDTH, D_MODEL), POOL_WIDTH ** -0.5),
        'w_up_attn': nrm(ks[16], (DEPTH, ATTN_WIDTH, D_MODEL), ATTN_WIDTH ** -0.5),
        'w_out': nrm(ks[17], (DEPTH, D_MODEL, D_MODEL), D_MODEL ** -0.5),
        'g_norm2': 1.0 + nrm(ks[18], (DEPTH, D_MODEL), 0.1),
        'w_ffn_in': nrm(ks[19], (DEPTH, D_MODEL, 2 * D_FF), D_MODEL ** -0.5),
        'w_ffn_out': nrm(ks[20], (DEPTH, D_FF, D_MODEL), D_FF ** -0.5),
        'g_final': 1.0 + nrm(ks[21], (D_MODEL,), 0.1),
    }


def reference(x_prompt, x_sample, cache_k, cache_v, cache_idx_k, state_pool, page_table, c_prompt, c_sample,
              w_ada, b_ada, g_norm1, w_in, w_pool_grp, pool_scale, w_up_pool, w_up_attn, w_out, g_norm2,
              w_ffn_in, w_ffn_out, g_final):
    hp, hs = x_prompt, x_sample
    kp, vp, ip, pp, ksm, vsm, ism, psm = [], [], [], [], [], [], [], []
    for l in range(DEPTH):
        lw = {'w_ada': w_ada[l], 'b_ada': b_ada[l], 'g_norm1': g_norm1[l], 'w_in': w_in[l],
              'w_pool_grp': w_pool_grp[l], 'pool_scale': pool_scale[l], 'w_up_pool': w_up_pool[l],
              'w_up_attn': w_up_attn[l], 'w_out': w_out[l], 'g_norm2': g_norm2[l],
              'w_ffn_in': w_ffn_in[l], 'w_ffn_out': w_ffn_out[l]}
        hp, k1, v1, i1, p1 = layer_prompt(hp, c_prompt, lw)
        hs, k2, v2, i2, p2 = layer_sample(hs, c_sample, cache_k[l], cache_v[l], cache_idx_k[l], state_pool[l], page_table, lw)
        kp.append(k1); vp.append(v1); ip.append(i1); pp.append(p1)
        ksm.append(k2); vsm.append(v2); ism.append(i2); psm.append(p2)
    y_prompt = rmsnorm(hp, g_final)
    y_sample = rmsnorm(hs, g_final)
    k_prompt, v_prompt, idxk_prompt, pool_prompt = jnp.stack(kp), jnp.stack(vp), jnp.stack(ip), jnp.stack(pp)
    k_sample, v_sample, idxk_sample, pool_sample = jnp.stack(ksm), jnp.stack(vsm), jnp.stack(ism), jnp.stack(psm)
    return (y_prompt, y_sample, k_prompt, v_prompt, idxk_prompt, pool_prompt, k_sample, v_sample, idxk_sample, pool_sample)
```

```python
import functools

import jax
import jax.numpy as jnp
from jax import lax
from jax.experimental import pallas as pl
from jax.experimental.pallas import tpu as pltpu

F32 = jnp.float32
BF16 = jnp.bfloat16

RMS_EPS = 1e-6
POOL_WINDOWS = (2, 4, 8, 16)
POOL_BUF = max(POOL_WINDOWS) - 1
HEAD_DIM = 128
N_KV_HEADS = 4
N_IDX_HEADS = 16
IDX_DIM = 64
TOPK_MAX = 256
Q_BLOCK = 128
LANES = 128
NEG = -1e30
NT_DIMS = (((1,), (1,)), ((), ()))
VMEM_LIMIT = 56 * 1024 * 1024


def _cparams(n_axes):
    return pltpu.CompilerParams(dimension_semantics=("arbitrary",) * n_axes, vmem_limit_bytes=VMEM_LIMIT)


def _rms(x, g):
    return x * lax.rsqrt(jnp.mean(x * x, axis=-1, keepdims=True) + RMS_EPS) * g


def _ada_kernel(c_ref, w_ref, b_ref, o_ref):
    c = c_ref[...]
    s = (c * jax.nn.sigmoid(c)).astype(BF16)
    o_ref[...] = jnp.dot(s, w_ref[...].astype(BF16), preferred_element_type=F32) + b_ref[...]


def _ada(c_all, w_ada, b_ada):
    rows, d = c_all.shape
    n = w_ada.shape[1]
    tn = 512
    return pl.pallas_call(
        _ada_kernel,
        out_shape=jax.ShapeDtypeStruct((rows, n), F32),
        grid=(n // tn,),
        in_specs=[pl.BlockSpec((rows, d), lambda j: (0, 0)),
                  pl.BlockSpec((d, tn), lambda j: (0, j)),
                  pl.BlockSpec((1, tn), lambda j: (0, j))],
        out_specs=pl.BlockSpec((rows, tn), lambda j: (0, j)),
        compiler_params=_cparams(1),
        name="ada",
    )(c_all, w_ada, b_ada.reshape(1, n))


FRONT_TN = 512
N_FRONT_TILES = 19
W16 = 16 * FRONT_TN
W32 = 5 * FRONT_TN


def _front_kernel(x_ref, g_ref, sh_ref, sc_ref, w_ref, o16_ref, o32_ref, u_ref):
    n = pl.program_id(1)

    @pl.when(n == 0)
    def _():
        u_ref[...] = (_rms(x_ref[...], g_ref[...]) * (1.0 + sc_ref[...]) + sh_ref[...]).astype(BF16)

    r = jnp.dot(u_ref[...], w_ref[...], preferred_element_type=F32)

    @pl.when((n < 14) | (n == 16) | (n == 17))
    def _():
        o16_ref[...] = r.astype(BF16)

    @pl.when(n >= 14)
    def _():
        o32_ref[...] = r


def _o16_block(n):
    return jnp.where(n < 14, n, jnp.where(n < 16, 13, jnp.where(n < 18, n - 2, 15)))


def _mod_spec(mod, tm, rows_per_mod):
    d = mod.shape[-1]
    if mod.ndim == 3:
        tiles_per_batch = rows_per_mod // tm
        return pl.BlockSpec((None, 1, d), lambda i, *_: (i // tiles_per_batch, 0, 0))
    return pl.BlockSpec((tm, d), lambda i, *_: (i, 0))


def _front(x, g1, shift, scale, w_cat, tm, rows_per_mod):
    m, d = x.shape
    return pl.pallas_call(
        _front_kernel,
        out_shape=(jax.ShapeDtypeStruct((m, W16), BF16), jax.ShapeDtypeStruct((m, W32), F32)),
        grid=(m // tm, N_FRONT_TILES),
        in_specs=[pl.BlockSpec((tm, d), lambda i, n: (i, 0)),
                  pl.BlockSpec((1, d), lambda i, n: (0, 0)),
                  _mod_spec(shift, tm, rows_per_mod),
                  _mod_spec(scale, tm, rows_per_mod),
                  pl.BlockSpec((d, FRONT_TN), lambda i, n: (0, n))],
        out_specs=(pl.BlockSpec((tm, FRONT_TN), lambda i, n: (i, _o16_block(n))),
                   pl.BlockSpec((tm, FRONT_TN), lambda i, n: (i, jnp.maximum(n - 14, 0)))),
        scratch_shapes=[pltpu.VMEM((tm, d), BF16)],
        compiler_params=_cparams(2),
        name="front",
    )(x, g1, shift, scale, w_cat)


def _pool_d_prompt_kernel(pin_ref, halo_ref, o_ref, buf_ref, *, tp, group):
    i = pl.program_id(1)
    cur = pin_ref[...]
    halo = jnp.where(i > 0, halo_ref[...], 0.0)
    hb = halo.shape[0]
    buf_ref[0:hb, :] = halo
    buf_ref[hb:hb + tp, :] = cur
    pos = i * tp + lax.broadcasted_iota(jnp.int32, (tp, 1), 0)
    for gi, w in enumerate(POOL_WINDOWS):
        sl = slice(gi * group, (gi + 1) * group)
        acc = cur[:, sl]
        for j in range(1, w):
            acc = acc + buf_ref[hb - j:hb - j + tp, sl]
        cnt = jnp.minimum(pos + 1, w).astype(F32)
        o_ref[:, sl] = (acc / cnt - cur[:, sl]).astype(BF16)


def _pool_d_prompt(p32, batch, seq, width):
    tp, hb = 512, 16
    group = width // len(POOL_WINDOWS)
    p32b = p32.reshape(batch, seq, p32.shape[-1])
    return pl.pallas_call(
        functools.partial(_pool_d_prompt_kernel, tp=tp, group=group),
        out_shape=jax.ShapeDtypeStruct((batch, seq, width), BF16),
        grid=(batch, seq // tp),
        in_specs=[pl.BlockSpec((None, tp, width), lambda b, i: (b, i, 0)),
                  pl.BlockSpec((None, hb, width), lambda b, i: (b, jnp.maximum(i * (tp // hb) - 1, 0), 0))],
        out_specs=pl.BlockSpec((None, tp, width), lambda b, i: (b, i, 0)),
        scratch_shapes=[pltpu.VMEM((tp + hb, width), F32)],
        compiler_params=_cparams(2),
        name="pool_d_prompt",
    )(p32b, p32b)


def _pool_d_sample_kernel(seq_ref, o_ref, *, n_new, group):
    for t in range(n_new):
        r = POOL_BUF + t
        for gi, w in enumerate(POOL_WINDOWS):
            sl = slice(gi * group, (gi + 1) * group)
            acc = seq_ref[r, :, sl]
            for j in range(1, w):
                acc = acc + seq_ref[r - j, :, sl]
            o_ref[t, :, sl] = (acc / float(w) - seq_ref[r, :, sl]).astype(BF16)


def _pool_d_sample(seq_t, n_new):
    rows, nb, width = seq_t.shape
    group = width // len(POOL_WINDOWS)
    return pl.pallas_call(
        functools.partial(_pool_d_sample_kernel, n_new=n_new, group=group),
        out_shape=jax.ShapeDtypeStruct((n_new, nb, width), BF16),
        grid=(1,),
        in_specs=[pl.BlockSpec((rows, nb, width), lambda i: (0, 0, 0))],
        out_specs=pl.BlockSpec((n_new, nb, width), lambda i: (0, 0, 0)),
        compiler_params=_cparams(1),
        name="pool_d_sample",
    )(seq_t)


def _key_to_float(key):
    bits = key ^ ((key >> 31) & jnp.int32(0x7FFFFFFF))
    return lax.bitcast_convert_type(bits, F32)


def _kth_largest(load_chunk, n_chunks, rows, chunk, k):
    int_min = jnp.int32(-(2 ** 31))

    def bit_body(it, key):
        cand = key ^ lax.shift_left(jnp.int32(1), 31 - it)
        cf = jnp.broadcast_to(_key_to_float(cand), (rows, LANES))

        def chunk_body(c, cnt):
            s = load_chunk(c)
            for j in range(chunk // LANES):
                cnt = cnt + jnp.where(s[:, j * LANES:(j + 1) * LANES] >= cf, 1, 0)
            return cnt

        cnt = lax.fori_loop(0, n_chunks, chunk_body, jnp.zeros((rows, LANES), jnp.int32))
        tot = jnp.sum(cnt, axis=1, keepdims=True)
        return jnp.where(tot >= k, cand, key)

    key = lax.fori_loop(0, 32, bit_body, jnp.full((rows, 1), int_min, jnp.int32))
    return _key_to_float(key)


PA_CHUNK = 512


def _p_attn_kernel(q_ref, qi_ref, wi_ref, kia_ref, kib_ref, k_ref, v_ref, o_ref, sc_ref, *, topk):
    blk = pl.program_id(1)
    t0 = blk * Q_BLOCK
    n_ch = (t0 + Q_BLOCK + PA_CHUNK - 1) // PA_CHUNK
    row_t = t0 + lax.broadcasted_iota(jnp.int32, (Q_BLOCK, 1), 0)
    n_grp = (q_ref.shape[1] // HEAD_DIM) // N_KV_HEADS

    wi = wi_ref[...] * (IDX_DIM ** -0.5 * N_IDX_HEADS ** -0.5)

    def score_chunk(c, carry):
        c0 = pl.multiple_of(c * PA_CHUNK, PA_CHUNK)
        ka = kia_ref[pl.ds(c0, PA_CHUNK), :]
        kb = kib_ref[pl.ds(c0, PA_CHUNK), :]
        acc = jnp.zeros((Q_BLOCK, PA_CHUNK), F32)
        for p in range(N_IDX_HEADS // 2):
            qp = qi_ref[:, p * LANES:(p + 1) * LANES]
            da = lax.dot_general(qp, ka, NT_DIMS, preferred_element_type=F32)
            db = lax.dot_general(qp, kb, NT_DIMS, preferred_element_type=F32)
            acc = acc + jnp.maximum(da, 0.0) * wi[:, 2 * p:2 * p + 1]
            acc = acc + jnp.maximum(db, 0.0) * wi[:, 2 * p + 1:2 * p + 2]
        col = c0 + lax.broadcasted_iota(jnp.int32, (Q_BLOCK, PA_CHUNK), 1)
        sc_ref[:, pl.ds(c0, PA_CHUNK)] = jnp.where(col <= row_t, acc, -jnp.inf)
        return carry

    lax.fori_loop(0, n_ch, score_chunk, 0)

    def load_chunk(c):
        return sc_ref[:, pl.ds(pl.multiple_of(c * PA_CHUNK, PA_CHUNK), PA_CHUNK)]

    thr = _kth_largest(load_chunk, n_ch, Q_BLOCK, PA_CHUNK, topk)
    thr = jnp.where(row_t + 1 <= topk, jnp.finfo(F32).min, thr)
    thr_b = jnp.broadcast_to(thr, (Q_BLOCK, LANES))

    scale = HEAD_DIM ** -0.5
    for j in range(N_KV_HEADS):
        heads = [j * n_grp + g for g in range(n_grp)]
        qj = jnp.concatenate([q_ref[:, h * HEAD_DIM:(h + 1) * HEAD_DIM] for h in heads], axis=0)

        def attn_chunk(c, carry, j=j, qj=qj):
            ms, ls, accs = carry
            c0 = pl.multiple_of(c * PA_CHUNK, PA_CHUNK)
            kc = k_ref[pl.ds(c0, PA_CHUNK), j * HEAD_DIM:(j + 1) * HEAD_DIM]
            vc = v_ref[pl.ds(c0, PA_CHUNK), j * HEAD_DIM:(j + 1) * HEAD_DIM]
            s_all = lax.dot_general(qj, kc, NT_DIMS, preferred_element_type=F32)
            sc = sc_ref[:, pl.ds(c0, PA_CHUNK)]
            sel = jnp.concatenate([sc[:, i * LANES:(i + 1) * LANES] >= thr_b for i in range(PA_CHUNK // LANES)],
                                  axis=1)
            new_m, new_l, new_acc = [], [], []
            for g in range(n_grp):
                s = jnp.where(sel, s_all[g * Q_BLOCK:(g + 1) * Q_BLOCK] * scale, NEG)
                m_new = jnp.maximum(ms[g], jnp.max(s, axis=1, keepdims=True))
                alpha = jnp.exp(ms[g] - m_new)
                p = jnp.exp(s - m_new)
                new_m.append(m_new)
                new_l.append(alpha * ls[g] + jnp.sum(p, axis=1, keepdims=True))
                new_acc.append(alpha * accs[g] + jnp.dot(p.astype(BF16), vc, preferred_element_type=F32))
            return tuple(new_m), tuple(new_l), tuple(new_acc)

        init = (tuple(jnp.full((Q_BLOCK, 1), NEG, F32) for _ in range(n_grp)),
                tuple(jnp.zeros((Q_BLOCK, 1), F32) for _ in range(n_grp)),
                tuple(jnp.zeros((Q_BLOCK, HEAD_DIM), F32) for _ in range(n_grp)))
        _, ls, accs = lax.fori_loop(0, n_ch, attn_chunk, init)
        for g, h in enumerate(heads):
            o_ref[:, h * HEAD_DIM:(h + 1) * HEAD_DIM] = (accs[g] / ls[g]).astype(BF16)


def _p_attn(p16, wi, kia, kib, batch, seq, topk):
    d_attn = 2048
    p16b = p16.reshape(batch, seq, W16)
    kv_w = N_KV_HEADS * HEAD_DIM
    return pl.pallas_call(
        functools.partial(_p_attn_kernel, topk=topk),
        out_shape=jax.ShapeDtypeStruct((batch, seq, d_attn), BF16),
        grid=(batch, seq // Q_BLOCK),
        in_specs=[pl.BlockSpec((None, Q_BLOCK, d_attn), lambda b, i: (b, i, 0)),
                  pl.BlockSpec((None, Q_BLOCK, 1024), lambda b, i: (b, i, 6)),
                  pl.BlockSpec((None, Q_BLOCK, N_IDX_HEADS), lambda b, i: (b, i, 0)),
                  pl.BlockSpec((None, seq, LANES), lambda b, i: (b, 0, 0)),
                  pl.BlockSpec((None, seq, LANES), lambda b, i: (b, 0, 0)),
                  pl.BlockSpec((None, seq, kv_w), lambda b, i: (b, 0, 14)),
                  pl.BlockSpec((None, seq, kv_w), lambda b, i: (b, 0, 15))],
        out_specs=pl.BlockSpec((None, Q_BLOCK, d_attn), lambda b, i: (b, i, 0)),
        scratch_shapes=[pltpu.VMEM((Q_BLOCK, seq), F32)],
        compiler_params=_cparams(2),
        name="p_attn",
    )(p16b, p16b, wi, kia, kib, p16b, p16b)


def _s_score_kernel(pt_ref, qi_ref, w_ref, gt_ref, kin_ref, *rest, n_pages, page, n_new):
    del pt_ref
    ik_refs, o_ref = rest[:n_pages], rest[n_pages]
    qi = qi_ref[...]
    w = w_ref[...]
    gt = gt_ref[...]

    def piece(ik):
        d_t = lax.dot_general(ik.astype(BF16), qi, NT_DIMS, preferred_element_type=F32)
        x = jnp.maximum(d_t, 0.0) * w
        hi = x.astype(BF16)
        lo = (x - hi.astype(F32)).astype(BF16)
        return (lax.dot_general(gt, hi, NT_DIMS, preferred_element_type=F32)
                + lax.dot_general(gt, lo, NT_DIMS, preferred_element_type=F32))

    for p in range(n_pages):
        o_ref[:, p * page:(p + 1) * page] = piece(ik_refs[p][...])
    new = piece(kin_ref[...])
    col = lax.broadcasted_iota(jnp.int32, new.shape, 1)
    tok = lax.broadcasted_iota(jnp.int32, new.shape, 0) % n_new
    o_ref[:, n_pages * page:(n_pages + 1) * page] = jnp.where(col <= tok, new, -jnp.inf)


def _page_specs(n_pages, shape):
    return [pl.BlockSpec((None,) + shape, functools.partial(lambda b, pt, p: (pt[b, p], 0, 0), p=p))
            for p in range(n_pages)]


def _s_score(page_table, qi_ht, w_ht, gt, ki_new_pad, cache_ik):
    nb, n_pages = page_table.shape
    page = cache_ik.shape[1]
    width = (n_pages + 1) * page
    hx = qi_ht.shape[1]
    return pl.pallas_call(
        functools.partial(_s_score_kernel, n_pages=n_pages, page=page, n_new=4),
        out_shape=jax.ShapeDtypeStruct((nb, 8, width), F32),
        grid_spec=pltpu.PrefetchScalarGridSpec(
            num_scalar_prefetch=1,
            grid=(nb,),
            in_specs=[pl.BlockSpec((None, hx, IDX_DIM), lambda b, pt: (b, 0, 0)),
                      pl.BlockSpec((None, 1, hx), lambda b, pt: (b, 0, 0)),
                      pl.BlockSpec((8, hx), lambda b, pt: (0, 0)),
                      pl.BlockSpec((None, page, IDX_DIM), lambda b, pt: (b, 0, 0))]
            + _page_specs(n_pages, (page, IDX_DIM)),
            out_specs=pl.BlockSpec((None, 8, width), lambda b, pt: (b, 0, 0))),
        compiler_params=_cparams(1),
        name="s_score",
    )(page_table, qi_ht, w_ht, gt, ki_new_pad, *([cache_ik] * n_pages))


def _s_thr_kernel(sc_ref, o_ref, *, topk, chunk):
    rows, width = sc_ref.shape

    def load_chunk(c):
        return sc_ref[:, pl.ds(pl.multiple_of(c * chunk, chunk), chunk)]

    thr = _kth_largest(load_chunk, width // chunk, rows, chunk, topk)
    o_ref[...] = jnp.broadcast_to(thr, (rows, LANES))


def _s_thr(scores, topk):
    rows, width = scores.shape
    tr = 128
    return pl.pallas_call(
        functools.partial(_s_thr_kernel, topk=topk, chunk=LANES),
        out_shape=jax.ShapeDtypeStruct((rows, LANES), F32),
        grid=(rows // tr,),
        in_specs=[pl.BlockSpec((tr, width), lambda i: (i, 0))],
        out_specs=pl.BlockSpec((tr, LANES), lambda i: (i, 0)),
        compiler_params=_cparams(1),
        name="s_thr",
    )(scores)


def _s_attn_kernel(pt_ref, q_ref, sc_ref, thr_ref, kn_ref, vn_ref, *rest, n_pages, page):
    del pt_ref
    k_refs, v_refs, o_ref = rest[:n_pages], rest[n_pages:2 * n_pages], rest[2 * n_pages]
    q = q_ref[...]
    rows = q.shape[0]
    bias8 = jnp.where(sc_ref[...] >= thr_ref[:, 0:1], 0.0, NEG)
    bias = jnp.concatenate([bias8] * (rows // 8), axis=0)
    scale = HEAD_DIM ** -0.5
    pieces = []
    for p in range(n_pages + 1):
        kp = kn_ref[...] if p == n_pages else k_refs[p][...].astype(BF16)
        s = lax.dot_general(q, kp, NT_DIMS, preferred_element_type=F32)
        pieces.append(s * scale + bias[:, p * page:(p + 1) * page])
    m = functools.reduce(jnp.maximum, [jnp.max(s, axis=1, keepdims=True) for s in pieces])
    l = jnp.zeros((rows, 1), F32)
    o = jnp.zeros((rows, q.shape[1]), F32)
    for p in range(n_pages + 1):
        e = jnp.exp(pieces[p] - m)
        l = l + jnp.sum(e, axis=1, keepdims=True)
        vp = vn_ref[...] if p == n_pages else v_refs[p][...].astype(BF16)
        o = o + jnp.dot(e.astype(BF16), vp, preferred_element_type=F32)
    o = o / l
    rj = rows // N_KV_HEADS
    for j in range(N_KV_HEADS):
        o_ref[j * rj:(j + 1) * rj, :] = o[j * rj:(j + 1) * rj, j * HEAD_DIM:(j + 1) * HEAD_DIM].astype(BF16)


def _s_attn(page_table, q_bd, scores8, thr8, k_new_pad, v_new_pad, cache_k, cache_v):
    nb, n_pages = page_table.shape
    page, kv_w = cache_k.shape[1], cache_k.shape[2]
    rows = q_bd.shape[1]
    width = scores8.shape[-1]
    return pl.pallas_call(
        functools.partial(_s_attn_kernel, n_pages=n_pages, page=page),
        out_shape=jax.ShapeDtypeStruct((nb, rows, HEAD_DIM), BF16),
        grid_spec=pltpu.PrefetchScalarGridSpec(
            num_scalar_prefetch=1,
            grid=(nb,),
            in_specs=[pl.BlockSpec((None, rows, kv_w), lambda b, pt: (b, 0, 0)),
                      pl.BlockSpec((None, 8, width), lambda b, pt: (b, 0, 0)),
                      pl.BlockSpec((None, 8, LANES), lambda b, pt: (b, 0, 0)),
                      pl.BlockSpec((None, page, kv_w), lambda b, pt: (b, 0, 0)),
                      pl.BlockSpec((None, page, kv_w), lambda b, pt: (b, 0, 0))]
            + _page_specs(n_pages, (page, kv_w)) + _page_specs(n_pages, (page, kv_w)),
            out_specs=pl.BlockSpec((None, rows, HEAD_DIM), lambda b, pt: (b, 0, 0))),
        compiler_params=_cparams(1),
        name="s_attn",
    )(page_table, q_bd, scores8, thr8, k_new_pad, v_new_pad, *([cache_k] * n_pages), *([cache_v] * n_pages))


def _mix_kernel(d_ref, a_ref, ga_ref, gb_ref, wg_ref, ps_ref, wup_ref, wua_ref, o_ref):
    n_groups, group = wg_ref.shape[0], wg_ref.shape[1]
    y = jnp.concatenate(
        [jnp.dot(d_ref[:, g * group:(g + 1) * group], wg_ref[g], preferred_element_type=F32)
         for g in range(n_groups)], axis=1) * ps_ref[...]
    yp = jnp.dot(y.astype(BF16), wup_ref[...], preferred_element_type=F32)
    ya = jnp.dot(a_ref[...], wua_ref[...], preferred_element_type=F32)
    o_ref[...] = (jax.nn.sigmoid(ga_ref[...].astype(F32)) * yp
                  + jax.nn.sigmoid(gb_ref[...].astype(F32)) * ya).astype(BF16)


def _mix(d, attn, p16, w_grp, pool_scale, w_up_pool, w_up_attn, tm):
    m, dm = attn.shape
    pw = d.shape[1]
    const = lambda *shape: pl.BlockSpec(shape, lambda i: (0,) * len(shape))
    return pl.pallas_call(
        _mix_kernel,
        out_shape=jax.ShapeDtypeStruct((m, dm), BF16),
        grid=(m // tm,),
        in_specs=[pl.BlockSpec((tm, pw), lambda i: (i, 0)),
                  pl.BlockSpec((tm, dm), lambda i: (i, 0)),
                  pl.BlockSpec((tm, dm), lambda i: (i, 1)),
                  pl.BlockSpec((tm, dm), lambda i: (i, 2)),
                  const(*w_grp.shape), const(1, pw), const(*w_up_pool.shape), const(*w_up_attn.shape)],
        out_specs=pl.BlockSpec((tm, dm), lambda i: (i, 0)),
        compiler_params=_cparams(1),
        name="mix",
    )(d, attn, p16, p16, w_grp, pool_scale, w_up_pool, w_up_attn)


def _out_kernel(mix_ref, x_ref, gate_ref, sh_ref, sc_ref, g_ref, w_ref, h_ref, hn_ref):
    h = x_ref[...] + gate_ref[...] * jnp.dot(mix_ref[...], w_ref[...], preferred_element_type=F32)
    h_ref[...] = h
    hn_ref[...] = (_rms(h, g_ref[...]) * (1.0 + sc_ref[...]) + sh_ref[...]).astype(BF16)


def _out(mix, x, gate, shift, scale, g2, w_out, tm, rows_per_mod):
    m, d = x.shape
    return pl.pallas_call(
        _out_kernel,
        out_shape=(jax.ShapeDtypeStruct((m, d), F32), jax.ShapeDtypeStruct((m, d), BF16)),
        grid=(m // tm,),
        in_specs=[pl.BlockSpec((tm, d), lambda i: (i, 0)),
                  pl.BlockSpec((tm, d), lambda i: (i, 0)),
                  _mod_spec(gate, tm, rows_per_mod),
                  _mod_spec(shift, tm, rows_per_mod),
                  _mod_spec(scale, tm, rows_per_mod),
                  pl.BlockSpec((1, d), lambda i: (0, 0)),
                  pl.BlockSpec((d, d), lambda i: (0, 0))],
        out_specs=(pl.BlockSpec((tm, d), lambda i: (i, 0)), pl.BlockSpec((tm, d), lambda i: (i, 0))),
        compiler_params=_cparams(1),
        name="out",
    )(mix, x, gate, shift, scale, g2, w_out)


FFN_TF = 512


def _ffn_kernel(hn_ref, h_ref, gate_ref, gf_ref, wg_ref, wu_ref, wo_ref, o_ref, acc_ref):
    f = pl.program_id(1)

    @pl.when(f == 0)
    def _():
        acc_ref[...] = jnp.zeros_like(acc_ref)

    hn = hn_ref[...]
    a = jnp.dot(hn, wg_ref[...], preferred_element_type=F32)
    u = jnp.dot(hn, wu_ref[...], preferred_element_type=F32)
    z = (a * jax.nn.sigmoid(a) * u).astype(BF16)
    acc_ref[...] += jnp.dot(z, wo_ref[...], preferred_element_type=F32)

    @pl.when(f == pl.num_programs(1) - 1)
    def _():
        o_ref[...] = _rms(h_ref[...] + gate_ref[...] * acc_ref[...], gf_ref[...])


def _ffn(hn, h, gate, g_final, w_ffn_in, w_ffn_out, tm, rows_per_mod):
    m, d = h.shape
    d_ff = w_ffn_out.shape[0]
    n_f = d_ff // FFN_TF
    return pl.pallas_call(
        _ffn_kernel,
        out_shape=jax.ShapeDtypeStruct((m, d), F32),
        grid=(m // tm, n_f),
        in_specs=[pl.BlockSpec((tm, d), lambda i, f: (i, 0)),
                  pl.BlockSpec((tm, d), lambda i, f: (i, 0)),
                  _mod_spec(gate, tm, rows_per_mod),
                  pl.BlockSpec((1, d), lambda i, f: (0, 0)),
                  pl.BlockSpec((d, FFN_TF), lambda i, f: (0, f)),
                  pl.BlockSpec((d, FFN_TF), lambda i, f: (0, f + n_f)),
                  pl.BlockSpec((FFN_TF, d), lambda i, f: (f, 0))],
        out_specs=pl.BlockSpec((tm, d), lambda i, f: (i, 0)),
        scratch_shapes=[pltpu.VMEM((tm, d), F32)],
        compiler_params=_cparams(2),
        name="ffn",
    )(hn, h, gate, g_final, w_ffn_in, w_ffn_in, w_ffn_out)


def _back(x2, p16, d, attn, mods, lw, tm, rows_per_mod):
    mix = _mix(d, attn, p16, lw["w_grp"], lw["pool_scale"], lw["w_up_pool"], lw["w_up_attn"], tm)
    h, hn = _out(mix, x2, mods[2], mods[3], mods[4], lw["g2"], lw["w_out"], tm, rows_per_mod)
    return _ffn(hn, h, mods[5], lw["g_final"], lw["w_ffn_in"], lw["w_ffn_out"], tm, rows_per_mod)


def kernel(x_prompt, x_sample, cache_k, cache_v, cache_idx_k, state_pool, page_table, c_prompt, c_sample,
           w_ada, b_ada, g_norm1, w_in, w_pool_grp, pool_scale, w_up_pool, w_up_attn, w_out, g_norm2,
           w_ffn_in, w_ffn_out, g_final):
    batch, seq, dm = x_prompt.shape
    nb, n_new, _ = x_sample.shape
    depth = w_ada.shape[0]
    assert depth == 1, "single-layer step"
    n_phys, page = cache_k.shape[1], cache_k.shape[2]
    n_pages = page_table.shape[1]
    past = n_pages * page
    pool_w = state_pool.shape[-1]
    kv_w = N_KV_HEADS * HEAD_DIM
    n_heads = dm // HEAD_DIM
    n_grp = n_heads // N_KV_HEADS
    idx_w = N_IDX_HEADS * IDX_DIM
    assert page == LANES and n_new == 4 and seq % PA_CHUNK == 0

    w = w_in[0]
    o_pool, o_q, o_k, o_v = 0, pool_w, pool_w + dm, pool_w + dm + kv_w
    o_qi = o_v + kv_w
    o_ki = o_qi + idx_w
    o_ga = o_ki + IDX_DIM + N_IDX_HEADS
    o_gb = o_ga + dm
    small = w[:, o_ki:o_ga]
    w_cat = jnp.concatenate(
        [w[:, o_q:o_k], w[:, o_ga:o_gb], w[:, o_gb:o_gb + dm], w[:, o_qi:o_ki], w[:, o_pool:o_q],
         w[:, o_k:o_v], w[:, o_v:o_qi], small, jnp.zeros((dm, FRONT_TN - small.shape[1]), w.dtype)],
        axis=1).astype(BF16)
    assert w_cat.shape[1] == N_FRONT_TILES * FRONT_TN
    lw = {"w_grp": w_pool_grp[0].astype(BF16), "pool_scale": pool_scale[0].reshape(1, pool_w),
          "w_up_pool": w_up_pool[0].astype(BF16), "w_up_attn": w_up_attn[0].astype(BF16),
          "w_out": w_out[0].astype(BF16), "g2": g_norm2[0].reshape(1, dm), "g_final": g_final.reshape(1, dm),
          "w_ffn_in": w_ffn_in[0].astype(BF16), "w_ffn_out": w_ffn_out[0].astype(BF16)}
    g1 = g_norm1[0].reshape(1, dm)

    n_c = batch + nb
    c_all = jnp.concatenate([c_prompt, c_sample, jnp.zeros((-n_c % 8, dm), F32)], axis=0)
    mod = _ada(c_all, w_ada[0], b_ada[0])
    mods_p = [mod[:batch, i * dm:(i + 1) * dm].reshape(batch, 1, dm) for i in range(6)]
    mods_s = [jnp.repeat(mod[batch:n_c, i * dm:(i + 1) * dm], n_new, axis=0) for i in range(6)]

    xp = x_prompt.reshape(batch * seq, dm)
    p16, p32 = _front(xp, g1, mods_p[0], mods_p[1], w_cat, 1024, seq)
    p32b = p32.reshape(batch, seq, W32)
    k_prompt = p32b[:, :, pool_w:pool_w + kv_w].reshape(1, batch, seq, N_KV_HEADS, HEAD_DIM)
    v_prompt = p32b[:, :, pool_w + kv_w:pool_w + 2 * kv_w].reshape(1, batch, seq, N_KV_HEADS, HEAD_DIM)
    o_small = pool_w + 2 * kv_w
    ki_p = p32b[:, :, o_small:o_small + IDX_DIM]
    wi_p = p32b[:, :, o_small + IDX_DIM:o_small + IDX_DIM + N_IDX_HEADS]
    pool_prompt = p32b[:, seq - POOL_BUF:, :pool_w][None]
    ki16 = ki_p.astype(BF16)
    zeros = jnp.zeros_like(ki16)
    kia = jnp.concatenate([ki16, zeros], axis=-1)
    kib = jnp.concatenate([zeros, ki16], axis=-1)
    d_p = _pool_d_prompt(p32, batch, seq, pool_w).reshape(batch * seq, pool_w)
    attn_p = _p_attn(p16, wi_p, kia, kib, batch, seq, min(TOPK_MAX, seq // 4)).reshape(batch * seq, dm)
    y_prompt = _back(xp, p16, d_p, attn_p, mods_p, lw, 512, seq).reshape(batch, seq, dm)

    xs = x_sample.reshape(nb * n_new, dm)
    s16, s32 = _front(xs, g1, mods_s[0], mods_s[1], w_cat, nb * n_new, 0)
    s32b = s32.reshape(nb, n_new, W32)
    k_new = s32b[:, :, pool_w:pool_w + kv_w]
    v_new = s32b[:, :, pool_w + kv_w:pool_w + 2 * kv_w]
    ki_new = s32b[:, :, o_small:o_small + IDX_DIM]
    wi_s = s32b[:, :, o_small + IDX_DIM:o_small + IDX_DIM + N_IDX_HEADS]
    seq_s = jnp.concatenate([state_pool[0], s32b[:, :, :pool_w]], axis=1)
    pool_sample = seq_s[:, n_new:][None]
    d_s = _pool_d_sample(jnp.swapaxes(seq_s, 0, 1), n_new)
    d_s = jnp.swapaxes(d_s, 0, 1).reshape(nb * n_new, pool_w)

    s16b = s16.reshape(nb, n_new, W16)
    hx = N_IDX_HEADS * n_new
    qi_ht = s16b[:, :, 3 * dm:3 * dm + idx_w].reshape(nb, n_new, N_IDX_HEADS, IDX_DIM)
    qi_ht = jnp.swapaxes(qi_ht, 1, 2).reshape(nb, hx, IDX_DIM)
    w_ht = (jnp.swapaxes(wi_s, 1, 2) * (IDX_DIM ** -0.5 * N_IDX_HEADS ** -0.5)).reshape(nb, 1, hx)
    gt = (jnp.arange(hx)[None, :] % n_new == jnp.arange(8)[:, None] % n_new).astype(BF16)
    ki_new_pad = jnp.pad(ki_new, ((0, 0), (0, page - n_new), (0, 0)))
    cache_ik = cache_idx_k[0]
    scores8 = _s_score(page_table, qi_ht, w_ht, gt, ki_new_pad, cache_ik)
    topk_s = min(TOPK_MAX, (past + n_new) // 4)
    thr8 = _s_thr(scores8.reshape(nb * 8, scores8.shape[-1]), topk_s).reshape(nb, 8, LANES)

    q_s = s16b[:, :, :dm].reshape(nb, n_new, N_KV_HEADS, n_grp, HEAD_DIM)
    q_s = jnp.transpose(q_s, (0, 2, 3, 1, 4))
    eye = jnp.eye(N_KV_HEADS, dtype=BF16)
    q_bd = jnp.einsum("bjgtd,jk->bjgtkd", q_s, eye).reshape(nb, n_heads * n_new, kv_w)
    k_new_pad = jnp.pad(k_new.astype(BF16), ((0, 0), (0, page - n_new), (0, 0)))
    v_new_pad = jnp.pad(v_new.astype(BF16), ((0, 0), (0, page - n_new), (0, 0)))
    ck = cache_k[0].reshape(n_phys, page, kv_w)
    cv = cache_v[0].reshape(n_phys, page, kv_w)
    o_s = _s_attn(page_table, q_bd, scores8, thr8, k_new_pad, v_new_pad, ck, cv)
    attn_s = jnp.swapaxes(o_s.reshape(nb, n_heads, n_new, HEAD_DIM), 1, 2).reshape(nb * n_new, dm)
    y_sample = _back(xs, s16, d_s, attn_s, mods_s, lw, nb * n_new, 0).reshape(nb, n_new, dm)

    k_sample = k_new.reshape(1, nb, n_new, N_KV_HEADS, HEAD_DIM)
    v_sample = v_new.reshape(1, nb, n_new, N_KV_HEADS, HEAD_DIM)
    return (y_prompt, y_sample, k_prompt, v_prompt, ki_p[None], pool_prompt,
            k_sample, v_sample, ki_new[None], pool_sample)
```

```python
import functools
import math

import jax
import jax.numpy as jnp
from jax import lax
from jax.experimental import pallas as pl
from jax.experimental.pallas import tpu as pltpu

F32 = jnp.float32
BF16 = jnp.bfloat16

RMS_EPS = 1e-6
POOL_WINDOWS = (2, 4, 8, 16)
POOL_BUF = max(POOL_WINDOWS) - 1
HEAD_DIM = 128
N_KV_HEADS = 4
N_IDX_HEADS = 16
IDX_DIM = 64
TOPK_MAX = 256
Q_BLOCK = 128
LANES = 128
NEG = -1e30
NT_DIMS = (((1,), (1,)), ((), ()))
VMEM_LIMIT = 56 * 1024 * 1024


def _cparams(n_axes):
    return pltpu.CompilerParams(dimension_semantics=("arbitrary",) * n_axes, vmem_limit_bytes=VMEM_LIMIT)


def _rms(x, g):
    return x * lax.rsqrt(jnp.mean(x * x, axis=-1, keepdims=True) + RMS_EPS) * g


def _ada_kernel(c_ref, w_ref, b_ref, o_ref):
    c = c_ref[...]
    s = (c * jax.nn.sigmoid(c)).astype(BF16)
    o_ref[...] = jnp.dot(s, w_ref[...].astype(BF16), preferred_element_type=F32) + b_ref[...]


def _ada(c_all, w_ada, b_ada):
    rows, d = c_all.shape
    n = w_ada.shape[1]
    tn = 512
    return pl.pallas_call(
        _ada_kernel,
        out_shape=jax.ShapeDtypeStruct((rows, n), F32),
        grid=(n // tn,),
        in_specs=[pl.BlockSpec((rows, d), lambda j: (0, 0)),
                  pl.BlockSpec((d, tn), lambda j: (0, j)),
                  pl.BlockSpec((1, tn), lambda j: (0, j))],
        out_specs=pl.BlockSpec((rows, tn), lambda j: (0, j)),
        compiler_params=_cparams(1),
        name="ada",
    )(c_all, w_ada, b_ada.reshape(1, n))


FRONT_TN = 512
N_FRONT_TILES = 19
W16 = 16 * FRONT_TN
W32 = 5 * FRONT_TN


def _front_kernel(x_ref, g_ref, sh_ref, sc_ref, w_ref, o16_ref, o32_ref, u_ref):
    n = pl.program_id(1)

    @pl.when(n == 0)
    def _():
        u_ref[...] = (_rms(x_ref[...], g_ref[...]) * (1.0 + sc_ref[...]) + sh_ref[...]).astype(BF16)

    r = jnp.dot(u_ref[...], w_ref[...], preferred_element_type=F32)

    @pl.when((n < 14) | (n == 16) | (n == 17))
    def _():
        o16_ref[...] = r.astype(BF16)

    @pl.when(n >= 14)
    def _():
        o32_ref[...] = r


def _o16_block(n):
    return jnp.where(n < 14, n, jnp.where(n < 16, 13, jnp.where(n < 18, n - 2, 15)))


def _mod_spec(mod, tm, rows_per_mod):
    d = mod.shape[-1]
    if mod.ndim == 3:
        tiles_per_batch = rows_per_mod // tm
        return pl.BlockSpec((None, 1, d), lambda i, *_: (i // tiles_per_batch, 0, 0))
    return pl.BlockSpec((tm, d), lambda i, *_: (i, 0))


def _front(x, g1, shift, scale, w_cat, tm, rows_per_mod):
    m, d = x.shape
    return pl.pallas_call(
        _front_kernel,
        out_shape=(jax.ShapeDtypeStruct((m, W16), BF16), jax.ShapeDtypeStruct((m, W32), F32)),
        grid=(m // tm, N_FRONT_TILES),
        in_specs=[pl.BlockSpec((tm, d), lambda i, n: (i, 0)),
                  pl.BlockSpec((1, d), lambda i, n: (0, 0)),
                  _mod_spec(shift, tm, rows_per_mod),
                  _mod_spec(scale, tm, rows_per_mod),
                  pl.BlockSpec((d, FRONT_TN), lambda i, n: (0, n))],
        out_specs=(pl.BlockSpec((tm, FRONT_TN), lambda i, n: (i, _o16_block(n))),
                   pl.BlockSpec((tm, FRONT_TN), lambda i, n: (i, jnp.maximum(n - 14, 0)))),
        scratch_shapes=[pltpu.VMEM((tm, d), BF16)],
        compiler_params=_cparams(2),
        name="front",
    )(x, g1, shift, scale, w_cat)


def _pool_d_prompt_kernel(pin_ref, halo_ref, o_ref, buf_ref, *, tp, group):
    i = pl.program_id(1)
    cur = pin_ref[...]
    halo = jnp.where(i > 0, halo_ref[...], 0.0)
    hb = halo.shape[0]
    buf_ref[0:hb, :] = halo
    buf_ref[hb:hb + tp, :] = cur
    pos = i * tp + lax.broadcasted_iota(jnp.int32, (tp, 1), 0)
    for gi, w in enumerate(POOL_WINDOWS):
        sl = slice(gi * group, (gi + 1) * group)
        acc = cur[:, sl]
        for j in range(1, w):
            acc = acc + buf_ref[hb - j:hb - j + tp, sl]
        cnt = jnp.minimum(pos + 1, w).astype(F32)
        o_ref[:, sl] = (acc / cnt - cur[:, sl]).astype(BF16)


def _pool_d_prompt(p32, batch, seq, width):
    tp, hb = 512, 16
    group = width // len(POOL_WINDOWS)
    p32b = p32.reshape(batch, seq, p32.shape[-1])
    return pl.pallas_call(
        functools.partial(_pool_d_prompt_kernel, tp=tp, group=group),
        out_shape=jax.ShapeDtypeStruct((batch, seq, width), BF16),
        grid=(batch, seq // tp),
        in_specs=[pl.BlockSpec((None, tp, width), lambda b, i: (b, i, 0)),
                  pl.BlockSpec((None, hb, width), lambda b, i: (b, jnp.maximum(i * (tp // hb) - 1, 0), 0))],
        out_specs=pl.BlockSpec((None, tp, width), lambda b, i: (b, i, 0)),
        scratch_shapes=[pltpu.VMEM((tp + hb, width), F32)],
        compiler_params=_cparams(2),
        name="pool_d_prompt",
    )(p32b, p32b)


def _pool_d_sample_kernel(seq_ref, o_ref, *, n_new, group):
    for t in range(n_new):
        r = POOL_BUF + t
        for gi, w in enumerate(POOL_WINDOWS):
            sl = slice(gi * group, (gi + 1) * group)
            acc = seq_ref[r, :, sl]
            for j in range(1, w):
                acc = acc + seq_ref[r - j, :, sl]
            o_ref[t, :, sl] = (acc / float(w) - seq_ref[r, :, sl]).astype(BF16)


def _pool_d_sample(seq_t, n_new):
    rows, nb, width = seq_t.shape
    group = width // len(POOL_WINDOWS)
    return pl.pallas_call(
        functools.partial(_pool_d_sample_kernel, n_new=n_new, group=group),
        out_shape=jax.ShapeDtypeStruct((n_new, nb, width), BF16),
        grid=(1,),
        in_specs=[pl.BlockSpec((rows, nb, width), lambda i: (0, 0, 0))],
        out_specs=pl.BlockSpec((n_new, nb, width), lambda i: (0, 0, 0)),
        compiler_params=_cparams(1),
        name="pool_d_sample",
    )(seq_t)


def _key_to_float(key):
    bits = key ^ ((key >> 31) & jnp.int32(0x7FFFFFFF))
    return lax.bitcast_convert_type(bits, F32)


def _kth_largest(load_chunk, n_chunks, rows, chunk, k):
    int_min = jnp.int32(-(2 ** 31))

    def bit_body(it, key):
        cand = key ^ lax.shift_left(jnp.int32(1), 31 - it)
        cf = jnp.broadcast_to(_key_to_float(cand), (rows, LANES))

        def chunk_body(c, cnt):
            s = load_chunk(c)
            for j in range(chunk // LANES):
                cnt = cnt + jnp.where(s[:, j * LANES:(j + 1) * LANES] >= cf, 1, 0)
            return cnt

        cnt = lax.fori_loop(0, n_chunks, chunk_body, jnp.zeros((rows, LANES), jnp.int32))
        tot = jnp.sum(cnt, axis=1, keepdims=True)
        return jnp.where(tot >= k, cand, key)

    key = lax.fori_loop(0, 32, bit_body, jnp.full((rows, 1), int_min, jnp.int32))
    return _key_to_float(key)


def _kth_largest_cols(load_chunk, n_chunks, chunk, k):
    int_min = jnp.int32(-(2 ** 31))
    n_acc = 4

    def bit_body(it, key):
        cand = key ^ lax.shift_left(jnp.int32(1), 31 - it)
        cf = jnp.broadcast_to(_key_to_float(cand), (8, LANES))

        def chunk_body(c, cnts):
            s = load_chunk(c)
            cnts = list(cnts)
            for r in range(chunk // 8):
                cnts[r % n_acc] = cnts[r % n_acc] + jnp.where(s[r * 8:(r + 1) * 8] >= cf, 1, 0)
            return tuple(cnts)

        cnts = lax.fori_loop(0, n_chunks, chunk_body,
                             tuple(jnp.zeros((8, LANES), jnp.int32) for _ in range(n_acc)))
        tot = jnp.sum(functools.reduce(jnp.add, cnts), axis=0, keepdims=True)
        return jnp.where(tot >= k, cand, key)

    key = lax.fori_loop(0, 32, bit_body, jnp.full((1, LANES), int_min, jnp.int32))
    return _key_to_float(key)


PA_CHUNK = 512


def _p_attn_kernel(q_ref, qi_ref, wit_ref, kia_ref, kib_ref, k_ref, vt_ref, o_ref, sc_ref, *, topk):
    blk = pl.program_id(1)
    t0 = blk * Q_BLOCK
    n_ch = (t0 + Q_BLOCK + PA_CHUNK - 1) // PA_CHUNK
    tok = t0 + lax.broadcasted_iota(jnp.int32, (1, Q_BLOCK), 1)
    n_grp = (q_ref.shape[1] // HEAD_DIM) // N_KV_HEADS
    n_pairs = N_IDX_HEADS // 2

    wit = wit_ref[...] * (IDX_DIM ** -0.5 * N_IDX_HEADS ** -0.5)

    def score_chunk(c, carry):
        c0 = pl.multiple_of(c * PA_CHUNK, PA_CHUNK)
        ka = kia_ref[pl.ds(c0, PA_CHUNK), :]
        kb = kib_ref[pl.ds(c0, PA_CHUNK), :]
        acc = jnp.zeros((PA_CHUNK, Q_BLOCK), F32)
        for pp in range(n_pairs // 2):
            qp = jnp.concatenate([qi_ref[:, (2 * pp + i) * LANES:(2 * pp + i + 1) * LANES] for i in range(2)], axis=0)
            da = lax.dot_general(ka, qp, NT_DIMS, preferred_element_type=F32)
            db = lax.dot_general(kb, qp, NT_DIMS, preferred_element_type=F32)
            for i in range(2):
                h = 2 * (2 * pp + i)
                sl = slice(i * Q_BLOCK, (i + 1) * Q_BLOCK)
                acc = acc + jnp.maximum(da[:, sl], 0.0) * wit[h:h + 1, :]
                acc = acc + jnp.maximum(db[:, sl], 0.0) * wit[h + 1:h + 2, :]
        keypos = c0 + lax.broadcasted_iota(jnp.int32, (PA_CHUNK, Q_BLOCK), 0)
        sc_ref[pl.ds(c0, PA_CHUNK), :] = jnp.where(keypos <= tok, acc, -jnp.inf)
        return carry

    lax.fori_loop(0, n_ch, score_chunk, 0)

    def load_chunk(c):
        return sc_ref[pl.ds(pl.multiple_of(c * PA_CHUNK, PA_CHUNK), PA_CHUNK), :]

    thr = _kth_largest_cols(load_chunk, n_ch, PA_CHUNK, topk)
    thr = jnp.where(tok + 1 <= topk, jnp.finfo(F32).min, thr)

    c_exp = HEAD_DIM ** -0.5 * math.log2(math.e)
    width = n_grp * Q_BLOCK

    def attn_chunk(c, carry):
        c0 = pl.multiple_of(c * PA_CHUNK, PA_CHUNK)
        bias = jnp.where(sc_ref[pl.ds(c0, PA_CHUNK), :] >= thr, 0.0, NEG)
        bias = jnp.concatenate([bias] * n_grp, axis=1)
        out = []
        for j, (m, l, acc) in enumerate(carry):
            kv = slice(j * HEAD_DIM, (j + 1) * HEAD_DIM)
            qj = jnp.concatenate([q_ref[:, (j * n_grp + g) * HEAD_DIM:(j * n_grp + g + 1) * HEAD_DIM]
                                  for g in range(n_grp)], axis=0)
            s = lax.dot_general(k_ref[pl.ds(c0, PA_CHUNK), kv], qj, NT_DIMS, preferred_element_type=F32) + bias
            m_new = jnp.maximum(m, jnp.max(s, axis=0, keepdims=True))
            alpha = jnp.exp2((m - m_new) * c_exp)
            p = jnp.exp2((s - m_new) * c_exp)
            l = alpha * l + jnp.sum(p, axis=0, keepdims=True)
            acc = alpha * acc + jnp.dot(vt_ref[kv, pl.ds(c0, PA_CHUNK)], p.astype(BF16),
                                        preferred_element_type=F32)
            out.append((m_new, l, acc))
        return tuple(out)

    init = tuple((jnp.full((1, width), NEG, F32), jnp.zeros((1, width), F32), jnp.zeros((HEAD_DIM, width), F32))
                 for _ in range(N_KV_HEADS))
    final = lax.fori_loop(0, n_ch, attn_chunk, init)
    for j, (_, l, acc) in enumerate(final):
        o_t = acc / l
        for g in range(n_grp):
            h = j * n_grp + g
            o_ref[:, h * HEAD_DIM:(h + 1) * HEAD_DIM] = o_t[:, g * Q_BLOCK:(g + 1) * Q_BLOCK].T.astype(BF16)


def _p_attn(p16, wit, kia, kib, vt, batch, seq, topk):
    d_attn = 2048
    p16b = p16.reshape(batch, seq, W16)
    kv_w = N_KV_HEADS * HEAD_DIM
    return pl.pallas_call(
        functools.partial(_p_attn_kernel, topk=topk),
        out_shape=jax.ShapeDtypeStruct((batch, seq, d_attn), BF16),
        grid=(batch, seq // Q_BLOCK),
        in_specs=[pl.BlockSpec((None, Q_BLOCK, d_attn), lambda b, i: (b, i, 0)),
                  pl.BlockSpec((None, Q_BLOCK, 1024), lambda b, i: (b, i, 6)),
                  pl.BlockSpec((None, N_IDX_HEADS, Q_BLOCK), lambda b, i: (b, 0, i)),
                  pl.BlockSpec((None, seq, LANES), lambda b, i: (b, 0, 0)),
                  pl.BlockSpec((None, seq, LANES), lambda b, i: (b, 0, 0)),
                  pl.BlockSpec((None, seq, kv_w), lambda b, i: (b, 0, 14)),
                  pl.BlockSpec((None, kv_w, seq), lambda b, i: (b, 0, 0))],
        out_specs=pl.BlockSpec((None, Q_BLOCK, d_attn), lambda b, i: (b, i, 0)),
        scratch_shapes=[pltpu.VMEM((seq, Q_BLOCK), F32)],
        compiler_params=_cparams(2),
        name="p_attn",
    )(p16b, p16b, wit, kia, kib, p16b, vt)


def _s_score_kernel(pt_ref, qi_ref, w_ref, kin_ref, *rest, n_pages, page, n_new):
    del pt_ref
    ik_refs, o_ref = rest[:n_pages], rest[n_pages]
    qi = qi_ref[...]
    w = w_ref[...]

    def piece(ik_t):
        d = jnp.dot(qi, ik_t.astype(BF16), preferred_element_type=F32)
        x = jnp.maximum(d, 0.0) * w
        rows = [jnp.sum(x[t * N_IDX_HEADS:(t + 1) * N_IDX_HEADS], axis=0, keepdims=True) for t in range(n_new)]
        return jnp.concatenate(rows + rows, axis=0)

    for p in range(n_pages):
        o_ref[:, p * page:(p + 1) * page] = piece(ik_refs[p][...])
    new = piece(kin_ref[...])
    col = lax.broadcasted_iota(jnp.int32, new.shape, 1)
    tok = lax.broadcasted_iota(jnp.int32, new.shape, 0) % n_new
    o_ref[:, n_pages * page:(n_pages + 1) * page] = jnp.where(col <= tok, new, -jnp.inf)


def _page_specs(n_pages, shape):
    zeros = (0,) * len(shape)
    return [pl.BlockSpec((None,) + shape, functools.partial(lambda b, pt, p: (pt[b, p],) + zeros, p=p))
            for p in range(n_pages)]


def _s_score(page_table, qi_th, w_th, ki_new_t, cache_ik_t, n_new):
    nb, n_pages = page_table.shape
    page = cache_ik_t.shape[2]
    width = (n_pages + 1) * page
    hx = qi_th.shape[1]
    return pl.pallas_call(
        functools.partial(_s_score_kernel, n_pages=n_pages, page=page, n_new=n_new),
        out_shape=jax.ShapeDtypeStruct((nb, 2 * n_new, width), F32),
        grid_spec=pltpu.PrefetchScalarGridSpec(
            num_scalar_prefetch=1,
            grid=(nb,),
            in_specs=[pl.BlockSpec((None, hx, IDX_DIM), lambda b, pt: (b, 0, 0)),
                      pl.BlockSpec((None, hx, 1), lambda b, pt: (b, 0, 0)),
                      pl.BlockSpec((None, IDX_DIM, page), lambda b, pt: (b, 0, 0))]
            + _page_specs(n_pages, (IDX_DIM, page)),
            out_specs=pl.BlockSpec((None, 2 * n_new, width), lambda b, pt: (b, 0, 0))),
        compiler_params=_cparams(1),
        name="s_score",
    )(page_table, qi_th, w_th, ki_new_t, *([cache_ik_t] * n_pages))


def _s_thr_kernel(sc_ref, o_ref, *, topk, chunk):
    rows, width = sc_ref.shape

    def load_chunk(c):
        return sc_ref[:, pl.ds(pl.multiple_of(c * chunk, chunk), chunk)]

    thr = _kth_largest(load_chunk, width // chunk, rows, chunk, topk)
    o_ref[...] = jnp.broadcast_to(thr, (rows, LANES))


def _s_thr(scores, topk):
    rows, width = scores.shape
    tr = 256
    return pl.pallas_call(
        functools.partial(_s_thr_kernel, topk=topk, chunk=LANES),
        out_shape=jax.ShapeDtypeStruct((rows, LANES), F32),
        grid=(rows // tr,),
        in_specs=[pl.BlockSpec((tr, width), lambda i: (i, 0))],
        out_specs=pl.BlockSpec((tr, LANES), lambda i: (i, 0)),
        compiler_params=_cparams(1),
        name="s_thr",
    )(scores)


def _s_attn_kernel(pt_ref, q_ref, sc_ref, thr_ref, kn_ref, vn_ref, *rest, n_pages, n_grp, n_new):
    del pt_ref
    k_refs, v_refs, o_ref = rest[:n_pages], rest[n_pages:2 * n_pages], rest[2 * n_pages]
    q = q_ref[...]
    rows = q.shape[0]
    pr = kn_ref.shape[0]
    sel8 = jnp.where(sc_ref[...] >= thr_ref[:, 0:1], 0.0, NEG)
    lane_kv = lax.broadcasted_iota(jnp.int32, (rows, pr), 1) % N_KV_HEADS
    row_kv = lax.broadcasted_iota(jnp.int32, (rows, pr), 0) // (n_grp * n_new)
    head_bias = jnp.where(lane_kv == row_kv, 0.0, NEG)
    scale = HEAD_DIM ** -0.5
    pieces = []
    for p in range(n_pages + 1):
        kp = kn_ref[...] if p == n_pages else k_refs[p][...].astype(BF16)
        s = lax.dot_general(q, kp, NT_DIMS, preferred_element_type=F32)
        sel = jnp.concatenate([sel8[:, p * pr:(p + 1) * pr]] * (rows // 8), axis=0)
        pieces.append(s * scale + (sel + head_bias))
    m = functools.reduce(jnp.maximum, [jnp.max(s, axis=1, keepdims=True) for s in pieces])
    l = jnp.zeros((rows, 1), F32)
    o = jnp.zeros((rows, HEAD_DIM), F32)
    for p in range(n_pages + 1):
        e = jnp.exp(pieces[p] - m)
        l = l + jnp.sum(e, axis=1, keepdims=True)
        vp = vn_ref[...] if p == n_pages else v_refs[p][...].astype(BF16)
        o = o + jnp.dot(e.astype(BF16), vp, preferred_element_type=F32)
    o_ref[...] = (o / l).astype(BF16)


def _s_attn(page_table, q_ht, scores_rep, thr8, k_new2d, v_new2d, cache_k2d, cache_v2d, n_grp, n_new):
    nb, n_pages = page_table.shape
    pr = cache_k2d.shape[1]
    rows = q_ht.shape[1]
    width = scores_rep.shape[-1]
    return pl.pallas_call(
        functools.partial(_s_attn_kernel, n_pages=n_pages, n_grp=n_grp, n_new=n_new),
        out_shape=jax.ShapeDtypeStruct((nb, rows, HEAD_DIM), BF16),
        grid_spec=pltpu.PrefetchScalarGridSpec(
            num_scalar_prefetch=1,
            grid=(nb,),
            in_specs=[pl.BlockSpec((None, rows, HEAD_DIM), lambda b, pt: (b, 0, 0)),
                      pl.BlockSpec((None, 8, width), lambda b, pt: (b, 0, 0)),
                      pl.BlockSpec((None, 8, LANES), lambda b, pt: (b, 0, 0)),
                      pl.BlockSpec((None, pr, HEAD_DIM), lambda b, pt: (b, 0, 0)),
                      pl.BlockSpec((None, pr, HEAD_DIM), lambda b, pt: (b, 0, 0))]
            + _page_specs(n_pages, (pr, HEAD_DIM)) + _page_specs(n_pages, (pr, HEAD_DIM)),
            out_specs=pl.BlockSpec((None, rows, HEAD_DIM), lambda b, pt: (b, 0, 0))),
        compiler_params=_cparams(1),
        name="s_attn",
    )(page_table, q_ht, scores_rep, thr8, k_new2d, v_new2d, *([cache_k2d] * n_pages), *([cache_v2d] * n_pages))


def _mix_kernel(d_ref, a_ref, ga_ref, gb_ref, wg_ref, ps_ref, wup_ref, wua_ref, o_ref):
    n_groups, group = wg_ref.shape[0], wg_ref.shape[1]
    y = jnp.concatenate(
        [jnp.dot(d_ref[:, g * group:(g + 1) * group], wg_ref[g], preferred_element_type=F32)
         for g in range(n_groups)], axis=1) * ps_ref[...]
    yp = jnp.dot(y.astype(BF16), wup_ref[...], preferred_element_type=F32)
    ya = jnp.dot(a_ref[...], wua_ref[...], preferred_element_type=F32)
    o_ref[...] = (jax.nn.sigmoid(ga_ref[...].astype(F32)) * yp
                  + jax.nn.sigmoid(gb_ref[...].astype(F32)) * ya).astype(BF16)


def _mix(d, attn, p16, w_grp, pool_scale, w_up_pool, w_up_attn, tm):
    m, dm = attn.shape
    pw = d.shape[1]
    const = lambda *shape: pl.BlockSpec(shape, lambda i: (0,) * len(shape))
    return pl.pallas_call(
        _mix_kernel,
        out_shape=jax.ShapeDtypeStruct((m, dm), BF16),
        grid=(m // tm,),
        in_specs=[pl.BlockSpec((tm, pw), lambda i: (i, 0)),
                  pl.BlockSpec((tm, dm), lambda i: (i, 0)),
                  pl.BlockSpec((tm, dm), lambda i: (i, 1)),
                  pl.BlockSpec((tm, dm), lambda i: (i, 2)),
                  const(*w_grp.shape), const(1, pw), const(*w_up_pool.shape), const(*w_up_attn.shape)],
        out_specs=pl.BlockSpec((tm, dm), lambda i: (i, 0)),
        compiler_params=_cparams(1),
        name="mix",
    )(d, attn, p16, p16, w_grp, pool_scale, w_up_pool, w_up_attn)


def _out_kernel(mix_ref, x_ref, gate_ref, sh_ref, sc_ref, g_ref, w_ref, h_ref, hn_ref):
    h = x_ref[...] + gate_ref[...] * jnp.dot(mix_ref[...], w_ref[...], preferred_element_type=F32)
    h_ref[...] = h
    hn_ref[...] = (_rms(h, g_ref[...]) * (1.0 + sc_ref[...]) + sh_ref[...]).astype(BF16)


def _out(mix, x, gate, shift, scale, g2, w_out, tm, rows_per_mod):
    m, d = x.shape
    return pl.pallas_call(
        _out_kernel,
        out_shape=(jax.ShapeDtypeStruct((m, d), F32), jax.ShapeDtypeStruct((m, d), BF16)),
        grid=(m // tm,),
        in_specs=[pl.BlockSpec((tm, d), lambda i: (i, 0)),
                  pl.BlockSpec((tm, d), lambda i: (i, 0)),
                  _mod_spec(gate, tm, rows_per_mod),
                  _mod_spec(shift, tm, rows_per_mod),
                  _mod_spec(scale, tm, rows_per_mod),
                  pl.BlockSpec((1, d), lambda i: (0, 0)),
                  pl.BlockSpec((d, d), lambda i: (0, 0))],
        out_specs=(pl.BlockSpec((tm, d), lambda i: (i, 0)), pl.BlockSpec((tm, d), lambda i: (i, 0))),
        compiler_params=_cparams(1),
        name="out",
    )(mix, x, gate, shift, scale, g2, w_out)


FFN_TF = 512


def _ffn_kernel(hn_ref, h_ref, gate_ref, gf_ref, wg_ref, wu_ref, wo_ref, o_ref, acc_ref):
    f = pl.program_id(1)

    @pl.when(f == 0)
    def _():
        acc_ref[...] = jnp.zeros_like(acc_ref)

    hn = hn_ref[...]
    a = jnp.dot(hn, wg_ref[...], preferred_element_type=F32)
    u = jnp.dot(hn, wu_ref[...], preferred_element_type=F32)
    z = (a * jax.nn.sigmoid(a) * u).astype(BF16)
    acc_ref[...] += jnp.dot(z, wo_ref[...], preferred_element_type=F32)

    @pl.when(f == pl.num_programs(1) - 1)
    def _():
        o_ref[...] = _rms(h_ref[...] + gate_ref[...] * acc_ref[...], gf_ref[...])


def _ffn(hn, h, gate, g_final, w_ffn_in, w_ffn_out, tm, rows_per_mod):
    m, d = h.shape
    d_ff = w_ffn_out.shape[0]
    n_f = d_ff // FFN_TF
    return pl.pallas_call(
        _ffn_kernel,
        out_shape=jax.ShapeDtypeStruct((m, d), F32),
        grid=(m // tm, n_f),
        in_specs=[pl.BlockSpec((tm, d), lambda i, f: (i, 0)),
                  pl.BlockSpec((tm, d), lambda i, f: (i, 0)),
                  _mod_spec(gate, tm, rows_per_mod),
                  pl.BlockSpec((1, d), lambda i, f: (0, 0)),
                  pl.BlockSpec((d, FFN_TF), lambda i, f: (0, f)),
                  pl.BlockSpec((d, FFN_TF), lambda i, f: (0, f + n_f)),
                  pl.BlockSpec((FFN_TF, d), lambda i, f: (f, 0))],
        out_specs=pl.BlockSpec((tm, d), lambda i, f: (i, 0)),
        scratch_shapes=[pltpu.VMEM((tm, d), F32)],
        compiler_params=_cparams(2),
        name="ffn",
    )(hn, h, gate, g_final, w_ffn_in, w_ffn_in, w_ffn_out)


def _back(x2, p16, d, attn, mods, lw, tm, rows_per_mod):
    mix = _mix(d, attn, p16, lw["w_grp"], lw["pool_scale"], lw["w_up_pool"], lw["w_up_attn"], tm)
    h, hn = _out(mix, x2, mods[2], mods[3], mods[4], lw["g2"], lw["w_out"], tm, rows_per_mod)
    return _ffn(hn, h, mods[5], lw["g_final"], lw["w_ffn_in"], lw["w_ffn_out"], tm, rows_per_mod)


def kernel(x_prompt, x_sample, cache_k, cache_v, cache_idx_k, state_pool, page_table, c_prompt, c_sample,
           w_ada, b_ada, g_norm1, w_in, w_pool_grp, pool_scale, w_up_pool, w_up_attn, w_out, g_norm2,
           w_ffn_in, w_ffn_out, g_final):
    batch, seq, dm = x_prompt.shape
    nb, n_new, _ = x_sample.shape
    depth = w_ada.shape[0]
    assert depth == 1, "single-layer step"
    n_phys, page = cache_k.shape[1], cache_k.shape[2]
    n_pages = page_table.shape[1]
    past = n_pages * page
    pool_w = state_pool.shape[-1]
    kv_w = N_KV_HEADS * HEAD_DIM
    n_heads = dm // HEAD_DIM
    n_grp = n_heads // N_KV_HEADS
    idx_w = N_IDX_HEADS * IDX_DIM
    assert page == LANES and n_new == 4 and seq % PA_CHUNK == 0

    w = w_in[0]
    o_pool, o_q, o_k, o_v = 0, pool_w, pool_w + dm, pool_w + dm + kv_w
    o_qi = o_v + kv_w
    o_ki = o_qi + idx_w
    o_ga = o_ki + IDX_DIM + N_IDX_HEADS
    o_gb = o_ga + dm
    small = w[:, o_ki:o_ga]
    w_cat = jnp.concatenate(
        [w[:, o_q:o_k], w[:, o_ga:o_gb], w[:, o_gb:o_gb + dm], w[:, o_qi:o_ki], w[:, o_pool:o_q],
         w[:, o_k:o_v], w[:, o_v:o_qi], small, jnp.zeros((dm, FRONT_TN - small.shape[1]), w.dtype)],
        axis=1).astype(BF16)
    assert w_cat.shape[1] == N_FRONT_TILES * FRONT_TN
    lw = {"w_grp": w_pool_grp[0].astype(BF16), "pool_scale": pool_scale[0].reshape(1, pool_w),
          "w_up_pool": w_up_pool[0].astype(BF16), "w_up_attn": w_up_attn[0].astype(BF16),
          "w_out": w_out[0].astype(BF16), "g2": g_norm2[0].reshape(1, dm), "g_final": g_final.reshape(1, dm),
          "w_ffn_in": w_ffn_in[0].astype(BF16), "w_ffn_out": w_ffn_out[0].astype(BF16)}
    g1 = g_norm1[0].reshape(1, dm)

    n_c = batch + nb
    c_all = jnp.concatenate([c_prompt, c_sample, jnp.zeros((-n_c % 8, dm), F32)], axis=0)
    mod = _ada(c_all, w_ada[0], b_ada[0])
    mods_p = [mod[:batch, i * dm:(i + 1) * dm].reshape(batch, 1, dm) for i in range(6)]
    mods_s = [jnp.repeat(mod[batch:n_c, i * dm:(i + 1) * dm], n_new, axis=0) for i in range(6)]

    xp = x_prompt.reshape(batch * seq, dm)
    p16, p32 = _front(xp, g1, mods_p[0], mods_p[1], w_cat, 1024, seq)
    p32b = p32.reshape(batch, seq, W32)
    k_prompt = p32b[:, :, pool_w:pool_w + kv_w].reshape(1, batch, seq, N_KV_HEADS, HEAD_DIM)
    v_prompt = p32b[:, :, pool_w + kv_w:pool_w + 2 * kv_w].reshape(1, batch, seq, N_KV_HEADS, HEAD_DIM)
    o_small = pool_w + 2 * kv_w
    ki_p = p32b[:, :, o_small:o_small + IDX_DIM]
    wi_p = p32b[:, :, o_small + IDX_DIM:o_small + IDX_DIM + N_IDX_HEADS]
    pool_prompt = p32b[:, seq - POOL_BUF:, :pool_w][None]
    ki16 = ki_p.astype(BF16)
    zeros = jnp.zeros_like(ki16)
    kia = jnp.concatenate([ki16, zeros], axis=-1)
    kib = jnp.concatenate([zeros, ki16], axis=-1)
    d_p = _pool_d_prompt(p32, batch, seq, pool_w).reshape(batch * seq, pool_w)
    wit_p = jnp.swapaxes(wi_p, 1, 2)
    vt_p = jnp.swapaxes(p16.reshape(batch, seq, W16)[:, :, 15 * FRONT_TN:16 * FRONT_TN], 1, 2)
    attn_p = _p_attn(p16, wit_p, kia, kib, vt_p, batch, seq, min(TOPK_MAX, seq // 4)).reshape(batch * seq, dm)
    y_prompt = _back(xp, p16, d_p, attn_p, mods_p, lw, 512, seq).reshape(batch, seq, dm)

    xs = x_sample.reshape(nb * n_new, dm)
    s16, s32 = _front(xs, g1, mods_s[0], mods_s[1], w_cat, nb * n_new, 0)
    s32b = s32.reshape(nb, n_new, W32)
    k_new = s32b[:, :, pool_w:pool_w + kv_w]
    v_new = s32b[:, :, pool_w + kv_w:pool_w + 2 * kv_w]
    ki_new = s32b[:, :, o_small:o_small + IDX_DIM]
    wi_s = s32b[:, :, o_small + IDX_DIM:o_small + IDX_DIM + N_IDX_HEADS]
    seq_s = jnp.concatenate([state_pool[0], s32b[:, :, :pool_w]], axis=1)
    pool_sample = seq_s[:, n_new:][None]
    d_s = _pool_d_sample(jnp.swapaxes(seq_s, 0, 1), n_new)
    d_s = jnp.swapaxes(d_s, 0, 1).reshape(nb * n_new, pool_w)

    s16b = s16.reshape(nb, n_new, W16)
    hx = N_IDX_HEADS * n_new
    qi_th = s16b[:, :, 3 * dm:3 * dm + idx_w].reshape(nb, hx, IDX_DIM)
    w_th = (wi_s * (IDX_DIM ** -0.5 * N_IDX_HEADS ** -0.5)).reshape(nb, hx, 1)
    ki_new_t = jnp.swapaxes(jnp.pad(ki_new, ((0, 0), (0, page - n_new), (0, 0))), 1, 2)
    cache_ik_t = jnp.swapaxes(cache_idx_k[0], 1, 2)
    scores8 = _s_score(page_table, qi_th, w_th, ki_new_t, cache_ik_t, n_new)
    topk_s = min(TOPK_MAX, (past + n_new) // 4)
    thr4 = _s_thr(scores8[:, :n_new].reshape(nb * n_new, scores8.shape[-1]), topk_s).reshape(nb, n_new, LANES)
    thr8 = jnp.concatenate([thr4, thr4], axis=1)

    scores_rep = jnp.repeat(scores8, N_KV_HEADS, axis=2)
    q_ht = jnp.swapaxes(s16b[:, :, :dm].reshape(nb, n_new, n_heads, HEAD_DIM), 1, 2)
    q_ht = q_ht.reshape(nb, n_heads * n_new, HEAD_DIM)
    pr = page * N_KV_HEADS
    k_new2d = jnp.pad(k_new.astype(BF16), ((0, 0), (0, page - n_new), (0, 0))).reshape(nb, pr, HEAD_DIM)
    v_new2d = jnp.pad(v_new.astype(BF16), ((0, 0), (0, page - n_new), (0, 0))).reshape(nb, pr, HEAD_DIM)
    ck2d = cache_k[0].reshape(n_phys, pr, HEAD_DIM)
    cv2d = cache_v[0].reshape(n_phys, pr, HEAD_DIM)
    o_s = _s_attn(page_table, q_ht, scores_rep, thr8, k_new2d, v_new2d, ck2d, cv2d, n_grp, n_new)
    attn_s = jnp.swapaxes(o_s.reshape(nb, n_heads, n_new, HEAD_DIM), 1, 2).reshape(nb * n_new, dm)
    y_sample = _back(xs, s16, d_s, attn_s, mods_s, lw, nb * n_new, 0).reshape(nb, n_new, dm)

    k_sample = k_new.reshape(1, nb, n_new, N_KV_HEADS, HEAD_DIM)
    v_sample = v_new.reshape(1, nb, n_new, N_KV_HEADS, HEAD_DIM)
    return (y_prompt, y_sample, k_prompt, v_prompt, ki_p[None], pool_prompt,
            k_sample, v_sample, ki_new[None], pool_sample)
```

```python
import functools
import math

import jax
import jax.numpy as jnp
from jax import lax
from jax.experimental import pallas as pl
from jax.experimental.pallas import tpu as pltpu

F32 = jnp.float32
BF16 = jnp.bfloat16

RMS_EPS = 1e-6
POOL_WINDOWS = (2, 4, 8, 16)
POOL_BUF = max(POOL_WINDOWS) - 1
HEAD_DIM = 128
N_KV_HEADS = 4
N_IDX_HEADS = 16
IDX_DIM = 64
TOPK_MAX = 256
Q_BLOCK = 128
LANES = 128
NEG = -1e30
Q_SCALE = HEAD_DIM ** -0.5 * math.log2(math.e)
V_ROWS = HEAD_DIM + 16
NT_DIMS = (((1,), (1,)), ((), ()))
VMEM_LIMIT = 56 * 1024 * 1024


def _cparams(n_axes):
    return pltpu.CompilerParams(dimension_semantics=("arbitrary",) * n_axes, vmem_limit_bytes=VMEM_LIMIT)


def _rms(x, g):
    return x * lax.rsqrt(jnp.mean(x * x, axis=-1, keepdims=True) + RMS_EPS) * g


def _ada_kernel(c_ref, w_ref, b_ref, o_ref):
    c = c_ref[...]
    s = (c * jax.nn.sigmoid(c)).astype(BF16)
    o_ref[...] = jnp.dot(s, w_ref[...].astype(BF16), preferred_element_type=F32) + b_ref[...]


def _ada(c_all, w_ada, b_ada):
    rows, d = c_all.shape
    n = w_ada.shape[1]
    tn = 512
    return pl.pallas_call(
        _ada_kernel,
        out_shape=jax.ShapeDtypeStruct((rows, n), F32),
        grid=(n // tn,),
        in_specs=[pl.BlockSpec((rows, d), lambda j: (0, 0)),
                  pl.BlockSpec((d, tn), lambda j: (0, j)),
                  pl.BlockSpec((1, tn), lambda j: (0, j))],
        out_specs=pl.BlockSpec((rows, tn), lambda j: (0, j)),
        compiler_params=_cparams(1),
        name="ada",
    )(c_all, w_ada, b_ada.reshape(1, n))


FRONT_TN = 512
N_FRONT_TILES = 19
W16 = 16 * FRONT_TN
W32 = 5 * FRONT_TN


def _front_kernel(x_ref, g_ref, sh_ref, sc_ref, wt_ref, o16_ref, o32_ref, u_ref):
    n = pl.program_id(1)

    @pl.when(n == 0)
    def _():
        u_ref[...] = (_rms(x_ref[...], g_ref[...]) * (1.0 + sc_ref[...]) + sh_ref[...]).astype(BF16)

    r = lax.dot_general(u_ref[...], wt_ref[...].astype(BF16), NT_DIMS, preferred_element_type=F32)

    @pl.when((n < 14) | (n == 16) | (n == 17))
    def _():
        o16_ref[...] = (r * jnp.where(n < 4, Q_SCALE, 1.0)).astype(BF16)

    @pl.when(n >= 14)
    def _():
        o32_ref[...] = r


def _lookup(n, table):
    out = jnp.int32(table[-1])
    for i, v in enumerate(table[:-1]):
        out = jnp.where(n == i, v, out)
    return out


def _o16_block(n):
    return jnp.where(n < 14, n, jnp.where(n < 16, 13, jnp.where(n < 18, n - 2, 15)))


def _mod_spec(mod, tm, rows_per_mod):
    d = mod.shape[-1]
    if mod.ndim == 3:
        tiles_per_batch = rows_per_mod // tm
        return pl.BlockSpec((None, 1, d), lambda i, *_: (i // tiles_per_batch, 0, 0))
    return pl.BlockSpec((tm, d), lambda i, *_: (i, 0))


def _front(x, g1, shift, scale, w_t, tile_rows, tm, rows_per_mod):
    m, d = x.shape
    return pl.pallas_call(
        _front_kernel,
        out_shape=(jax.ShapeDtypeStruct((m, W16), BF16), jax.ShapeDtypeStruct((m, W32), F32)),
        grid=(m // tm, N_FRONT_TILES),
        in_specs=[pl.BlockSpec((tm, d), lambda i, n: (i, 0)),
                  pl.BlockSpec((1, d), lambda i, n: (0, 0)),
                  _mod_spec(shift, tm, rows_per_mod),
                  _mod_spec(scale, tm, rows_per_mod),
                  pl.BlockSpec((pl.Element(FRONT_TN), pl.Element(d)),
                               lambda i, n: (_lookup(n, [r // 8 for r in tile_rows]) * 8, 0))],
        out_specs=(pl.BlockSpec((tm, FRONT_TN), lambda i, n: (i, _o16_block(n))),
                   pl.BlockSpec((tm, FRONT_TN), lambda i, n: (i, jnp.maximum(n - 14, 0)))),
        scratch_shapes=[pltpu.VMEM((tm, d), BF16)],
        compiler_params=_cparams(2),
        name="front",
    )(x, g1, shift, scale, w_t)


def _pool_d_prompt_kernel(pin_ref, halo_ref, o_ref, buf_ref, *, tp, group):
    i = pl.program_id(1)
    cur = pin_ref[...]
    halo = jnp.where(i > 0, halo_ref[...], 0.0)
    hb = halo.shape[0]
    buf_ref[0:hb, :] = halo
    buf_ref[hb:hb + tp, :] = cur
    pos = i * tp + lax.broadcasted_iota(jnp.int32, (tp, 1), 0)
    for gi, w in enumerate(POOL_WINDOWS):
        sl = slice(gi * group, (gi + 1) * group)
        acc = cur[:, sl]
        for j in range(1, w):
            acc = acc + buf_ref[hb - j:hb - j + tp, sl]
        cnt = jnp.minimum(pos + 1, w).astype(F32)
        o_ref[:, sl] = (acc / cnt - cur[:, sl]).astype(BF16)


def _pool_d_prompt(p32, batch, seq, width):
    tp, hb = 512, 16
    group = width // len(POOL_WINDOWS)
    p32b = p32.reshape(batch, seq, p32.shape[-1])
    return pl.pallas_call(
        functools.partial(_pool_d_prompt_kernel, tp=tp, group=group),
        out_shape=jax.ShapeDtypeStruct((batch, seq, width), BF16),
        grid=(batch, seq // tp),
        in_specs=[pl.BlockSpec((None, tp, width), lambda b, i: (b, i, 0)),
                  pl.BlockSpec((None, hb, width), lambda b, i: (b, jnp.maximum(i * (tp // hb) - 1, 0), 0))],
        out_specs=pl.BlockSpec((None, tp, width), lambda b, i: (b, i, 0)),
        scratch_shapes=[pltpu.VMEM((tp + hb, width), F32)],
        compiler_params=_cparams(2),
        name="pool_d_prompt",
    )(p32b, p32b)


def _pool_d_sample_kernel(seq_ref, o_ref, *, n_new, group):
    for t in range(n_new):
        r = POOL_BUF + t
        for gi, w in enumerate(POOL_WINDOWS):
            sl = slice(gi * group, (gi + 1) * group)
            acc = seq_ref[r, :, sl]
            for j in range(1, w):
                acc = acc + seq_ref[r - j, :, sl]
            o_ref[t, :, sl] = (acc / float(w) - seq_ref[r, :, sl]).astype(BF16)


def _pool_d_sample(seq_t, n_new):
    rows, nb, width = seq_t.shape
    group = width // len(POOL_WINDOWS)
    return pl.pallas_call(
        functools.partial(_pool_d_sample_kernel, n_new=n_new, group=group),
        out_shape=jax.ShapeDtypeStruct((n_new, nb, width), BF16),
        grid=(1,),
        in_specs=[pl.BlockSpec((rows, nb, width), lambda i: (0, 0, 0))],
        out_specs=pl.BlockSpec((n_new, nb, width), lambda i: (0, 0, 0)),
        compiler_params=_cparams(1),
        name="pool_d_sample",
    )(seq_t)


def _key_to_float(key):
    bits = key ^ ((key >> 31) & jnp.int32(0x7FFFFFFF))
    return lax.bitcast_convert_type(bits, F32)


def _kth_largest(load_chunk, n_chunks, rows, chunk, k):
    int_min = jnp.int32(-(2 ** 31))

    def bit_body(it, key):
        cand = key ^ lax.shift_left(jnp.int32(1), 31 - it)
        cf = jnp.broadcast_to(_key_to_float(cand), (rows, LANES))

        def chunk_body(c, cnt):
            s = load_chunk(c)
            for j in range(chunk // LANES):
                cnt = cnt + jnp.where(s[:, j * LANES:(j + 1) * LANES] >= cf, 1, 0)
            return cnt

        cnt = lax.fori_loop(0, n_chunks, chunk_body, jnp.zeros((rows, LANES), jnp.int32))
        tot = jnp.sum(cnt, axis=1, keepdims=True)
        return jnp.where(tot >= k, cand, key)

    key = lax.fori_loop(0, 32, bit_body, jnp.full((rows, 1), int_min, jnp.int32))
    return _key_to_float(key)


def _kth_largest_cols(load_chunk, n_chunks, chunk, k):
    int_min = jnp.int32(-(2 ** 31))
    n_acc = 4

    def bit_body(it, key):
        cand = key ^ lax.shift_left(jnp.int32(1), 31 - it)
        cf = jnp.broadcast_to(_key_to_float(cand), (8, LANES))

        def chunk_body(c, cnts):
            s = load_chunk(c)
            cnts = list(cnts)
            for r in range(chunk // 8):
                cnts[r % n_acc] = cnts[r % n_acc] + jnp.where(s[r * 8:(r + 1) * 8] >= cf, 1, 0)
            return tuple(cnts)

        cnts = lax.fori_loop(0, n_chunks, chunk_body,
                             tuple(jnp.zeros((8, LANES), jnp.int32) for _ in range(n_acc)))
        tot = jnp.sum(functools.reduce(jnp.add, cnts), axis=0, keepdims=True)
        return jnp.where(tot >= k, cand, key)

    key = lax.fori_loop(0, 32, bit_body, jnp.full((1, LANES), int_min, jnp.int32))
    return _key_to_float(key)


PA_CHUNK = 512


def _p_attn_kernel(q_ref, qi_ref, wit_ref, kia_ref, kib_ref, k_ref, vt_ref, o_ref, sc_ref, *, topk):
    blk = pl.program_id(1)
    t0 = blk * Q_BLOCK
    n_ch = (t0 + Q_BLOCK + PA_CHUNK - 1) // PA_CHUNK
    tok = t0 + lax.broadcasted_iota(jnp.int32, (1, Q_BLOCK), 1)
    n_grp = (q_ref.shape[1] // HEAD_DIM) // N_KV_HEADS
    n_pairs = N_IDX_HEADS // 2

    wit = wit_ref[...] * (IDX_DIM ** -0.5 * N_IDX_HEADS ** -0.5)

    def score_chunk(c, carry):
        c0 = pl.multiple_of(c * PA_CHUNK, PA_CHUNK)
        ka = kia_ref[pl.ds(c0, PA_CHUNK), :]
        kb = kib_ref[pl.ds(c0, PA_CHUNK), :]
        acc = jnp.zeros((PA_CHUNK, Q_BLOCK), F32)
        for pp in range(n_pairs // 2):
            qp = jnp.concatenate([qi_ref[:, (2 * pp + i) * LANES:(2 * pp + i + 1) * LANES] for i in range(2)], axis=0)
            da = lax.dot_general(ka, qp, NT_DIMS, preferred_element_type=F32)
            db = lax.dot_general(kb, qp, NT_DIMS, preferred_element_type=F32)
            for i in range(2):
                h = 2 * (2 * pp + i)
                sl = slice(i * Q_BLOCK, (i + 1) * Q_BLOCK)
                acc = acc + jnp.maximum(da[:, sl], 0.0) * wit[h:h + 1, :]
                acc = acc + jnp.maximum(db[:, sl], 0.0) * wit[h + 1:h + 2, :]
        keypos = c0 + lax.broadcasted_iota(jnp.int32, (PA_CHUNK, Q_BLOCK), 0)
        sc_ref[pl.ds(c0, PA_CHUNK), :] = jnp.where(keypos <= tok, acc, -jnp.inf)
        return carry

    lax.fori_loop(0, n_ch, score_chunk, 0)

    def load_chunk(c):
        return sc_ref[pl.ds(pl.multiple_of(c * PA_CHUNK, PA_CHUNK), PA_CHUNK), :]

    thr = _kth_largest_cols(load_chunk, n_ch, PA_CHUNK, topk)
    thr = jnp.where(tok + 1 <= topk, jnp.finfo(F32).min, thr)

    width = n_grp * Q_BLOCK

    def attn_chunk(c, carry):
        c0 = pl.multiple_of(c * PA_CHUNK, PA_CHUNK)
        bias = jnp.where(sc_ref[pl.ds(c0, PA_CHUNK), :] >= thr, 0.0, NEG)
        bias = jnp.concatenate([bias] * n_grp, axis=1)

        def logits(j):
            qj = jnp.concatenate([q_ref[:, (j * n_grp + g) * HEAD_DIM:(j * n_grp + g + 1) * HEAD_DIM]
                                  for g in range(n_grp)], axis=0)
            kc = k_ref[pl.ds(c0, PA_CHUNK), j * HEAD_DIM:(j + 1) * HEAD_DIM]
            return lax.dot_general(kc, qj, NT_DIMS, preferred_element_type=F32) + bias

        def weights(j, s):
            m = carry[j][0]
            m_new = jnp.maximum(m, jnp.max(s, axis=0, keepdims=True))
            return m_new, jnp.exp2(m - m_new), jnp.exp2(s - m_new).astype(BF16)

        def accumulate(j, alpha, p):
            vt = vt_ref[j * V_ROWS:(j + 1) * V_ROWS, pl.ds(c0, PA_CHUNK)]
            return alpha * carry[j][1] + jnp.dot(vt, p, preferred_element_type=F32)

        s, w, out = {}, {}, []
        for step in range(N_KV_HEADS + 2):
            if step < N_KV_HEADS:
                s[step] = logits(step)
            if 0 <= step - 1 < N_KV_HEADS:
                w[step - 1] = weights(step - 1, s.pop(step - 1))
            if 0 <= step - 2 < N_KV_HEADS:
                m_new, alpha, p = w.pop(step - 2)
                out.append((m_new, accumulate(step - 2, alpha, p)))
        return tuple(out)

    init = tuple((jnp.full((1, width), NEG, F32), jnp.zeros((V_ROWS, width), F32)) for _ in range(N_KV_HEADS))
    final = lax.fori_loop(0, n_ch, attn_chunk, init)
    for j, (_, acc) in enumerate(final):
        o_t = acc[:HEAD_DIM] / acc[HEAD_DIM:HEAD_DIM + 1]
        for g in range(n_grp):
            h = j * n_grp + g
            o_ref[:, h * HEAD_DIM:(h + 1) * HEAD_DIM] = o_t[:, g * Q_BLOCK:(g + 1) * Q_BLOCK].T.astype(BF16)


def _p_attn(p16, wit, kia, kib, vt, batch, seq, topk):
    d_attn = 2048
    p16b = p16.reshape(batch, seq, W16)
    kv_w = N_KV_HEADS * HEAD_DIM
    return pl.pallas_call(
        functools.partial(_p_attn_kernel, topk=topk),
        out_shape=jax.ShapeDtypeStruct((batch, seq, d_attn), BF16),
        grid=(batch, seq // Q_BLOCK),
        in_specs=[pl.BlockSpec((None, Q_BLOCK, d_attn), lambda b, i: (b, i, 0)),
                  pl.BlockSpec((None, Q_BLOCK, 1024), lambda b, i: (b, i, 6)),
                  pl.BlockSpec((None, N_IDX_HEADS, Q_BLOCK), lambda b, i: (b, 0, i)),
                  pl.BlockSpec((None, seq, LANES), lambda b, i: (b, 0, 0)),
                  pl.BlockSpec((None, seq, LANES), lambda b, i: (b, 0, 0)),
                  pl.BlockSpec((None, seq, kv_w), lambda b, i: (b, 0, 14)),
                  pl.BlockSpec((None, N_KV_HEADS * V_ROWS, seq), lambda b, i: (b, 0, 0))],
        out_specs=pl.BlockSpec((None, Q_BLOCK, d_attn), lambda b, i: (b, i, 0)),
        scratch_shapes=[pltpu.VMEM((seq, Q_BLOCK), F32)],
        compiler_params=_cparams(2),
        name="p_attn",
    )(p16b, p16b, wit, kia, kib, p16b, vt)


def _s_score_kernel(pt_ref, qi_ref, w_ref, kin_ref, *rest, n_pages, page, n_new):
    del pt_ref
    ik_refs, o_ref = rest[:n_pages], rest[n_pages]
    qi = qi_ref[...]
    w = w_ref[...]

    def piece(ik_t):
        d = jnp.dot(qi, ik_t.astype(BF16), preferred_element_type=F32)
        x = jnp.maximum(d, 0.0) * w
        rows = [jnp.sum(x[t * N_IDX_HEADS:(t + 1) * N_IDX_HEADS], axis=0, keepdims=True) for t in range(n_new)]
        return jnp.concatenate(rows + rows, axis=0)

    for p in range(n_pages):
        o_ref[:, p * page:(p + 1) * page] = piece(ik_refs[p][...])
    new = piece(kin_ref[...])
    col = lax.broadcasted_iota(jnp.int32, new.shape, 1)
    tok = lax.broadcasted_iota(jnp.int32, new.shape, 0) % n_new
    o_ref[:, n_pages * page:(n_pages + 1) * page] = jnp.where(col <= tok, new, -jnp.inf)


def _page_specs(n_pages, shape):
    zeros = (0,) * len(shape)
    return [pl.BlockSpec((None,) + shape, functools.partial(lambda b, pt, p: (pt[b, p],) + zeros, p=p))
            for p in range(n_pages)]


def _s_score(page_table, qi_th, w_th, ki_new_t, cache_ik_t, n_new):
    nb, n_pages = page_table.shape
    page = cache_ik_t.shape[2]
    width = (n_pages + 1) * page
    hx = qi_th.shape[1]
    return pl.pallas_call(
        functools.partial(_s_score_kernel, n_pages=n_pages, page=page, n_new=n_new),
        out_shape=jax.ShapeDtypeStruct((nb, 2 * n_new, width), F32),
        grid_spec=pltpu.PrefetchScalarGridSpec(
            num_scalar_prefetch=1,
            grid=(nb,),
            in_specs=[pl.BlockSpec((None, hx, IDX_DIM), lambda b, pt: (b, 0, 0)),
                      pl.BlockSpec((None, hx, 1), lambda b, pt: (b, 0, 0)),
                      pl.BlockSpec((None, IDX_DIM, page), lambda b, pt: (b, 0, 0))]
            + _page_specs(n_pages, (IDX_DIM, page)),
            out_specs=pl.BlockSpec((None, 2 * n_new, width), lambda b, pt: (b, 0, 0))),
        compiler_params=_cparams(1),
        name="s_score",
    )(page_table, qi_th, w_th, ki_new_t, *([cache_ik_t] * n_pages))


def _s_thr_kernel(sc_ref, o_ref, *, topk, chunk):
    rows, width = sc_ref.shape

    def load_chunk(c):
        return sc_ref[:, pl.ds(pl.multiple_of(c * chunk, chunk), chunk)]

    thr = _kth_largest(load_chunk, width // chunk, rows, chunk, topk)
    o_ref[...] = jnp.broadcast_to(thr, (rows, LANES))


def _s_thr(scores, topk):
    rows, width = scores.shape
    tr = 256
    return pl.pallas_call(
        functools.partial(_s_thr_kernel, topk=topk, chunk=LANES),
        out_shape=jax.ShapeDtypeStruct((rows, LANES), F32),
        grid=(rows // tr,),
        in_specs=[pl.BlockSpec((tr, width), lambda i: (i, 0))],
        out_specs=pl.BlockSpec((tr, LANES), lambda i: (i, 0)),
        compiler_params=_cparams(1),
        name="s_thr",
    )(scores)


def _s_attn_kernel(pt_ref, q_ref, sc_ref, thr_ref, kn_ref, vn_ref, *rest, n_pages, n_grp, n_new):
    del pt_ref
    k_refs, v_refs, o_ref = rest[:n_pages], rest[n_pages:2 * n_pages], rest[2 * n_pages]
    q = q_ref[...]
    rows = q.shape[0]
    pr = k_refs[0].shape[0]
    page = pr // N_KV_HEADS
    sel = jnp.where(sc_ref[...] >= thr_ref[:, 0:1], 1.0, 0.0).astype(BF16)
    sel = jnp.concatenate([sel[:, p * page:(p + 1) * page] for p in range(n_pages + 1)], axis=0)
    spread = (lax.broadcasted_iota(jnp.int32, (page, pr), 1) // N_KV_HEADS
              == lax.broadcasted_iota(jnp.int32, (page, pr), 0)).astype(BF16)
    sel = jnp.dot(sel, spread, preferred_element_type=F32)
    lane_kv = lax.broadcasted_iota(jnp.int32, (rows, pr), 1) % N_KV_HEADS
    row_kv = lax.broadcasted_iota(jnp.int32, (rows, pr), 0) // (n_grp * n_new)
    head_bias = jnp.where(lane_kv == row_kv, 0.0, NEG)
    pieces = []
    for p in range(n_pages + 1):
        kp = kn_ref[...] if p == n_pages else k_refs[p][...].astype(BF16)
        w = kp.shape[0]
        s = lax.dot_general(q, kp, NT_DIMS, preferred_element_type=F32)
        bias = jnp.concatenate([(sel[p * 8:(p + 1) * 8, :w] - 1.0) * -NEG] * (rows // 8), axis=0)
        pieces.append(s + (bias + head_bias[:, :w]))
    m = functools.reduce(jnp.maximum, [jnp.max(s, axis=1, keepdims=True) for s in pieces])
    l = jnp.zeros((rows, 1), F32)
    o = jnp.zeros((rows, HEAD_DIM), F32)
    for p in range(n_pages + 1):
        e = jnp.exp2(pieces[p] - m)
        l = l + jnp.sum(e, axis=1, keepdims=True)
        vp = vn_ref[...] if p == n_pages else v_refs[p][...].astype(BF16)
        o = o + jnp.dot(e.astype(BF16), vp, preferred_element_type=F32)
    o_ref[...] = (o / l).astype(BF16)


def _s_attn(page_table, q_ht, scores8, thr8, k_new2d, v_new2d, cache_k2d, cache_v2d, n_grp, n_new):
    nb, n_pages = page_table.shape
    pr = cache_k2d.shape[1]
    pr_new = k_new2d.shape[1]
    rows = q_ht.shape[1]
    width = scores8.shape[-1]
    return pl.pallas_call(
        functools.partial(_s_attn_kernel, n_pages=n_pages, n_grp=n_grp, n_new=n_new),
        out_shape=jax.ShapeDtypeStruct((nb, rows, HEAD_DIM), BF16),
        grid_spec=pltpu.PrefetchScalarGridSpec(
            num_scalar_prefetch=1,
            grid=(nb,),
            in_specs=[pl.BlockSpec((None, rows, HEAD_DIM), lambda b, pt: (b, 0, 0)),
                      pl.BlockSpec((None, 8, width), lambda b, pt: (b, 0, 0)),
                      pl.BlockSpec((None, 8, LANES), lambda b, pt: (b, 0, 0)),
                      pl.BlockSpec((None, pr_new, HEAD_DIM), lambda b, pt: (b, 0, 0)),
                      pl.BlockSpec((None, pr_new, HEAD_DIM), lambda b, pt: (b, 0, 0))]
            + _page_specs(n_pages, (pr, HEAD_DIM)) + _page_specs(n_pages, (pr, HEAD_DIM)),
            out_specs=pl.BlockSpec((None, rows, HEAD_DIM), lambda b, pt: (b, 0, 0))),
        compiler_params=_cparams(1),
        name="s_attn",
    )(page_table, q_ht, scores8, thr8, k_new2d, v_new2d, *([cache_k2d] * n_pages), *([cache_v2d] * n_pages))


def _mix_kernel(d_ref, a_ref, ga_ref, gb_ref, wg_ref, ps_ref, wup_ref, wua_ref, o_ref):
    n_groups, group = wg_ref.shape[0], wg_ref.shape[1]
    y = jnp.concatenate(
        [jnp.dot(d_ref[:, g * group:(g + 1) * group], wg_ref[g], preferred_element_type=F32)
         for g in range(n_groups)], axis=1) * ps_ref[...]
    yp = jnp.dot(y.astype(BF16), wup_ref[...], preferred_element_type=F32)
    ya = jnp.dot(a_ref[...], wua_ref[...], preferred_element_type=F32)
    o_ref[...] = (jax.nn.sigmoid(ga_ref[...].astype(F32)) * yp
                  + jax.nn.sigmoid(gb_ref[...].astype(F32)) * ya).astype(BF16)


def _mix(d, attn, p16, w_grp, pool_scale, w_up_pool, w_up_attn, tm):
    m, dm = attn.shape
    pw = d.shape[1]
    const = lambda *shape: pl.BlockSpec(shape, lambda i: (0,) * len(shape))
    return pl.pallas_call(
        _mix_kernel,
        out_shape=jax.ShapeDtypeStruct((m, dm), BF16),
        grid=(m // tm,),
        in_specs=[pl.BlockSpec((tm, pw), lambda i: (i, 0)),
                  pl.BlockSpec((tm, dm), lambda i: (i, 0)),
                  pl.BlockSpec((tm, dm), lambda i: (i, 1)),
                  pl.BlockSpec((tm, dm), lambda i: (i, 2)),
                  const(*w_grp.shape), const(1, pw), const(*w_up_pool.shape), const(*w_up_attn.shape)],
        out_specs=pl.BlockSpec((tm, dm), lambda i: (i, 0)),
        compiler_params=_cparams(1),
        name="mix",
    )(d, attn, p16, p16, w_grp, pool_scale, w_up_pool, w_up_attn)


def _out_kernel(mix_ref, x_ref, gate_ref, sh_ref, sc_ref, g_ref, w_ref, h_ref, hn_ref):
    h = x_ref[...] + gate_ref[...] * jnp.dot(mix_ref[...], w_ref[...], preferred_element_type=F32)
    h_ref[...] = h
    hn_ref[...] = (_rms(h, g_ref[...]) * (1.0 + sc_ref[...]) + sh_ref[...]).astype(BF16)


def _out(mix, x, gate, shift, scale, g2, w_out, tm, rows_per_mod):
    m, d = x.shape
    return pl.pallas_call(
        _out_kernel,
        out_shape=(jax.ShapeDtypeStruct((m, d), F32), jax.ShapeDtypeStruct((m, d), BF16)),
        grid=(m // tm,),
        in_specs=[pl.BlockSpec((tm, d), lambda i: (i, 0)),
                  pl.BlockSpec((tm, d), lambda i: (i, 0)),
                  _mod_spec(gate, tm, rows_per_mod),
                  _mod_spec(shift, tm, rows_per_mod),
                  _mod_spec(scale, tm, rows_per_mod),
                  pl.BlockSpec((1, d), lambda i: (0, 0)),
                  pl.BlockSpec((d, d), lambda i: (0, 0))],
        out_specs=(pl.BlockSpec((tm, d), lambda i: (i, 0)), pl.BlockSpec((tm, d), lambda i: (i, 0))),
        compiler_params=_cparams(1),
        name="out",
    )(mix, x, gate, shift, scale, g2, w_out)


FFN_TF = 512


def _ffn_kernel(hn_ref, h_ref, gate_ref, gf_ref, wg_ref, wu_ref, wo_ref, o_ref, acc_ref):
    f = pl.program_id(1)

    @pl.when(f == 0)
    def _():
        acc_ref[...] = jnp.zeros_like(acc_ref)

    hn = hn_ref[...]
    a = jnp.dot(hn, wg_ref[...], preferred_element_type=F32)
    u = jnp.dot(hn, wu_ref[...], preferred_element_type=F32)
    z = (a * jax.nn.sigmoid(a) * u).astype(BF16)
    acc_ref[...] += jnp.dot(z, wo_ref[...], preferred_element_type=F32)

    @pl.when(f == pl.num_programs(1) - 1)
    def _():
        o_ref[...] = _rms(h_ref[...] + gate_ref[...] * acc_ref[...], gf_ref[...])


def _ffn(hn, h, gate, g_final, w_ffn_in, w_ffn_out, tm, rows_per_mod):
    m, d = h.shape
    d_ff = w_ffn_out.shape[0]
    n_f = d_ff // FFN_TF
    return pl.pallas_call(
        _ffn_kernel,
        out_shape=jax.ShapeDtypeStruct((m, d), F32),
        grid=(m // tm, n_f),
        in_specs=[pl.BlockSpec((tm, d), lambda i, f: (i, 0)),
                  pl.BlockSpec((tm, d), lambda i, f: (i, 0)),
                  _mod_spec(gate, tm, rows_per_mod),
                  pl.BlockSpec((1, d), lambda i, f: (0, 0)),
                  pl.BlockSpec((d, FFN_TF), lambda i, f: (0, f)),
                  pl.BlockSpec((d, FFN_TF), lambda i, f: (0, f + n_f)),
                  pl.BlockSpec((FFN_TF, d), lambda i, f: (f, 0))],
        out_specs=pl.BlockSpec((tm, d), lambda i, f: (i, 0)),
        scratch_shapes=[pltpu.VMEM((tm, d), F32)],
        compiler_params=_cparams(2),
        name="ffn",
    )(hn, h, gate, g_final, w_ffn_in, w_ffn_in, w_ffn_out)


def _back(x2, p16, d, attn, mods, lw, tm, rows_per_mod):
    mix = _mix(d, attn, p16, lw["w_grp"], lw["pool_scale"], lw["w_up_pool"], lw["w_up_attn"], tm)
    h, hn = _out(mix, x2, mods[2], mods[3], mods[4], lw["g2"], lw["w_out"], tm, rows_per_mod)
    return _ffn(hn, h, mods[5], lw["g_final"], lw["w_ffn_in"], lw["w_ffn_out"], tm, rows_per_mod)


def kernel(x_prompt, x_sample, cache_k, cache_v, cache_idx_k, state_pool, page_table, c_prompt, c_sample,
           w_ada, b_ada, g_norm1, w_in, w_pool_grp, pool_scale, w_up_pool, w_up_attn, w_out, g_norm2,
           w_ffn_in, w_ffn_out, g_final):
    batch, seq, dm = x_prompt.shape
    nb, n_new, _ = x_sample.shape
    depth = w_ada.shape[0]
    assert depth == 1, "single-layer step"
    n_phys, page = cache_k.shape[1], cache_k.shape[2]
    n_pages = page_table.shape[1]
    past = n_pages * page
    pool_w = state_pool.shape[-1]
    kv_w = N_KV_HEADS * HEAD_DIM
    n_heads = dm // HEAD_DIM
    n_grp = n_heads // N_KV_HEADS
    idx_w = N_IDX_HEADS * IDX_DIM
    assert page == LANES and n_new == 4 and seq % PA_CHUNK == 0

    w_t = jnp.swapaxes(w_in[0], 0, 1)
    assert (pool_w, dm, kv_w, idx_w) == (2 * FRONT_TN, 4 * FRONT_TN, FRONT_TN, 2 * FRONT_TN)
    o_q, o_k, o_v, o_qi = pool_w, pool_w + dm, pool_w + dm + kv_w, pool_w + dm + 2 * kv_w
    o_ki = o_qi + idx_w
    o_ga = o_ki + IDX_DIM + N_IDX_HEADS
    o_gb = o_ga + dm
    assert o_gb + dm == w_t.shape[0] and o_ki + FRONT_TN <= w_t.shape[0]
    tile_rows = ([o_q + i * FRONT_TN for i in range(4)] + [o_ga + i * FRONT_TN for i in range(4)]
                 + [o_gb + i * FRONT_TN for i in range(4)] + [o_qi, o_qi + FRONT_TN, 0, FRONT_TN, o_k, o_v, o_ki])
    assert len(tile_rows) == N_FRONT_TILES and all(r % 8 == 0 for r in tile_rows)
    lw = {"w_grp": w_pool_grp[0].astype(BF16), "pool_scale": pool_scale[0].reshape(1, pool_w),
          "w_up_pool": w_up_pool[0].astype(BF16), "w_up_attn": w_up_attn[0].astype(BF16),
          "w_out": w_out[0].astype(BF16), "g2": g_norm2[0].reshape(1, dm), "g_final": g_final.reshape(1, dm),
          "w_ffn_in": w_ffn_in[0].astype(BF16), "w_ffn_out": w_ffn_out[0].astype(BF16)}
    g1 = g_norm1[0].reshape(1, dm)

    n_c = batch + nb
    c_all = jnp.concatenate([c_prompt, c_sample, jnp.zeros((-n_c % 8, dm), F32)], axis=0)
    mod = _ada(c_all, w_ada[0], b_ada[0])
    mods_p = [mod[:batch, i * dm:(i + 1) * dm].reshape(batch, 1, dm) for i in range(6)]
    mods_s = [jnp.repeat(mod[batch:n_c, i * dm:(i + 1) * dm], n_new, axis=0) for i in range(6)]

    xp = x_prompt.reshape(batch * seq, dm)
    p16, p32 = _front(xp, g1, mods_p[0], mods_p[1], w_t, tile_rows, 1024, seq)
    p32b = p32.reshape(batch, seq, W32)
    k_prompt = p32b[:, :, pool_w:pool_w + kv_w].reshape(1, batch, seq, N_KV_HEADS, HEAD_DIM)
    v_prompt = p32b[:, :, pool_w + kv_w:pool_w + 2 * kv_w].reshape(1, batch, seq, N_KV_HEADS, HEAD_DIM)
    o_small = pool_w + 2 * kv_w
    ki_p = p32b[:, :, o_small:o_small + IDX_DIM]
    wi_p = p32b[:, :, o_small + IDX_DIM:o_small + IDX_DIM + N_IDX_HEADS]
    pool_prompt = p32b[:, seq - POOL_BUF:, :pool_w][None]
    ki16 = ki_p.astype(BF16)
    zeros = jnp.zeros_like(ki16)
    kia = jnp.concatenate([ki16, zeros], axis=-1)
    kib = jnp.concatenate([zeros, ki16], axis=-1)
    d_p = _pool_d_prompt(p32, batch, seq, pool_w).reshape(batch * seq, pool_w)
    wit_p = jnp.swapaxes(wi_p, 1, 2)
    vt_p = jnp.swapaxes(p16.reshape(batch, seq, W16)[:, :, 15 * FRONT_TN:16 * FRONT_TN], 1, 2)
    vt_p = jnp.concatenate([vt_p.reshape(batch, N_KV_HEADS, HEAD_DIM, seq),
                            jnp.ones((batch, N_KV_HEADS, V_ROWS - HEAD_DIM, seq), BF16)], axis=2)
    vt_p = vt_p.reshape(batch, N_KV_HEADS * V_ROWS, seq)
    attn_p = _p_attn(p16, wit_p, kia, kib, vt_p, batch, seq, min(TOPK_MAX, seq // 4)).reshape(batch * seq, dm)
    y_prompt = _back(xp, p16, d_p, attn_p, mods_p, lw, 512, seq).reshape(batch, seq, dm)

    xs = x_sample.reshape(nb * n_new, dm)
    s16, s32 = _front(xs, g1, mods_s[0], mods_s[1], w_t, tile_rows, nb * n_new, 0)
    s32b = s32.reshape(nb, n_new, W32)
    k_new = s32b[:, :, pool_w:pool_w + kv_w]
    v_new = s32b[:, :, pool_w + kv_w:pool_w + 2 * kv_w]
    ki_new = s32b[:, :, o_small:o_small + IDX_DIM]
    wi_s = s32b[:, :, o_small + IDX_DIM:o_small + IDX_DIM + N_IDX_HEADS]
    seq_s = jnp.concatenate([state_pool[0], s32b[:, :, :pool_w]], axis=1)
    pool_sample = seq_s[:, n_new:][None]
    d_s = _pool_d_sample(jnp.swapaxes(seq_s, 0, 1), n_new)
    d_s = jnp.swapaxes(d_s, 0, 1).reshape(nb * n_new, pool_w)

    s16b = s16.reshape(nb, n_new, W16)
    hx = N_IDX_HEADS * n_new
    qi_th = s16b[:, :, 3 * dm:3 * dm + idx_w].reshape(nb, hx, IDX_DIM)
    w_th = (wi_s * (IDX_DIM ** -0.5 * N_IDX_HEADS ** -0.5)).reshape(nb, hx, 1)
    ki_new_t = jnp.swapaxes(jnp.pad(ki_new, ((0, 0), (0, page - n_new), (0, 0))), 1, 2)
    cache_ik_t = jnp.swapaxes(cache_idx_k[0], 1, 2)
    scores8 = _s_score(page_table, qi_th, w_th, ki_new_t, cache_ik_t, n_new)
    topk_s = min(TOPK_MAX, (past + n_new) // 4)
    thr4 = _s_thr(scores8[:, :n_new].reshape(nb * n_new, scores8.shape[-1]), topk_s).reshape(nb, n_new, LANES)
    thr8 = jnp.concatenate([thr4, thr4], axis=1)

    q_ht = jnp.swapaxes(s16b[:, :, :dm].reshape(nb, n_new, n_heads, HEAD_DIM), 1, 2)
    q_ht = q_ht.reshape(nb, n_heads * n_new, HEAD_DIM)
    pr = page * N_KV_HEADS
    k_new2d = jnp.pad(k_new.astype(BF16), ((0, 0), (0, 8 - n_new), (0, 0))).reshape(nb, 8 * N_KV_HEADS, HEAD_DIM)
    v_new2d = jnp.pad(v_new.astype(BF16), ((0, 0), (0, 8 - n_new), (0, 0))).reshape(nb, 8 * N_KV_HEADS, HEAD_DIM)
    ck2d = cache_k[0].reshape(n_phys, pr, HEAD_DIM)
    cv2d = cache_v[0].reshape(n_phys, pr, HEAD_DIM)
    o_s = _s_attn(page_table, q_ht, scores8, thr8, k_new2d, v_new2d, ck2d, cv2d, n_grp, n_new)
    attn_s = jnp.swapaxes(o_s.reshape(nb, n_heads, n_new, HEAD_DIM), 1, 2).reshape(nb * n_new, dm)
    y_sample = _back(xs, s16, d_s, attn_s, mods_s, lw, nb * n_new, 0).reshape(nb, n_new, dm)

    k_sample = k_new.reshape(1, nb, n_new, N_KV_HEADS, HEAD_DIM)
    v_sample = v_new.reshape(1, nb, n_new, N_KV_HEADS, HEAD_DIM)
    return (y_prompt, y_sample, k_prompt, v_prompt, ki_p[None], pool_prompt,
            k_sample, v_sample, ki_new[None], pool_sample)
```

```python
import functools
import math

import jax
import jax.numpy as jnp
from jax import lax
from jax.experimental import pallas as pl
from jax.experimental.pallas import tpu as pltpu

F32 = jnp.float32
BF16 = jnp.bfloat16

RMS_EPS = 1e-6
POOL_WINDOWS = (2, 4, 8, 16)
POOL_BUF = max(POOL_WINDOWS) - 1
HEAD_DIM = 128
N_KV_HEADS = 4
N_IDX_HEADS = 16
IDX_DIM = 64
TOPK_MAX = 256
Q_BLOCK = 128
LANES = 128
NEG = -1e30
Q_SCALE = HEAD_DIM ** -0.5 * math.log2(math.e)
V_ROWS = HEAD_DIM + 16
NT_DIMS = (((1,), (1,)), ((), ()))
VMEM_LIMIT = 56 * 1024 * 1024


def _cparams(n_axes):
    return pltpu.CompilerParams(dimension_semantics=("arbitrary",) * n_axes, vmem_limit_bytes=VMEM_LIMIT)


def _rms(x, g):
    return x * lax.rsqrt(jnp.mean(x * x, axis=-1, keepdims=True) + RMS_EPS) * g


def _ada_kernel(c_ref, w_ref, b_ref, o_ref):
    c = c_ref[...]
    s = (c * jax.nn.sigmoid(c)).astype(BF16)
    o_ref[...] = jnp.dot(s, w_ref[...].astype(BF16), preferred_element_type=F32) + b_ref[...]


def _ada(c_all, w_ada, b_ada):
    rows, d = c_all.shape
    n = w_ada.shape[1]
    tn = 512
    return pl.pallas_call(
        _ada_kernel,
        out_shape=jax.ShapeDtypeStruct((rows, n), F32),
        grid=(n // tn,),
        in_specs=[pl.BlockSpec((rows, d), lambda j: (0, 0)),
                  pl.BlockSpec((d, tn), lambda j: (0, j)),
                  pl.BlockSpec((1, tn), lambda j: (0, j))],
        out_specs=pl.BlockSpec((rows, tn), lambda j: (0, j)),
        compiler_params=_cparams(1),
        name="ada",
    )(c_all, w_ada, b_ada.reshape(1, n))


FRONT_TN = 512
N_FRONT_TILES = 19
W16 = 16 * FRONT_TN
W32 = 5 * FRONT_TN


def _front_kernel(x_ref, g_ref, sh_ref, sc_ref, wt_ref, o16_ref, o32_ref, u_ref):
    n = pl.program_id(1)

    @pl.when(n == 0)
    def _():
        u_ref[...] = (_rms(x_ref[...], g_ref[...]) * (1.0 + sc_ref[...]) + sh_ref[...]).astype(BF16)

    r = lax.dot_general(u_ref[...], wt_ref[...].astype(BF16), NT_DIMS, preferred_element_type=F32)

    @pl.when((n < 14) | (n == 16) | (n == 17))
    def _():
        o16_ref[...] = (r * jnp.where(n < 4, Q_SCALE, 1.0)).astype(BF16)

    @pl.when(n >= 14)
    def _():
        o32_ref[...] = r


def _lookup(n, table):
    out = jnp.int32(table[-1])
    for i, v in enumerate(table[:-1]):
        out = jnp.where(n == i, v, out)
    return out


def _o16_block(n):
    return jnp.where(n < 14, n, jnp.where(n < 16, 13, jnp.where(n < 18, n - 2, 15)))


def _mod_spec(mod, tm, rows_per_mod):
    d = mod.shape[-1]
    if mod.ndim == 3:
        tiles_per_batch = rows_per_mod // tm
        return pl.BlockSpec((None, 1, d), lambda i, *_: (i // tiles_per_batch, 0, 0))
    return pl.BlockSpec((tm, d), lambda i, *_: (i, 0))


def _front(x, g1, shift, scale, w_t, tile_rows, tm, rows_per_mod):
    m, d = x.shape
    return pl.pallas_call(
        _front_kernel,
        out_shape=(jax.ShapeDtypeStruct((m, W16), BF16), jax.ShapeDtypeStruct((m, W32), F32)),
        grid=(m // tm, N_FRONT_TILES),
        in_specs=[pl.BlockSpec((tm, d), lambda i, n: (i, 0)),
                  pl.BlockSpec((1, d), lambda i, n: (0, 0)),
                  _mod_spec(shift, tm, rows_per_mod),
                  _mod_spec(scale, tm, rows_per_mod),
                  pl.BlockSpec((pl.Element(FRONT_TN), pl.Element(d)),
                               lambda i, n: (_lookup(n, [r // 8 for r in tile_rows]) * 8, 0))],
        out_specs=(pl.BlockSpec((tm, FRONT_TN), lambda i, n: (i, _o16_block(n))),
                   pl.BlockSpec((tm, FRONT_TN), lambda i, n: (i, jnp.maximum(n - 14, 0)))),
        scratch_shapes=[pltpu.VMEM((tm, d), BF16)],
        compiler_params=_cparams(2),
        name="front",
    )(x, g1, shift, scale, w_t)


def _pool_d_prompt_kernel(pin_ref, halo_ref, o_ref, buf_ref, *, tp, group):
    i = pl.program_id(1)
    cur = pin_ref[...]
    halo = jnp.where(i > 0, halo_ref[...], 0.0)
    hb = halo.shape[0]
    buf_ref[0:hb, :] = halo
    buf_ref[hb:hb + tp, :] = cur
    pos = i * tp + lax.broadcasted_iota(jnp.int32, (tp, 1), 0)
    for gi, w in enumerate(POOL_WINDOWS):
        sl = slice(gi * group, (gi + 1) * group)
        acc = cur[:, sl]
        for j in range(1, w):
            acc = acc + buf_ref[hb - j:hb - j + tp, sl]
        cnt = jnp.minimum(pos + 1, w).astype(F32)
        o_ref[:, sl] = (acc / cnt - cur[:, sl]).astype(BF16)


def _pool_d_prompt(p32, batch, seq, width):
    tp, hb = 512, 16
    group = width // len(POOL_WINDOWS)
    p32b = p32.reshape(batch, seq, p32.shape[-1])
    return pl.pallas_call(
        functools.partial(_pool_d_prompt_kernel, tp=tp, group=group),
        out_shape=jax.ShapeDtypeStruct((batch, seq, width), BF16),
        grid=(batch, seq // tp),
        in_specs=[pl.BlockSpec((None, tp, width), lambda b, i: (b, i, 0)),
                  pl.BlockSpec((None, hb, width), lambda b, i: (b, jnp.maximum(i * (tp // hb) - 1, 0), 0))],
        out_specs=pl.BlockSpec((None, tp, width), lambda b, i: (b, i, 0)),
        scratch_shapes=[pltpu.VMEM((tp + hb, width), F32)],
        compiler_params=_cparams(2),
        name="pool_d_prompt",
    )(p32b, p32b)


def _pool_d_sample_kernel(seq_ref, o_ref, *, n_new, group):
    for t in range(n_new):
        r = POOL_BUF + t
        for gi, w in enumerate(POOL_WINDOWS):
            sl = slice(gi * group, (gi + 1) * group)
            acc = seq_ref[r, :, sl]
            for j in range(1, w):
                acc = acc + seq_ref[r - j, :, sl]
            o_ref[t, :, sl] = (acc / float(w) - seq_ref[r, :, sl]).astype(BF16)


def _pool_d_sample(seq_t, n_new):
    rows, nb, width = seq_t.shape
    group = width // len(POOL_WINDOWS)
    return pl.pallas_call(
        functools.partial(_pool_d_sample_kernel, n_new=n_new, group=group),
        out_shape=jax.ShapeDtypeStruct((n_new, nb, width), BF16),
        grid=(1,),
        in_specs=[pl.BlockSpec((rows, nb, width), lambda i: (0, 0, 0))],
        out_specs=pl.BlockSpec((n_new, nb, width), lambda i: (0, 0, 0)),
        compiler_params=_cparams(1),
        name="pool_d_sample",
    )(seq_t)


def _key_to_float(key):
    bits = key ^ ((key >> 31) & jnp.int32(0x7FFFFFFF))
    return lax.bitcast_convert_type(bits, F32)


def _count_cols(sc_ref, n_chunks, chunk, flag):
    n_acc = 4

    def chunk_body(c, cnts):
        c0 = pl.multiple_of(c * chunk, chunk)
        s = sc_ref[pl.ds(c0, chunk), :]
        cnts = list(cnts)
        for r in range(chunk // 8):
            cnts[r % n_acc] = cnts[r % n_acc] + flag(s[r * 8:(r + 1) * 8], c0 + r * 8)
        return tuple(cnts)

    cnts = lax.fori_loop(0, n_chunks, chunk_body, tuple(jnp.zeros((8, LANES), jnp.int32) for _ in range(n_acc)))
    return jnp.sum(functools.reduce(jnp.add, cnts), axis=0, keepdims=True)


def _kth_largest_cols(sc_ref, n_chunks, chunk, k):
    int_min = jnp.int32(-(2 ** 31))

    def bit_body(it, key):
        cand = key ^ lax.shift_left(jnp.int32(1), 31 - it)
        cf = jnp.broadcast_to(_key_to_float(cand), (8, LANES))
        tot = _count_cols(sc_ref, n_chunks, chunk, lambda s, r0: jnp.where(s >= cf, 1, 0))
        return jnp.where(tot >= k, cand, key)

    key = lax.fori_loop(0, 32, bit_body, jnp.full((1, LANES), int_min, jnp.int32))
    return _key_to_float(key)


def _break_ties_cols(sc_ref, n_chunks, chunk, k, thr, n_rows):
    thr8 = jnp.broadcast_to(thr, (8, LANES))
    n_ge = _count_cols(sc_ref, n_chunks, chunk, lambda s, r0: jnp.where(s >= thr8, 1, 0))

    @pl.when(jnp.max(n_ge) > k)
    def _():
        need = k - _count_cols(sc_ref, n_chunks, chunk, lambda s, r0: jnp.where(s > thr8, 1, 0))
        row8 = lax.broadcasted_iota(jnp.int32, (8, LANES), 0)
        n_bits = (n_rows - 1).bit_length()

        def bit_body(it, last):
            cand = last | lax.shift_left(jnp.int32(1), n_bits - 1 - it)
            cand8 = jnp.broadcast_to(cand, (8, LANES))
            before = _count_cols(sc_ref, n_chunks, chunk,
                                 lambda s, r0: jnp.where(s == thr8, jnp.where(row8 + r0 < cand8, 1, 0), 0))
            return jnp.where(before < need, cand, last)

        last = lax.fori_loop(0, n_bits, bit_body, jnp.zeros((1, LANES), jnp.int32))

        def drop(c, carry):
            c0 = pl.multiple_of(c * chunk, chunk)
            s = sc_ref[pl.ds(c0, chunk), :]
            row = c0 + lax.broadcasted_iota(jnp.int32, (chunk, LANES), 0)
            sc_ref[pl.ds(c0, chunk), :] = jnp.where(s == thr, jnp.where(row > last, -jnp.inf, s), s)
            return carry

        lax.fori_loop(0, n_chunks, drop, 0)


PA_CHUNK = 512


def _p_attn_kernel(q_ref, qi_ref, wit_ref, kia_ref, kib_ref, k_ref, vt_ref, o_ref, sc_ref, *, topk):
    blk = pl.program_id(1)
    t0 = blk * Q_BLOCK
    n_ch = (t0 + Q_BLOCK + PA_CHUNK - 1) // PA_CHUNK
    tok = t0 + lax.broadcasted_iota(jnp.int32, (1, Q_BLOCK), 1)
    n_grp = (q_ref.shape[1] // HEAD_DIM) // N_KV_HEADS
    n_pairs = N_IDX_HEADS // 2

    wit = wit_ref[...] * (IDX_DIM ** -0.5 * N_IDX_HEADS ** -0.5)

    def score_chunk(c, carry):
        c0 = pl.multiple_of(c * PA_CHUNK, PA_CHUNK)
        ka = kia_ref[pl.ds(c0, PA_CHUNK), :]
        kb = kib_ref[pl.ds(c0, PA_CHUNK), :]
        acc = jnp.zeros((PA_CHUNK, Q_BLOCK), F32)
        for pp in range(n_pairs // 2):
            qp = jnp.concatenate([qi_ref[:, (2 * pp + i) * LANES:(2 * pp + i + 1) * LANES] for i in range(2)], axis=0)
            da = lax.dot_general(ka, qp, NT_DIMS, preferred_element_type=F32)
            db = lax.dot_general(kb, qp, NT_DIMS, preferred_element_type=F32)
            for i in range(2):
                h = 2 * (2 * pp + i)
                sl = slice(i * Q_BLOCK, (i + 1) * Q_BLOCK)
                acc = acc + jnp.maximum(da[:, sl], 0.0) * wit[h:h + 1, :]
                acc = acc + jnp.maximum(db[:, sl], 0.0) * wit[h + 1:h + 2, :]
        keypos = c0 + lax.broadcasted_iota(jnp.int32, (PA_CHUNK, Q_BLOCK), 0)
        sc_ref[pl.ds(c0, PA_CHUNK), :] = jnp.where(keypos <= tok, acc, -jnp.inf)
        return carry

    lax.fori_loop(0, n_ch, score_chunk, 0)

    thr = _kth_largest_cols(sc_ref, n_ch, PA_CHUNK, topk)
    thr = jnp.where(tok + 1 <= topk, jnp.finfo(F32).min, thr)
    _break_ties_cols(sc_ref, n_ch, PA_CHUNK, topk, thr, sc_ref.shape[0])

    width = n_grp * Q_BLOCK

    def attn_chunk(c, carry):
        c0 = pl.multiple_of(c * PA_CHUNK, PA_CHUNK)
        bias = jnp.where(sc_ref[pl.ds(c0, PA_CHUNK), :] >= thr, 0.0, NEG)
        bias = jnp.concatenate([bias] * n_grp, axis=1)

        def logits(j):
            qj = jnp.concatenate([q_ref[:, (j * n_grp + g) * HEAD_DIM:(j * n_grp + g + 1) * HEAD_DIM]
                                  for g in range(n_grp)], axis=0)
            kc = k_ref[pl.ds(c0, PA_CHUNK), j * HEAD_DIM:(j + 1) * HEAD_DIM]
            return lax.dot_general(kc, qj, NT_DIMS, preferred_element_type=F32) + bias

        def weights(j, s):
            m = carry[j][0]
            m_new = jnp.maximum(m, jnp.max(s, axis=0, keepdims=True))
            return m_new, jnp.exp2(m - m_new), jnp.exp2(s - m_new).astype(BF16)

        def accumulate(j, alpha, p):
            vt = vt_ref[j * V_ROWS:(j + 1) * V_ROWS, pl.ds(c0, PA_CHUNK)]
            return alpha * carry[j][1] + jnp.dot(vt, p, preferred_element_type=F32)

        s, w, out = {}, {}, []
        for step in range(N_KV_HEADS + 2):
            if step < N_KV_HEADS:
                s[step] = logits(step)
            if 0 <= step - 1 < N_KV_HEADS:
                w[step - 1] = weights(step - 1, s.pop(step - 1))
            if 0 <= step - 2 < N_KV_HEADS:
                m_new, alpha, p = w.pop(step - 2)
                out.append((m_new, accumulate(step - 2, alpha, p)))
        return tuple(out)

    init = tuple((jnp.full((1, width), NEG, F32), jnp.zeros((V_ROWS, width), F32)) for _ in range(N_KV_HEADS))
    final = lax.fori_loop(0, n_ch, attn_chunk, init)
    for j, (_, acc) in enumerate(final):
        o_t = acc[:HEAD_DIM] / acc[HEAD_DIM:HEAD_DIM + 1]
        for g in range(n_grp):
            h = j * n_grp + g
            o_ref[:, h * HEAD_DIM:(h + 1) * HEAD_DIM] = o_t[:, g * Q_BLOCK:(g + 1) * Q_BLOCK].T.astype(BF16)


def _p_attn(p16, wit, kia, kib, vt, batch, seq, topk):
    d_attn = 2048
    p16b = p16.reshape(batch, seq, W16)
    kv_w = N_KV_HEADS * HEAD_DIM
    return pl.pallas_call(
        functools.partial(_p_attn_kernel, topk=topk),
        out_shape=jax.ShapeDtypeStruct((batch, seq, d_attn), BF16),
        grid=(batch, seq // Q_BLOCK),
        in_specs=[pl.BlockSpec((None, Q_BLOCK, d_attn), lambda b, i: (b, i, 0)),
                  pl.BlockSpec((None, Q_BLOCK, 1024), lambda b, i: (b, i, 6)),
                  pl.BlockSpec((None, N_IDX_HEADS, Q_BLOCK), lambda b, i: (b, 0, i)),
                  pl.BlockSpec((None, seq, LANES), lambda b, i: (b, 0, 0)),
                  pl.BlockSpec((None, seq, LANES), lambda b, i: (b, 0, 0)),
                  pl.BlockSpec((None, seq, kv_w), lambda b, i: (b, 0, 14)),
                  pl.BlockSpec((None, N_KV_HEADS * V_ROWS, seq), lambda b, i: (b, 0, 0))],
        out_specs=pl.BlockSpec((None, Q_BLOCK, d_attn), lambda b, i: (b, i, 0)),
        scratch_shapes=[pltpu.VMEM((seq, Q_BLOCK), F32)],
        compiler_params=_cparams(2),
        name="p_attn",
    )(p16b, p16b, wit, kia, kib, p16b, vt)


def _s_score_kernel(pt_ref, qi_ref, w_ref, kin_ref, *rest, n_pages, page, n_new):
    del pt_ref
    ik_refs, o_ref = rest[:n_pages], rest[n_pages]
    qi = qi_ref[...]
    w = w_ref[...]

    def piece(ik_t):
        d = jnp.dot(qi, ik_t.astype(BF16), preferred_element_type=F32)
        x = jnp.maximum(d, 0.0) * w
        rows = [jnp.sum(x[t * N_IDX_HEADS:(t + 1) * N_IDX_HEADS], axis=0, keepdims=True) for t in range(n_new)]
        return jnp.concatenate(rows + rows, axis=0)

    for p in range(n_pages):
        o_ref[:, p * page:(p + 1) * page] = piece(ik_refs[p][...])
    new = piece(kin_ref[...])
    col = lax.broadcasted_iota(jnp.int32, new.shape, 1)
    tok = lax.broadcasted_iota(jnp.int32, new.shape, 0) % n_new
    o_ref[:, n_pages * page:(n_pages + 1) * page] = jnp.where(col <= tok, new, -jnp.inf)


def _page_specs(n_pages, shape):
    zeros = (0,) * len(shape)
    return [pl.BlockSpec((None,) + shape, functools.partial(lambda b, pt, p: (pt[b, p],) + zeros, p=p))
            for p in range(n_pages)]


def _s_score(page_table, qi_th, w_th, ki_new_t, cache_ik_t, n_new):
    nb, n_pages = page_table.shape
    page = cache_ik_t.shape[2]
    width = (n_pages + 1) * page
    hx = qi_th.shape[1]
    return pl.pallas_call(
        functools.partial(_s_score_kernel, n_pages=n_pages, page=page, n_new=n_new),
        out_shape=jax.ShapeDtypeStruct((nb, 2 * n_new, width), F32),
        grid_spec=pltpu.PrefetchScalarGridSpec(
            num_scalar_prefetch=1,
            grid=(nb,),
            in_specs=[pl.BlockSpec((None, hx, IDX_DIM), lambda b, pt: (b, 0, 0)),
                      pl.BlockSpec((None, hx, 1), lambda b, pt: (b, 0, 0)),
                      pl.BlockSpec((None, IDX_DIM, page), lambda b, pt: (b, 0, 0))]
            + _page_specs(n_pages, (IDX_DIM, page)),
            out_specs=pl.BlockSpec((None, 2 * n_new, width), lambda b, pt: (b, 0, 0))),
        compiler_params=_cparams(1),
        name="s_score",
    )(page_table, qi_th, w_th, ki_new_t, *([cache_ik_t] * n_pages))


def _s_sel_kernel(sc_ref, o_ref, buf_ref, *, topk, chunk):
    n_rows = sc_ref.shape[0]
    buf_ref[...] = sc_ref[...]
    thr = _kth_largest_cols(buf_ref, n_rows // chunk, chunk, topk)
    _break_ties_cols(buf_ref, n_rows // chunk, chunk, topk, thr, n_rows)
    o_ref[...] = jnp.where(buf_ref[...] >= thr, 1.0, 0.0)


def _s_sel(scores_t, topk):
    n_rows, n_q = scores_t.shape
    return pl.pallas_call(
        functools.partial(_s_sel_kernel, topk=topk, chunk=LANES),
        out_shape=jax.ShapeDtypeStruct((n_rows, n_q), F32),
        grid=(n_q // LANES,),
        in_specs=[pl.BlockSpec((n_rows, LANES), lambda i: (0, i))],
        out_specs=pl.BlockSpec((n_rows, LANES), lambda i: (0, i)),
        scratch_shapes=[pltpu.VMEM((n_rows, LANES), F32)],
        compiler_params=_cparams(1),
        name="s_sel",
    )(scores_t)


def _s_attn_kernel(pt_ref, q_ref, sel_ref, kn_ref, vn_ref, *rest, n_pages, n_grp, n_new):
    del pt_ref
    k_refs, v_refs, o_ref = rest[:n_pages], rest[n_pages:2 * n_pages], rest[2 * n_pages]
    q = q_ref[...]
    rows = q.shape[0]
    pr = k_refs[0].shape[0]
    page = pr // N_KV_HEADS
    sel = sel_ref[...].astype(BF16)
    sel = jnp.concatenate([sel[:, p * page:(p + 1) * page] for p in range(n_pages + 1)], axis=0)
    spread = (lax.broadcasted_iota(jnp.int32, (page, pr), 1) // N_KV_HEADS
              == lax.broadcasted_iota(jnp.int32, (page, pr), 0)).astype(BF16)
    sel = jnp.dot(sel, spread, preferred_element_type=F32)
    lane_kv = lax.broadcasted_iota(jnp.int32, (rows, pr), 1) % N_KV_HEADS
    row_kv = lax.broadcasted_iota(jnp.int32, (rows, pr), 0) // (n_grp * n_new)
    head_bias = jnp.where(lane_kv == row_kv, 0.0, NEG)
    pieces = []
    for p in range(n_pages + 1):
        kp = kn_ref[...] if p == n_pages else k_refs[p][...].astype(BF16)
        w = kp.shape[0]
        s = lax.dot_general(q, kp, NT_DIMS, preferred_element_type=F32)
        bias = jnp.concatenate([(sel[p * 8:(p + 1) * 8, :w] - 1.0) * -NEG] * (rows // 8), axis=0)
        pieces.append(s + (bias + head_bias[:, :w]))
    m = functools.reduce(jnp.maximum, [jnp.max(s, axis=1, keepdims=True) for s in pieces])
    l = jnp.zeros((rows, 1), F32)
    o = jnp.zeros((rows, HEAD_DIM), F32)
    for p in range(n_pages + 1):
        e = jnp.exp2(pieces[p] - m)
        l = l + jnp.sum(e, axis=1, keepdims=True)
        vp = vn_ref[...] if p == n_pages else v_refs[p][...].astype(BF16)
        o = o + jnp.dot(e.astype(BF16), vp, preferred_element_type=F32)
    o_ref[...] = (o / l).astype(BF16)


def _s_attn(page_table, q_ht, sel8, k_new2d, v_new2d, cache_k2d, cache_v2d, n_grp, n_new):
    nb, n_pages = page_table.shape
    pr = cache_k2d.shape[1]
    pr_new = k_new2d.shape[1]
    rows = q_ht.shape[1]
    width = sel8.shape[-1]
    return pl.pallas_call(
        functools.partial(_s_attn_kernel, n_pages=n_pages, n_grp=n_grp, n_new=n_new),
        out_shape=jax.ShapeDtypeStruct((nb, rows, HEAD_DIM), BF16),
        grid_spec=pltpu.PrefetchScalarGridSpec(
            num_scalar_prefetch=1,
            grid=(nb,),
            in_specs=[pl.BlockSpec((None, rows, HEAD_DIM), lambda b, pt: (b, 0, 0)),
                      pl.BlockSpec((None, 8, width), lambda b, pt: (b, 0, 0)),
                      pl.BlockSpec((None, pr_new, HEAD_DIM), lambda b, pt: (b, 0, 0)),
                      pl.BlockSpec((None, pr_new, HEAD_DIM), lambda b, pt: (b, 0, 0))]
            + _page_specs(n_pages, (pr, HEAD_DIM)) + _page_specs(n_pages, (pr, HEAD_DIM)),
            out_specs=pl.BlockSpec((None, rows, HEAD_DIM), lambda b, pt: (b, 0, 0))),
        compiler_params=_cparams(1),
        name="s_attn",
    )(page_table, q_ht, sel8, k_new2d, v_new2d, *([cache_k2d] * n_pages), *([cache_v2d] * n_pages))


def _mix_kernel(d_ref, a_ref, ga_ref, gb_ref, wg_ref, ps_ref, wup_ref, wua_ref, o_ref):
    n_groups, group = wg_ref.shape[0], wg_ref.shape[1]
    y = jnp.concatenate(
        [jnp.dot(d_ref[:, g * group:(g + 1) * group], wg_ref[g], preferred_element_type=F32)
         for g in range(n_groups)], axis=1) * ps_ref[...]
    yp = jnp.dot(y.astype(BF16), wup_ref[...], preferred_element_type=F32)
    ya = jnp.dot(a_ref[...], wua_ref[...], preferred_element_type=F32)
    o_ref[...] = (jax.nn.sigmoid(ga_ref[...].astype(F32)) * yp
                  + jax.nn.sigmoid(gb_ref[...].astype(F32)) * ya).astype(BF16)


def _mix(d, attn, p16, w_grp, pool_scale, w_up_pool, w_up_attn, tm):
    m, dm = attn.shape
    pw = d.shape[1]
    const = lambda *shape: pl.BlockSpec(shape, lambda i: (0,) * len(shape))
    return pl.pallas_call(
        _mix_kernel,
        out_shape=jax.ShapeDtypeStruct((m, dm), BF16),
        grid=(m // tm,),
        in_specs=[pl.BlockSpec((tm, pw), lambda i: (i, 0)),
                  pl.BlockSpec((tm, dm), lambda i: (i, 0)),
                  pl.BlockSpec((tm, dm), lambda i: (i, 1)),
                  pl.BlockSpec((tm, dm), lambda i: (i, 2)),
                  const(*w_grp.shape), const(1, pw), const(*w_up_pool.shape), const(*w_up_attn.shape)],
        out_specs=pl.BlockSpec((tm, dm), lambda i: (i, 0)),
        compiler_params=_cparams(1),
        name="mix",
    )(d, attn, p16, p16, w_grp, pool_scale, w_up_pool, w_up_attn)


def _out_kernel(mix_ref, x_ref, gate_ref, sh_ref, sc_ref, g_ref, w_ref, h_ref, hn_ref):
    h = x_ref[...] + gate_ref[...] * jnp.dot(mix_ref[...], w_ref[...], preferred_element_type=F32)
    h_ref[...] = h
    hn_ref[...] = (_rms(h, g_ref[...]) * (1.0 + sc_ref[...]) + sh_ref[...]).astype(BF16)


def _out(mix, x, gate, shift, scale, g2, w_out, tm, rows_per_mod):
    m, d = x.shape
    return pl.pallas_call(
        _out_kernel,
        out_shape=(jax.ShapeDtypeStruct((m, d), F32), jax.ShapeDtypeStruct((m, d), BF16)),
        grid=(m // tm,),
        in_specs=[pl.BlockSpec((tm, d), lambda i: (i, 0)),
                  pl.BlockSpec((tm, d), lambda i: (i, 0)),
                  _mod_spec(gate, tm, rows_per_mod),
                  _mod_spec(shift, tm, rows_per_mod),
                  _mod_spec(scale, tm, rows_per_mod),
                  pl.BlockSpec((1, d), lambda i: (0, 0)),
                  pl.BlockSpec((d, d), lambda i: (0, 0))],
        out_specs=(pl.BlockSpec((tm, d), lambda i: (i, 0)), pl.BlockSpec((tm, d), lambda i: (i, 0))),
        compiler_params=_cparams(1),
        name="out",
    )(mix, x, gate, shift, scale, g2, w_out)


FFN_TF = 512


def _ffn_kernel(hn_ref, h_ref, gate_ref, gf_ref, wg_ref, wu_ref, wo_ref, o_ref, acc_ref):
    f = pl.program_id(1)

    @pl.when(f == 0)
    def _():
        acc_ref[...] = jnp.zeros_like(acc_ref)

    hn = hn_ref[...]
    a = jnp.dot(hn, wg_ref[...], preferred_element_type=F32)
    u = jnp.dot(hn, wu_ref[...], preferred_element_type=F32)
    z = (a * jax.nn.sigmoid(a) * u).astype(BF16)
    acc_ref[...] += jnp.dot(z, wo_ref[...], preferred_element_type=F32)

    @pl.when(f == pl.num_programs(1) - 1)
    def _():
        o_ref[...] = _rms(h_ref[...] + gate_ref[...] * acc_ref[...], gf_ref[...])


def _ffn(hn, h, gate, g_final, w_ffn_in, w_ffn_out, tm, rows_per_mod):
    m, d = h.shape
    d_ff = w_ffn_out.shape[0]
    n_f = d_ff // FFN_TF
    return pl.pallas_call(
        _ffn_kernel,
        out_shape=jax.ShapeDtypeStruct((m, d), F32),
        grid=(m // tm, n_f),
        in_specs=[pl.BlockSpec((tm, d), lambda i, f: (i, 0)),
                  pl.BlockSpec((tm, d), lambda i, f: (i, 0)),
                  _mod_spec(gate, tm, rows_per_mod),
                  pl.BlockSpec((1, d), lambda i, f: (0, 0)),
                  pl.BlockSpec((d, FFN_TF), lambda i, f: (0, f)),
                  pl.BlockSpec((d, FFN_TF), lambda i, f: (0, f + n_f)),
                  pl.BlockSpec((FFN_TF, d), lambda i, f: (f, 0))],
        out_specs=pl.BlockSpec((tm, d), lambda i, f: (i, 0)),
        scratch_shapes=[pltpu.VMEM((tm, d), F32)],
        compiler_params=_cparams(2),
        name="ffn",
    )(hn, h, gate, g_final, w_ffn_in, w_ffn_in, w_ffn_out)


def _back(x2, p16, d, attn, mods, lw, tm, rows_per_mod):
    mix = _mix(d, attn, p16, lw["w_grp"], lw["pool_scale"], lw["w_up_pool"], lw["w_up_attn"], tm)
    h, hn = _out(mix, x2, mods[2], mods[3], mods[4], lw["g2"], lw["w_out"], tm, rows_per_mod)
    return _ffn(hn, h, mods[5], lw["g_final"], lw["w_ffn_in"], lw["w_ffn_out"], tm, rows_per_mod)


def kernel(x_prompt, x_sample, cache_k, cache_v, cache_idx_k, state_pool, page_table, c_prompt, c_sample,
           w_ada, b_ada, g_norm1, w_in, w_pool_grp, pool_scale, w_up_pool, w_up_attn, w_out, g_norm2,
           w_ffn_in, w_ffn_out, g_final):
    batch, seq, dm = x_prompt.shape
    nb, n_new, _ = x_sample.shape
    depth = w_ada.shape[0]
    assert depth == 1, "single-layer step"
    n_phys, page = cache_k.shape[1], cache_k.shape[2]
    n_pages = page_table.shape[1]
    past = n_pages * page
    pool_w = state_pool.shape[-1]
    kv_w = N_KV_HEADS * HEAD_DIM
    n_heads = dm // HEAD_DIM
    n_grp = n_heads // N_KV_HEADS
    idx_w = N_IDX_HEADS * IDX_DIM
    assert page == LANES and n_new == 4 and seq % PA_CHUNK == 0

    w_t = jnp.swapaxes(w_in[0], 0, 1)
    assert (pool_w, dm, kv_w, idx_w) == (2 * FRONT_TN, 4 * FRONT_TN, FRONT_TN, 2 * FRONT_TN)
    o_q, o_k, o_v, o_qi = pool_w, pool_w + dm, pool_w + dm + kv_w, pool_w + dm + 2 * kv_w
    o_ki = o_qi + idx_w
    o_ga = o_ki + IDX_DIM + N_IDX_HEADS
    o_gb = o_ga + dm
    assert o_gb + dm == w_t.shape[0] and o_ki + FRONT_TN <= w_t.shape[0]
    tile_rows = ([o_q + i * FRONT_TN for i in range(4)] + [o_ga + i * FRONT_TN for i in range(4)]
                 + [o_gb + i * FRONT_TN for i in range(4)] + [o_qi, o_qi + FRONT_TN, 0, FRONT_TN, o_k, o_v, o_ki])
    assert len(tile_rows) == N_FRONT_TILES and all(r % 8 == 0 for r in tile_rows)
    lw = {"w_grp": w_pool_grp[0].astype(BF16), "pool_scale": pool_scale[0].reshape(1, pool_w),
          "w_up_pool": w_up_pool[0].astype(BF16), "w_up_attn": w_up_attn[0].astype(BF16),
          "w_out": w_out[0].astype(BF16), "g2": g_norm2[0].reshape(1, dm), "g_final": g_final.reshape(1, dm),
          "w_ffn_in": w_ffn_in[0].astype(BF16), "w_ffn_out": w_ffn_out[0].astype(BF16)}
    g1 = g_norm1[0].reshape(1, dm)

    n_c = batch + nb
    c_all = jnp.concatenate([c_prompt, c_sample, jnp.zeros((-n_c % 8, dm), F32)], axis=0)
    mod = _ada(c_all, w_ada[0], b_ada[0])
    mods_p = [mod[:batch, i * dm:(i + 1) * dm].reshape(batch, 1, dm) for i in range(6)]
    mods_s = [jnp.repeat(mod[batch:n_c, i * dm:(i + 1) * dm], n_new, axis=0) for i in range(6)]

    xp = x_prompt.reshape(batch * seq, dm)
    p16, p32 = _front(xp, g1, mods_p[0], mods_p[1], w_t, tile_rows, 1024, seq)
    p32b = p32.reshape(batch, seq, W32)
    k_prompt = p32b[:, :, pool_w:pool_w + kv_w].reshape(1, batch, seq, N_KV_HEADS, HEAD_DIM)
    v_prompt = p32b[:, :, pool_w + kv_w:pool_w + 2 * kv_w].reshape(1, batch, seq, N_KV_HEADS, HEAD_DIM)
    o_small = pool_w + 2 * kv_w
    ki_p = p32b[:, :, o_small:o_small + IDX_DIM]
    wi_p = p32b[:, :, o_small + IDX_DIM:o_small + IDX_DIM + N_IDX_HEADS]
    pool_prompt = p32b[:, seq - POOL_BUF:, :pool_w][None]
    ki16 = ki_p.astype(BF16)
    zeros = jnp.zeros_like(ki16)
    kia = jnp.concatenate([ki16, zeros], axis=-1)
    kib = jnp.concatenate([zeros, ki16], axis=-1)
    d_p = _pool_d_prompt(p32, batch, seq, pool_w).reshape(batch * seq, pool_w)
    wit_p = jnp.swapaxes(wi_p, 1, 2)
    vt_p = jnp.swapaxes(p16.reshape(batch, seq, W16)[:, :, 15 * FRONT_TN:16 * FRONT_TN], 1, 2)
    vt_p = jnp.concatenate([vt_p.reshape(batch, N_KV_HEADS, HEAD_DIM, seq),
                            jnp.ones((batch, N_KV_HEADS, V_ROWS - HEAD_DIM, seq), BF16)], axis=2)
    vt_p = vt_p.reshape(batch, N_KV_HEADS * V_ROWS, seq)
    attn_p = _p_attn(p16, wit_p, kia, kib, vt_p, batch, seq, min(TOPK_MAX, seq // 4)).reshape(batch * seq, dm)
    y_prompt = _back(xp, p16, d_p, attn_p, mods_p, lw, 512, seq).reshape(batch, seq, dm)

    xs = x_sample.reshape(nb * n_new, dm)
    s16, s32 = _front(xs, g1, mods_s[0], mods_s[1], w_t, tile_rows, nb * n_new, 0)
    s32b = s32.reshape(nb, n_new, W32)
    k_new = s32b[:, :, pool_w:pool_w + kv_w]
    v_new = s32b[:, :, pool_w + kv_w:pool_w + 2 * kv_w]
    ki_new = s32b[:, :, o_small:o_small + IDX_DIM]
    wi_s = s32b[:, :, o_small + IDX_DIM:o_small + IDX_DIM + N_IDX_HEADS]
    seq_s = jnp.concatenate([state_pool[0], s32b[:, :, :pool_w]], axis=1)
    pool_sample = seq_s[:, n_new:][None]
    d_s = _pool_d_sample(jnp.swapaxes(seq_s, 0, 1), n_new)
    d_s = jnp.swapaxes(d_s, 0, 1).reshape(nb * n_new, pool_w)

    s16b = s16.reshape(nb, n_new, W16)
    hx = N_IDX_HEADS * n_new
    qi_th = s16b[:, :, 3 * dm:3 * dm + idx_w].reshape(nb, hx, IDX_DIM)
    w_th = (wi_s * (IDX_DIM ** -0.5 * N_IDX_HEADS ** -0.5)).reshape(nb, hx, 1)
    ki_new_t = jnp.swapaxes(jnp.pad(ki_new, ((0, 0), (0, page - n_new), (0, 0))), 1, 2)
    cache_ik_t = jnp.swapaxes(cache_idx_k[0], 1, 2)
    scores8 = _s_score(page_table, qi_th, w_th, ki_new_t, cache_ik_t, n_new)
    topk_s = min(TOPK_MAX, (past + n_new) // 4)
    width = scores8.shape[-1]
    sel_t = _s_sel(scores8[:, :n_new].reshape(nb * n_new, width).T, topk_s)
    sel4 = sel_t.T.reshape(nb, n_new, width)
    sel8 = jnp.concatenate([sel4, sel4], axis=1)

    q_ht = jnp.swapaxes(s16b[:, :, :dm].reshape(nb, n_new, n_heads, HEAD_DIM), 1, 2)
    q_ht = q_ht.reshape(nb, n_heads * n_new, HEAD_DIM)
    pr = page * N_KV_HEADS
    k_new2d = jnp.pad(k_new.astype(BF16), ((0, 0), (0, 8 - n_new), (0, 0))).reshape(nb, 8 * N_KV_HEADS, HEAD_DIM)
    v_new2d = jnp.pad(v_new.astype(BF16), ((0, 0), (0, 8 - n_new), (0, 0))).reshape(nb, 8 * N_KV_HEADS, HEAD_DIM)
    ck2d = cache_k[0].reshape(n_phys, pr, HEAD_DIM)
    cv2d = cache_v[0].reshape(n_phys, pr, HEAD_DIM)
    o_s = _s_attn(page_table, q_ht, sel8, k_new2d, v_new2d, ck2d, cv2d, n_grp, n_new)
    attn_s = jnp.swapaxes(o_s.reshape(nb, n_heads, n_new, HEAD_DIM), 1, 2).reshape(nb * n_new, dm)
    y_sample = _back(xs, s16, d_s, attn_s, mods_s, lw, nb * n_new, 0).reshape(nb, n_new, dm)

    k_sample = k_new.reshape(1, nb, n_new, N_KV_HEADS, HEAD_DIM)
    v_sample = v_new.reshape(1, nb, n_new, N_KV_HEADS, HEAD_DIM)
    return (y_prompt, y_sample, k_prompt, v_prompt, ki_p[None], pool_prompt,
            k_sample, v_sample, ki_new[None], pool_sample)
```

```python
import functools
import math

import jax
import jax.numpy as jnp
from jax import lax
from jax.experimental import pallas as pl
from jax.experimental.pallas import tpu as pltpu

F32 = jnp.float32
BF16 = jnp.bfloat16

RMS_EPS = 1e-6
POOL_WINDOWS = (2, 4, 8, 16)
POOL_BUF = max(POOL_WINDOWS) - 1
HEAD_DIM = 128
N_KV_HEADS = 4
N_IDX_HEADS = 16
IDX_DIM = 64
TOPK_MAX = 256
Q_BLOCK = 128
LANES = 128
NEG = -1e30
Q_SCALE = HEAD_DIM ** -0.5 * math.log2(math.e)
V_ROWS = HEAD_DIM + 16
NT_DIMS = (((1,), (1,)), ((), ()))
VMEM_LIMIT = 56 * 1024 * 1024


def _cparams(n_axes):
    return pltpu.CompilerParams(dimension_semantics=("arbitrary",) * n_axes, vmem_limit_bytes=VMEM_LIMIT)


def _rms(x, g):
    return x * lax.rsqrt(jnp.mean(x * x, axis=-1, keepdims=True) + RMS_EPS) * g


def _ada_kernel(c_ref, w_ref, b_ref, o_ref):
    c = c_ref[...]
    s = (c * jax.nn.sigmoid(c)).astype(BF16)
    o_ref[...] = jnp.dot(s, w_ref[...].astype(BF16), preferred_element_type=F32) + b_ref[...]


def _ada(c_all, w_ada, b_ada):
    rows, d = c_all.shape
    n = w_ada.shape[1]
    tn = 512
    return pl.pallas_call(
        _ada_kernel,
        out_shape=jax.ShapeDtypeStruct((rows, n), F32),
        grid=(n // tn,),
        in_specs=[pl.BlockSpec((rows, d), lambda j: (0, 0)),
                  pl.BlockSpec((d, tn), lambda j: (0, j)),
                  pl.BlockSpec((1, tn), lambda j: (0, j))],
        out_specs=pl.BlockSpec((rows, tn), lambda j: (0, j)),
        compiler_params=_cparams(1),
        name="ada",
    )(c_all, w_ada, b_ada.reshape(1, n))


FRONT_TN = 512
N_FRONT_TILES = 19
N_Q_TILES = 4
F32_TILE0 = 14
K_TILE, V_TILE = 16, 17
W16 = 16 * FRONT_TN
W32 = 5 * FRONT_TN


def _front_kernel(x_ref, g_ref, sh_ref, sc_ref, wt_ref, o16_ref, o32_ref, u_ref):
    n = pl.program_id(1)

    @pl.when(n == 0)
    def _():
        u_ref[...] = (_rms(x_ref[...], g_ref[...]) * (1.0 + sc_ref[...]) + sh_ref[...]).astype(BF16)

    r = lax.dot_general(u_ref[...], wt_ref[...].astype(BF16), NT_DIMS, preferred_element_type=F32)

    @pl.when((n < F32_TILE0) | (n == K_TILE) | (n == V_TILE))
    def _():
        o16_ref[...] = (r * jnp.where(n < N_Q_TILES, Q_SCALE, 1.0)).astype(BF16)

    @pl.when(n >= F32_TILE0)
    def _():
        o32_ref[...] = r


def _lookup(n, table):
    out = jnp.int32(table[-1])
    for i, v in enumerate(table[:-1]):
        out = jnp.where(n == i, v, out)
    return out


def _o16_block(n):
    return jnp.where(n < F32_TILE0, n,
                     jnp.where(n < K_TILE, F32_TILE0 - 1, jnp.where(n <= V_TILE, n - 2, V_TILE - 2)))


def _mod_spec(mod, tm, rows_per_mod):
    d = mod.shape[-1]
    if mod.ndim == 3:
        tiles_per_batch = rows_per_mod // tm
        return pl.BlockSpec((None, 1, d), lambda i, *_: (i // tiles_per_batch, 0, 0))
    return pl.BlockSpec((tm, d), lambda i, *_: (i, 0))


def _front(x, g1, shift, scale, w_t, tile_rows, tm, rows_per_mod):
    m, d = x.shape
    return pl.pallas_call(
        _front_kernel,
        out_shape=(jax.ShapeDtypeStruct((m, W16), BF16), jax.ShapeDtypeStruct((m, W32), F32)),
        grid=(m // tm, N_FRONT_TILES),
        in_specs=[pl.BlockSpec((tm, d), lambda i, n: (i, 0)),
                  pl.BlockSpec((1, d), lambda i, n: (0, 0)),
                  _mod_spec(shift, tm, rows_per_mod),
                  _mod_spec(scale, tm, rows_per_mod),
                  pl.BlockSpec((pl.Element(FRONT_TN), pl.Element(d)),
                               lambda i, n: (_lookup(n, [r // 8 for r in tile_rows]) * 8, 0))],
        out_specs=(pl.BlockSpec((tm, FRONT_TN), lambda i, n: (i, _o16_block(n))),
                   pl.BlockSpec((tm, FRONT_TN), lambda i, n: (i, jnp.maximum(n - F32_TILE0, 0)))),
        scratch_shapes=[pltpu.VMEM((tm, d), BF16)],
        compiler_params=_cparams(2),
        name="front",
    )(x, g1, shift, scale, w_t)


def _pool_d_prompt_kernel(pin_ref, halo_ref, o_ref, buf_ref, *, tp, group):
    i = pl.program_id(1)
    cur = pin_ref[...]
    halo = jnp.where(i > 0, halo_ref[...], 0.0)
    hb = halo.shape[0]
    buf_ref[0:hb, :] = halo
    buf_ref[hb:hb + tp, :] = cur
    pos = i * tp + lax.broadcasted_iota(jnp.int32, (tp, 1), 0)
    for gi, w in enumerate(POOL_WINDOWS):
        sl = slice(gi * group, (gi + 1) * group)
        acc = cur[:, sl]
        for j in range(1, w):
            acc = acc + buf_ref[hb - j:hb - j + tp, sl]
        cnt = jnp.minimum(pos + 1, w).astype(F32)
        o_ref[:, sl] = (acc / cnt - cur[:, sl]).astype(BF16)


def _pool_d_prompt(p32, batch, seq, width):
    tp, hb = 512, 16
    group = width // len(POOL_WINDOWS)
    p32b = p32.reshape(batch, seq, p32.shape[-1])
    return pl.pallas_call(
        functools.partial(_pool_d_prompt_kernel, tp=tp, group=group),
        out_shape=jax.ShapeDtypeStruct((batch, seq, width), BF16),
        grid=(batch, seq // tp),
        in_specs=[pl.BlockSpec((None, tp, width), lambda b, i: (b, i, 0)),
                  pl.BlockSpec((None, hb, width), lambda b, i: (b, jnp.maximum(i * (tp // hb) - 1, 0), 0))],
        out_specs=pl.BlockSpec((None, tp, width), lambda b, i: (b, i, 0)),
        scratch_shapes=[pltpu.VMEM((tp + hb, width), F32)],
        compiler_params=_cparams(2),
        name="pool_d_prompt",
    )(p32b, p32b)


def _pool_d_sample_kernel(seq_ref, o_ref, *, n_new, group):
    for t in range(n_new):
        r = POOL_BUF + t
        for gi, w in enumerate(POOL_WINDOWS):
            sl = slice(gi * group, (gi + 1) * group)
            acc = seq_ref[r, :, sl]
            for j in range(1, w):
                acc = acc + seq_ref[r - j, :, sl]
            o_ref[t, :, sl] = (acc / float(w) - seq_ref[r, :, sl]).astype(BF16)


def _pool_d_sample(seq_t, n_new):
    rows, nb, width = seq_t.shape
    group = width // len(POOL_WINDOWS)
    return pl.pallas_call(
        functools.partial(_pool_d_sample_kernel, n_new=n_new, group=group),
        out_shape=jax.ShapeDtypeStruct((n_new, nb, width), BF16),
        grid=(1,),
        in_specs=[pl.BlockSpec((rows, nb, width), lambda i: (0, 0, 0))],
        out_specs=pl.BlockSpec((n_new, nb, width), lambda i: (0, 0, 0)),
        compiler_params=_cparams(1),
        name="pool_d_sample",
    )(seq_t)


def _key_to_float(key):
    bits = key ^ ((key >> 31) & jnp.int32(0x7FFFFFFF))
    return lax.bitcast_convert_type(bits, F32)


def _count_cols(sc_ref, n_chunks, chunk, flag):
    n_acc = 4

    def chunk_body(c, cnts):
        c0 = pl.multiple_of(c * chunk, chunk)
        s = sc_ref[pl.ds(c0, chunk), :]
        cnts = list(cnts)
        for r in range(chunk // 8):
            cnts[r % n_acc] = cnts[r % n_acc] + flag(s[r * 8:(r + 1) * 8], c0 + r * 8)
        return tuple(cnts)

    cnts = lax.fori_loop(0, n_chunks, chunk_body, tuple(jnp.zeros((8, LANES), jnp.int32) for _ in range(n_acc)))
    return jnp.sum(functools.reduce(jnp.add, cnts), axis=0, keepdims=True)


def _kth_largest_cols(sc_ref, n_chunks, chunk, k):
    int_min = jnp.int32(-(2 ** 31))

    def bit_body(it, key):
        cand = key ^ lax.shift_left(jnp.int32(1), 31 - it)
        cf = jnp.broadcast_to(_key_to_float(cand), (8, LANES))
        tot = _count_cols(sc_ref, n_chunks, chunk, lambda s, r0: jnp.where(s >= cf, 1, 0))
        return jnp.where(tot >= k, cand, key)

    key = lax.fori_loop(0, 32, bit_body, jnp.full((1, LANES), int_min, jnp.int32))
    return _key_to_float(key)


def _break_ties_cols(sc_ref, n_chunks, chunk, k, thr, n_rows):
    thr8 = jnp.broadcast_to(thr, (8, LANES))
    n_ge = _count_cols(sc_ref, n_chunks, chunk, lambda s, r0: jnp.where(s >= thr8, 1, 0))

    @pl.when(jnp.max(n_ge) > k)
    def _():
        need = k - _count_cols(sc_ref, n_chunks, chunk, lambda s, r0: jnp.where(s > thr8, 1, 0))
        row8 = lax.broadcasted_iota(jnp.int32, (8, LANES), 0)
        n_bits = (n_rows - 1).bit_length()

        def bit_body(it, last):
            cand = last | lax.shift_left(jnp.int32(1), n_bits - 1 - it)
            cand8 = jnp.broadcast_to(cand, (8, LANES))
            before = _count_cols(sc_ref, n_chunks, chunk,
                                 lambda s, r0: jnp.where(s == thr8, jnp.where(row8 + r0 < cand8, 1, 0), 0))
            return jnp.where(before < need, cand, last)

        last = lax.fori_loop(0, n_bits, bit_body, jnp.zeros((1, LANES), jnp.int32))

        def drop(c, carry):
            c0 = pl.multiple_of(c * chunk, chunk)
            s = sc_ref[pl.ds(c0, chunk), :]
            row = c0 + lax.broadcasted_iota(jnp.int32, (chunk, LANES), 0)
            sc_ref[pl.ds(c0, chunk), :] = jnp.where(s == thr, jnp.where(row > last, -jnp.inf, s), s)
            return carry

        lax.fori_loop(0, n_chunks, drop, 0)


PA_CHUNK = 512


def _p_attn_kernel(q_ref, qi_ref, wit_ref, kia_ref, kib_ref, k_ref, vt_ref, o_ref, sc_ref, *, topk):
    blk = pl.program_id(1)
    t0 = blk * Q_BLOCK
    n_ch = (t0 + Q_BLOCK + PA_CHUNK - 1) // PA_CHUNK
    tok = t0 + lax.broadcasted_iota(jnp.int32, (1, Q_BLOCK), 1)
    n_grp = (q_ref.shape[1] // HEAD_DIM) // N_KV_HEADS
    n_pairs = N_IDX_HEADS // 2

    wit = wit_ref[...] * (IDX_DIM ** -0.5 * N_IDX_HEADS ** -0.5)

    def score_chunk(c, carry):
        c0 = pl.multiple_of(c * PA_CHUNK, PA_CHUNK)
        ka = kia_ref[pl.ds(c0, PA_CHUNK), :]
        kb = kib_ref[pl.ds(c0, PA_CHUNK), :]
        acc = jnp.zeros((PA_CHUNK, Q_BLOCK), F32)
        for pp in range(n_pairs // 2):
            qp = jnp.concatenate([qi_ref[:, (2 * pp + i) * LANES:(2 * pp + i + 1) * LANES] for i in range(2)], axis=0)
            da = lax.dot_general(ka, qp, NT_DIMS, preferred_element_type=F32)
            db = lax.dot_general(kb, qp, NT_DIMS, preferred_element_type=F32)
            for i in range(2):
                h = 2 * (2 * pp + i)
                sl = slice(i * Q_BLOCK, (i + 1) * Q_BLOCK)
                acc = acc + jnp.maximum(da[:, sl], 0.0) * wit[h:h + 1, :]
                acc = acc + jnp.maximum(db[:, sl], 0.0) * wit[h + 1:h + 2, :]
        keypos = c0 + lax.broadcasted_iota(jnp.int32, (PA_CHUNK, Q_BLOCK), 0)
        sc_ref[pl.ds(c0, PA_CHUNK), :] = jnp.where(keypos <= tok, acc, -jnp.inf)
        return carry

    lax.fori_loop(0, n_ch, score_chunk, 0)

    thr = _kth_largest_cols(sc_ref, n_ch, PA_CHUNK, topk)
    thr = jnp.where(tok + 1 <= topk, jnp.finfo(F32).min, thr)
    _break_ties_cols(sc_ref, n_ch, PA_CHUNK, topk, thr, sc_ref.shape[0])

    width = n_grp * Q_BLOCK

    def attn_chunk(c, carry):
        c0 = pl.multiple_of(c * PA_CHUNK, PA_CHUNK)
        bias = jnp.where(sc_ref[pl.ds(c0, PA_CHUNK), :] >= thr, 0.0, NEG)
        bias = jnp.concatenate([bias] * n_grp, axis=1)

        def logits(j):
            qj = jnp.concatenate([q_ref[:, (j * n_grp + g) * HEAD_DIM:(j * n_grp + g + 1) * HEAD_DIM]
                                  for g in range(n_grp)], axis=0)
            kc = k_ref[pl.ds(c0, PA_CHUNK), j * HEAD_DIM:(j + 1) * HEAD_DIM]
            return lax.dot_general(kc, qj, NT_DIMS, preferred_element_type=F32) + bias

        def weights(j, s):
            m = carry[j][0]
            m_new = jnp.maximum(m, jnp.max(s, axis=0, keepdims=True))
            return m_new, jnp.exp2(m - m_new), jnp.exp2(s - m_new).astype(BF16)

        def accumulate(j, alpha, p):
            vt = vt_ref[j * V_ROWS:(j + 1) * V_ROWS, pl.ds(c0, PA_CHUNK)]
            return alpha * carry[j][1] + jnp.dot(vt, p, preferred_element_type=F32)

        s, w, out = {}, {}, []
        for step in range(N_KV_HEADS + 2):
            if step < N_KV_HEADS:
                s[step] = logits(step)
            if 0 <= step - 1 < N_KV_HEADS:
                w[step - 1] = weights(step - 1, s.pop(step - 1))
            if 0 <= step - 2 < N_KV_HEADS:
                m_new, alpha, p = w.pop(step - 2)
                out.append((m_new, accumulate(step - 2, alpha, p)))
        return tuple(out)

    init = tuple((jnp.full((1, width), NEG, F32), jnp.zeros((V_ROWS, width), F32)) for _ in range(N_KV_HEADS))
    final = lax.fori_loop(0, n_ch, attn_chunk, init)
    for j, (_, acc) in enumerate(final):
        o_t = acc[:HEAD_DIM] * (1.0 / acc[HEAD_DIM:HEAD_DIM + 1])
        for g in range(n_grp):
            h = j * n_grp + g
            o_ref[:, h * HEAD_DIM:(h + 1) * HEAD_DIM] = o_t[:, g * Q_BLOCK:(g + 1) * Q_BLOCK].T.astype(BF16)


def _p_attn(p16, wit, kia, kib, vt, batch, seq, topk):
    d_attn = 2048
    p16b = p16.reshape(batch, seq, W16)
    kv_w = N_KV_HEADS * HEAD_DIM
    return pl.pallas_call(
        functools.partial(_p_attn_kernel, topk=topk),
        out_shape=jax.ShapeDtypeStruct((batch, seq, d_attn), BF16),
        grid=(batch, seq // Q_BLOCK),
        in_specs=[pl.BlockSpec((None, Q_BLOCK, d_attn), lambda b, i: (b, i, 0)),
                  pl.BlockSpec((None, Q_BLOCK, 1024), lambda b, i: (b, i, 6)),
                  pl.BlockSpec((None, N_IDX_HEADS, Q_BLOCK), lambda b, i: (b, 0, i)),
                  pl.BlockSpec((None, seq, LANES), lambda b, i: (b, 0, 0)),
                  pl.BlockSpec((None, seq, LANES), lambda b, i: (b, 0, 0)),
                  pl.BlockSpec((None, seq, kv_w), lambda b, i: (b, 0, 14)),
                  pl.BlockSpec((None, N_KV_HEADS * V_ROWS, seq), lambda b, i: (b, 0, 0))],
        out_specs=pl.BlockSpec((None, Q_BLOCK, d_attn), lambda b, i: (b, i, 0)),
        scratch_shapes=[pltpu.VMEM((seq, Q_BLOCK), F32)],
        compiler_params=_cparams(2),
        name="p_attn",
    )(p16b, p16b, wit, kia, kib, p16b, vt)


S_SCORE_GROUP = 4


def _s_score_kernel(pt_ref, qi_ref, w_ref, kin_ref, *rest, n_pages, page, n_new, group):
    del pt_ref
    ik_refs, o_ref = rest[:group * n_pages], rest[group * n_pages]
    for g in range(group):
        qi = qi_ref[g]
        w = w_ref[g]

        def piece(ik_t, qi=qi, w=w):
            d = jnp.dot(qi, ik_t.astype(BF16), preferred_element_type=F32)
            x = jnp.maximum(d, 0.0) * w
            return jnp.concatenate([jnp.sum(x[t * N_IDX_HEADS:(t + 1) * N_IDX_HEADS], axis=0, keepdims=True)
                                    for t in range(n_new)], axis=0)

        for p in range(n_pages):
            o_ref[g, :, p * page:(p + 1) * page] = piece(ik_refs[g * n_pages + p][...])
        new = piece(kin_ref[g])
        col = lax.broadcasted_iota(jnp.int32, new.shape, 1)
        tok = lax.broadcasted_iota(jnp.int32, new.shape, 0)
        o_ref[g, :, n_pages * page:(n_pages + 1) * page] = jnp.where(col <= tok, new, -jnp.inf)


def _page_specs(n_pages, shape, group=1):
    zeros = (0,) * len(shape)
    return [pl.BlockSpec((None,) + shape,
                         functools.partial(lambda b, pt, g, p: (pt[b * group + g, p],) + zeros, g=g, p=p))
            for g in range(group) for p in range(n_pages)]


def _s_score(page_table, qi_th, w_th, ki_new_t, cache_ik_t, n_new):
    nb, n_pages = page_table.shape
    page = cache_ik_t.shape[2]
    width = (n_pages + 1) * page
    hx = qi_th.shape[1]
    group = S_SCORE_GROUP
    assert nb % group == 0
    return pl.pallas_call(
        functools.partial(_s_score_kernel, n_pages=n_pages, page=page, n_new=n_new, group=group),
        out_shape=jax.ShapeDtypeStruct((nb, n_new, width), F32),
        grid_spec=pltpu.PrefetchScalarGridSpec(
            num_scalar_prefetch=1,
            grid=(nb // group,),
            in_specs=[pl.BlockSpec((group, hx, IDX_DIM), lambda b, pt: (b, 0, 0)),
                      pl.BlockSpec((group, hx, 1), lambda b, pt: (b, 0, 0)),
                      pl.BlockSpec((group, IDX_DIM, page), lambda b, pt: (b, 0, 0))]
            + _page_specs(n_pages, (IDX_DIM, page), group),
            out_specs=pl.BlockSpec((group, n_new, width), lambda b, pt: (b, 0, 0))),
        compiler_params=_cparams(1),
        name="s_score",
    )(page_table, qi_th, w_th, ki_new_t, *([cache_ik_t] * (group * n_pages)))


def _s_sel_kernel(sc_ref, o_ref, buf_ref, *, topk, chunk):
    n_rows = sc_ref.shape[0]
    buf_ref[...] = sc_ref[...]
    thr = _kth_largest_cols(buf_ref, n_rows // chunk, chunk, topk)
    _break_ties_cols(buf_ref, n_rows // chunk, chunk, topk, thr, n_rows)
    o_ref[...] = jnp.where(buf_ref[...] >= thr, 1.0, 0.0)


def _s_sel(scores_t, topk):
    n_rows, n_q = scores_t.shape
    return pl.pallas_call(
        functools.partial(_s_sel_kernel, topk=topk, chunk=LANES),
        out_shape=jax.ShapeDtypeStruct((n_rows, n_q), F32),
        grid=(n_q // LANES,),
        in_specs=[pl.BlockSpec((n_rows, LANES), lambda i: (0, i))],
        out_specs=pl.BlockSpec((n_rows, LANES), lambda i: (0, i)),
        scratch_shapes=[pltpu.VMEM((n_rows, LANES), F32)],
        compiler_params=_cparams(1),
        name="s_sel",
    )(scores_t)


def _s_attn_kernel(pt_ref, q_ref, sel_ref, kn_ref, vn_ref, *rest, n_pages, n_grp, n_new):
    del pt_ref
    k_refs, v_refs, o_ref = rest[:n_pages], rest[n_pages:2 * n_pages], rest[2 * n_pages]
    q = q_ref[...]
    rows = q.shape[0]
    pr = k_refs[0].shape[0]
    page = pr // N_KV_HEADS
    sel = sel_ref[...].astype(BF16)
    sel = jnp.concatenate([sel[:, p * page:(p + 1) * page] for p in range(n_pages + 1)], axis=0)
    spread = (lax.broadcasted_iota(jnp.int32, (page, pr), 1) // N_KV_HEADS
              == lax.broadcasted_iota(jnp.int32, (page, pr), 0)).astype(BF16)
    sel = jnp.dot(sel, spread, preferred_element_type=F32)
    lane_kv = lax.broadcasted_iota(jnp.int32, (rows, pr), 1) % N_KV_HEADS
    row_kv = lax.broadcasted_iota(jnp.int32, (rows, pr), 0) // (n_grp * n_new)
    head_bias = jnp.where(lane_kv == row_kv, 0.0, NEG)
    pieces = []
    for p in range(n_pages + 1):
        kp = kn_ref[...] if p == n_pages else k_refs[p][...].astype(BF16)
        w = kp.shape[0]
        s = lax.dot_general(q, kp, NT_DIMS, preferred_element_type=F32)
        bias = jnp.concatenate([(sel[p * 8:(p + 1) * 8, :w] - 1.0) * -NEG] * (rows // 8), axis=0)
        pieces.append(s + (bias + head_bias[:, :w]))
    m = functools.reduce(jnp.maximum, [jnp.max(s, axis=1, keepdims=True) for s in pieces])
    l = jnp.zeros((rows, 1), F32)
    o = jnp.zeros((rows, HEAD_DIM), F32)
    for p in range(n_pages + 1):
        e = jnp.exp2(pieces[p] - m)
        l = l + jnp.sum(e, axis=1, keepdims=True)
        vp = vn_ref[...] if p == n_pages else v_refs[p][...].astype(BF16)
        o = o + jnp.dot(e.astype(BF16), vp, preferred_element_type=F32)
    o_ref[...] = (o / l).astype(BF16)


def _s_attn(page_table, q_ht, sel8, k_new2d, v_new2d, cache_k2d, cache_v2d, n_grp, n_new):
    nb, n_pages = page_table.shape
    pr = cache_k2d.shape[1]
    pr_new = k_new2d.shape[1]
    rows = q_ht.shape[1]
    width = sel8.shape[-1]
    return pl.pallas_call(
        functools.partial(_s_attn_kernel, n_pages=n_pages, n_grp=n_grp, n_new=n_new),
        out_shape=jax.ShapeDtypeStruct((nb, rows, HEAD_DIM), BF16),
        grid_spec=pltpu.PrefetchScalarGridSpec(
            num_scalar_prefetch=1,
            grid=(nb,),
            in_specs=[pl.BlockSpec((None, rows, HEAD_DIM), lambda b, pt: (b, 0, 0)),
                      pl.BlockSpec((None, 8, width), lambda b, pt: (b, 0, 0)),
                      pl.BlockSpec((None, pr_new, HEAD_DIM), lambda b, pt: (b, 0, 0)),
                      pl.BlockSpec((None, pr_new, HEAD_DIM), lambda b, pt: (b, 0, 0))]
            + _page_specs(n_pages, (pr, HEAD_DIM)) + _page_specs(n_pages, (pr, HEAD_DIM)),
            out_specs=pl.BlockSpec((None, rows, HEAD_DIM), lambda b, pt: (b, 0, 0))),
        compiler_params=_cparams(1),
        name="s_attn",
    )(page_table, q_ht, sel8, k_new2d, v_new2d, *([cache_k2d] * n_pages), *([cache_v2d] * n_pages))


def _mix_kernel(d_ref, a_ref, ga_ref, gb_ref, wg_ref, ps_ref, wup_ref, wua_ref, o_ref):
    n_groups, group = wg_ref.shape[0], wg_ref.shape[1]
    y = jnp.concatenate(
        [jnp.dot(d_ref[:, g * group:(g + 1) * group], wg_ref[g], preferred_element_type=F32)
         for g in range(n_groups)], axis=1) * ps_ref[...]
    yp = jnp.dot(y.astype(BF16), wup_ref[...], preferred_element_type=F32)
    ya = jnp.dot(a_ref[...], wua_ref[...], preferred_element_type=F32)
    o_ref[...] = (jax.nn.sigmoid(ga_ref[...].astype(F32)) * yp
                  + jax.nn.sigmoid(gb_ref[...].astype(F32)) * ya).astype(BF16)


def _mix(d, attn, p16, w_grp, pool_scale, w_up_pool, w_up_attn, tm):
    m, dm = attn.shape
    pw = d.shape[1]
    const = lambda *shape: pl.BlockSpec(shape, lambda i: (0,) * len(shape))
    return pl.pallas_call(
        _mix_kernel,
        out_shape=jax.ShapeDtypeStruct((m, dm), BF16),
        grid=(m // tm,),
        in_specs=[pl.BlockSpec((tm, pw), lambda i: (i, 0)),
                  pl.BlockSpec((tm, dm), lambda i: (i, 0)),
                  pl.BlockSpec((tm, dm), lambda i: (i, 1)),
                  pl.BlockSpec((tm, dm), lambda i: (i, 2)),
                  const(*w_grp.shape), const(1, pw), const(*w_up_pool.shape), const(*w_up_attn.shape)],
        out_specs=pl.BlockSpec((tm, dm), lambda i: (i, 0)),
        compiler_params=_cparams(1),
        name="mix",
    )(d, attn, p16, p16, w_grp, pool_scale, w_up_pool, w_up_attn)


def _out_kernel(mix_ref, x_ref, gate_ref, sh_ref, sc_ref, g_ref, w_ref, h_ref, hn_ref):
    h = x_ref[...] + gate_ref[...] * jnp.dot(mix_ref[...], w_ref[...], preferred_element_type=F32)
    h_ref[...] = h
    hn_ref[...] = (_rms(h, g_ref[...]) * (1.0 + sc_ref[...]) + sh_ref[...]).astype(BF16)


def _out(mix, x, gate, shift, scale, g2, w_out, tm, rows_per_mod):
    m, d = x.shape
    return pl.pallas_call(
        _out_kernel,
        out_shape=(jax.ShapeDtypeStruct((m, d), F32), jax.ShapeDtypeStruct((m, d), BF16)),
        grid=(m // tm,),
        in_specs=[pl.BlockSpec((tm, d), lambda i: (i, 0)),
                  pl.BlockSpec((tm, d), lambda i: (i, 0)),
                  _mod_spec(gate, tm, rows_per_mod),
                  _mod_spec(shift, tm, rows_per_mod),
                  _mod_spec(scale, tm, rows_per_mod),
                  pl.BlockSpec((1, d), lambda i: (0, 0)),
                  pl.BlockSpec((d, d), lambda i: (0, 0))],
        out_specs=(pl.BlockSpec((tm, d), lambda i: (i, 0)), pl.BlockSpec((tm, d), lambda i: (i, 0))),
        compiler_params=_cparams(1),
        name="out",
    )(mix, x, gate, shift, scale, g2, w_out)


FFN_TF = 512


def _ffn_kernel(hn_ref, h_ref, gate_ref, gf_ref, wg_ref, wu_ref, wo_ref, o_ref, acc_ref):
    f = pl.program_id(1)

    @pl.when(f == 0)
    def _():
        acc_ref[...] = jnp.zeros_like(acc_ref)

    hn = hn_ref[...]
    a = jnp.dot(hn, wg_ref[...], preferred_element_type=F32)
    u = jnp.dot(hn, wu_ref[...], preferred_element_type=F32)
    z = (a * jax.nn.sigmoid(a) * u).astype(BF16)
    acc_ref[...] += jnp.dot(z, wo_ref[...], preferred_element_type=F32)

    @pl.when(f == pl.num_programs(1) - 1)
    def _():
        o_ref[...] = _rms(h_ref[...] + gate_ref[...] * acc_ref[...], gf_ref[...])


def _ffn(hn, h, gate, g_final, w_ffn_in, w_ffn_out, tm, rows_per_mod):
    m, d = h.shape
    d_ff = w_ffn_out.shape[0]
    n_f = d_ff // FFN_TF
    return pl.pallas_call(
        _ffn_kernel,
        out_shape=jax.ShapeDtypeStruct((m, d), F32),
        grid=(m // tm, n_f),
        in_specs=[pl.BlockSpec((tm, d), lambda i, f: (i, 0)),
                  pl.BlockSpec((tm, d), lambda i, f: (i, 0)),
                  _mod_spec(gate, tm, rows_per_mod),
                  pl.BlockSpec((1, d), lambda i, f: (0, 0)),
                  pl.BlockSpec((d, FFN_TF), lambda i, f: (0, f)),
                  pl.BlockSpec((d, FFN_TF), lambda i, f: (0, f + n_f)),
                  pl.BlockSpec((FFN_TF, d), lambda i, f: (f, 0))],
        out_specs=pl.BlockSpec((tm, d), lambda i, f: (i, 0)),
        scratch_shapes=[pltpu.VMEM((tm, d), F32)],
        compiler_params=_cparams(2),
        name="ffn",
    )(hn, h, gate, g_final, w_ffn_in, w_ffn_in, w_ffn_out)


def _back(x2, p16, d, attn, mods, lw, tm, rows_per_mod):
    mix = _mix(d, attn, p16, lw["w_grp"], lw["pool_scale"], lw["w_up_pool"], lw["w_up_attn"], tm)
    h, hn = _out(mix, x2, mods[2], mods[3], mods[4], lw["g2"], lw["w_out"], tm, rows_per_mod)
    return _ffn(hn, h, mods[5], lw["g_final"], lw["w_ffn_in"], lw["w_ffn_out"], tm, rows_per_mod)


def kernel(x_prompt, x_sample, cache_k, cache_v, cache_idx_k, state_pool, page_table, c_prompt, c_sample,
           w_ada, b_ada, g_norm1, w_in, w_pool_grp, pool_scale, w_up_pool, w_up_attn, w_out, g_norm2,
           w_ffn_in, w_ffn_out, g_final):
    batch, seq, dm = x_prompt.shape
    nb, n_new, _ = x_sample.shape
    depth = w_ada.shape[0]
    assert depth == 1, "single-layer step"
    n_phys, page = cache_k.shape[1], cache_k.shape[2]
    n_pages = page_table.shape[1]
    past = n_pages * page
    pool_w = state_pool.shape[-1]
    kv_w = N_KV_HEADS * HEAD_DIM
    n_heads = dm // HEAD_DIM
    n_grp = n_heads // N_KV_HEADS
    idx_w = N_IDX_HEADS * IDX_DIM
    assert page == LANES and n_new == 4 and seq % PA_CHUNK == 0

    w_t = jnp.swapaxes(w_in[0], 0, 1)
    assert (pool_w, dm, kv_w, idx_w) == (2 * FRONT_TN, 4 * FRONT_TN, FRONT_TN, 2 * FRONT_TN)
    o_q, o_k, o_v, o_qi = pool_w, pool_w + dm, pool_w + dm + kv_w, pool_w + dm + 2 * kv_w
    o_ki = o_qi + idx_w
    o_ga = o_ki + IDX_DIM + N_IDX_HEADS
    o_gb = o_ga + dm
    assert o_gb + dm == w_t.shape[0] and o_ki + FRONT_TN <= w_t.shape[0]
    tile_rows = ([o_q + i * FRONT_TN for i in range(4)] + [o_ga + i * FRONT_TN for i in range(4)]
                 + [o_gb + i * FRONT_TN for i in range(4)] + [o_qi, o_qi + FRONT_TN, 0, FRONT_TN, o_k, o_v, o_ki])
    assert len(tile_rows) == N_FRONT_TILES and all(r % 8 == 0 for r in tile_rows)
    lw = {"w_grp": w_pool_grp[0].astype(BF16), "pool_scale": pool_scale[0].reshape(1, pool_w),
          "w_up_pool": w_up_pool[0].astype(BF16), "w_up_attn": w_up_attn[0].astype(BF16),
          "w_out": w_out[0].astype(BF16), "g2": g_norm2[0].reshape(1, dm), "g_final": g_final.reshape(1, dm),
          "w_ffn_in": w_ffn_in[0].astype(BF16), "w_ffn_out": w_ffn_out[0].astype(BF16)}
    g1 = g_norm1[0].reshape(1, dm)

    n_c = batch + nb
    c_all = jnp.concatenate([c_prompt, c_sample, jnp.zeros((-n_c % 8, dm), F32)], axis=0)
    mod = _ada(c_all, w_ada[0], b_ada[0])
    mods_p = [mod[:batch, i * dm:(i + 1) * dm].reshape(batch, 1, dm) for i in range(6)]
    mods_s = [jnp.repeat(mod[batch:n_c, i * dm:(i + 1) * dm], n_new, axis=0) for i in range(6)]

    xp = x_prompt.reshape(batch * seq, dm)
    p16, p32 = _front(xp, g1, mods_p[0], mods_p[1], w_t, tile_rows, 1024, seq)
    p32b = p32.reshape(batch, seq, W32)
    k_prompt = p32b[:, :, pool_w:pool_w + kv_w].reshape(1, batch, seq, N_KV_HEADS, HEAD_DIM)
    v_prompt = p32b[:, :, pool_w + kv_w:pool_w + 2 * kv_w].reshape(1, batch, seq, N_KV_HEADS, HEAD_DIM)
    o_small = pool_w + 2 * kv_w
    ki_p = p32b[:, :, o_small:o_small + IDX_DIM]
    wi_p = p32b[:, :, o_small + IDX_DIM:o_small + IDX_DIM + N_IDX_HEADS]
    pool_prompt = p32b[:, seq - POOL_BUF:, :pool_w][None]
    ki16 = ki_p.astype(BF16)
    zeros = jnp.zeros_like(ki16)
    kia = jnp.concatenate([ki16, zeros], axis=-1)
    kib = jnp.concatenate([zeros, ki16], axis=-1)
    d_p = _pool_d_prompt(p32, batch, seq, pool_w).reshape(batch * seq, pool_w)
    wit_p = jnp.swapaxes(wi_p, 1, 2)
    vt_p = jnp.swapaxes(p16.reshape(batch, seq, W16)[:, :, W16 - kv_w:], 1, 2)
    vt_p = jnp.concatenate([vt_p.reshape(batch, N_KV_HEADS, HEAD_DIM, seq),
                            jnp.ones((batch, N_KV_HEADS, V_ROWS - HEAD_DIM, seq), BF16)], axis=2)
    vt_p = vt_p.reshape(batch, N_KV_HEADS * V_ROWS, seq)
    attn_p = _p_attn(p16, wit_p, kia, kib, vt_p, batch, seq, min(TOPK_MAX, seq // 4)).reshape(batch * seq, dm)
    y_prompt = _back(xp, p16, d_p, attn_p, mods_p, lw, 512, seq).reshape(batch, seq, dm)

    xs = x_sample.reshape(nb * n_new, dm)
    s16, s32 = _front(xs, g1, mods_s[0], mods_s[1], w_t, tile_rows, nb * n_new, 0)
    s32b = s32.reshape(nb, n_new, W32)
    k_new = s32b[:, :, pool_w:pool_w + kv_w]
    v_new = s32b[:, :, pool_w + kv_w:pool_w + 2 * kv_w]
    ki_new = s32b[:, :, o_small:o_small + IDX_DIM]
    wi_s = s32b[:, :, o_small + IDX_DIM:o_small + IDX_DIM + N_IDX_HEADS]
    seq_s = jnp.concatenate([state_pool[0], s32b[:, :, :pool_w]], axis=1)
    pool_sample = seq_s[:, n_new:][None]
    d_s = _pool_d_sample(jnp.swapaxes(seq_s, 0, 1), n_new)
    d_s = jnp.swapaxes(d_s, 0, 1).reshape(nb * n_new, pool_w)

    s16b = s16.reshape(nb, n_new, W16)
    hx = N_IDX_HEADS * n_new
    qi_th = s16b[:, :, 3 * dm:3 * dm + idx_w].reshape(nb, hx, IDX_DIM)
    w_th = (wi_s * (IDX_DIM ** -0.5 * N_IDX_HEADS ** -0.5)).reshape(nb, hx, 1)
    ki_new_t = jnp.swapaxes(jnp.pad(ki_new, ((0, 0), (0, page - n_new), (0, 0))), 1, 2)
    cache_ik_t = jnp.swapaxes(cache_idx_k[0], 1, 2)
    scores = _s_score(page_table, qi_th, w_th, ki_new_t, cache_ik_t, n_new)
    topk_s = min(TOPK_MAX, (past + n_new) // 4)
    width = scores.shape[-1]
    sel_t = _s_sel(scores.reshape(nb * n_new, width).T, topk_s)
    sel4 = sel_t.T.reshape(nb, n_new, width)
    sel8 = jnp.concatenate([sel4, sel4], axis=1)

    q_ht = jnp.swapaxes(s16b[:, :, :dm].reshape(nb, n_new, n_heads, HEAD_DIM), 1, 2)
    q_ht = q_ht.reshape(nb, n_heads * n_new, HEAD_DIM)
    pr = page * N_KV_HEADS
    k_new2d = jnp.pad(k_new.astype(BF16), ((0, 0), (0, 8 - n_new), (0, 0))).reshape(nb, 8 * N_KV_HEADS, HEAD_DIM)
    v_new2d = jnp.pad(v_new.astype(BF16), ((0, 0), (0, 8 - n_new), (0, 0))).reshape(nb, 8 * N_KV_HEADS, HEAD_DIM)
    ck2d = cache_k[0].reshape(n_phys, pr, HEAD_DIM)
    cv2d = cache_v[0].reshape(n_phys, pr, HEAD_DIM)
    o_s = _s_attn(page_table, q_ht, sel8, k_new2d, v_new2d, ck2d, cv2d, n_grp, n_new)
    attn_s = jnp.swapaxes(o_s.reshape(nb, n_heads, n_new, HEAD_DIM), 1, 2).reshape(nb * n_new, dm)
    y_sample = _back(xs, s16, d_s, attn_s, mods_s, lw, nb * n_new, 0).reshape(nb, n_new, dm)

    k_sample = k_new.reshape(1, nb, n_new, N_KV_HEADS, HEAD_DIM)
    v_sample = v_new.reshape(1, nb, n_new, N_KV_HEADS, HEAD_DIM)
    return (y_prompt, y_sample, k_prompt, v_prompt, ki_p[None], pool_prompt,
            k_sample, v_sample, ki_new[None], pool_sample)
```

```python
import functools
import math

import jax
import jax.numpy as jnp
from jax import lax
from jax.experimental import pallas as pl
from jax.experimental.pallas import tpu as pltpu

F32 = jnp.float32
BF16 = jnp.bfloat16

RMS_EPS = 1e-6
POOL_WINDOWS = (2, 4, 8, 16)
POOL_BUF = max(POOL_WINDOWS) - 1
HEAD_DIM = 128
N_KV_HEADS = 4
N_IDX_HEADS = 16
IDX_DIM = 64
TOPK_MAX = 256
Q_BLOCK = 128
LANES = 128
NEG = -1e30
Q_SCALE = HEAD_DIM ** -0.5 * math.log2(math.e)
V_ROWS = HEAD_DIM + 16
NT_DIMS = (((1,), (1,)), ((), ()))
VMEM_LIMIT = 56 * 1024 * 1024


def _cparams(n_axes):
    return pltpu.CompilerParams(dimension_semantics=("arbitrary",) * n_axes, vmem_limit_bytes=VMEM_LIMIT)


def _rms(x, g):
    return x * lax.rsqrt(jnp.mean(x * x, axis=-1, keepdims=True) + RMS_EPS) * g


def _ada_kernel(c_ref, w_ref, b_ref, o_ref):
    c = c_ref[...]
    s = (c * jax.nn.sigmoid(c)).astype(BF16)
    o_ref[...] = jnp.dot(s, w_ref[...].astype(BF16), preferred_element_type=F32) + b_ref[...]


def _ada(c_all, w_ada, b_ada):
    rows, d = c_all.shape
    n = w_ada.shape[1]
    tn = 512
    return pl.pallas_call(
        _ada_kernel,
        out_shape=jax.ShapeDtypeStruct((rows, n), F32),
        grid=(n // tn,),
        in_specs=[pl.BlockSpec((rows, d), lambda j: (0, 0)),
                  pl.BlockSpec((d, tn), lambda j: (0, j)),
                  pl.BlockSpec((1, tn), lambda j: (0, j))],
        out_specs=pl.BlockSpec((rows, tn), lambda j: (0, j)),
        compiler_params=_cparams(1),
        name="ada",
    )(c_all, w_ada, b_ada.reshape(1, n))


FRONT_TN = 512
N_FRONT_TILES = 19
N_Q_TILES = 4
F32_TILE0 = 14
K_TILE, V_TILE = 16, 17
W16 = 16 * FRONT_TN
W32 = 5 * FRONT_TN


def _front_kernel(x_ref, g_ref, sh_ref, sc_ref, wt_ref, o16_ref, o32_ref, u_ref):
    n = pl.program_id(1)

    @pl.when(n == 0)
    def _():
        u_ref[...] = (_rms(x_ref[...], g_ref[...]) * (1.0 + sc_ref[...]) + sh_ref[...]).astype(BF16)

    r = lax.dot_general(u_ref[...], wt_ref[...].astype(BF16), NT_DIMS, preferred_element_type=F32)

    @pl.when((n < F32_TILE0) | (n == K_TILE) | (n == V_TILE))
    def _():
        o16_ref[...] = (r * jnp.where(n < N_Q_TILES, Q_SCALE, 1.0)).astype(BF16)

    @pl.when(n >= F32_TILE0)
    def _():
        o32_ref[...] = r


def _lookup(n, table):
    out = jnp.int32(table[-1])
    for i, v in enumerate(table[:-1]):
        out = jnp.where(n == i, v, out)
    return out


def _o16_block(n):
    return jnp.where(n < F32_TILE0, n,
                     jnp.where(n < K_TILE, F32_TILE0 - 1, jnp.where(n <= V_TILE, n - 2, V_TILE - 2)))


def _mod_spec(mod, tm, rows_per_mod):
    d = mod.shape[-1]
    if mod.ndim == 3:
        tiles_per_batch = rows_per_mod // tm
        return pl.BlockSpec((None, 1, d), lambda i, *_: (i // tiles_per_batch, 0, 0))
    return pl.BlockSpec((tm, d), lambda i, *_: (i, 0))


def _front(x, g1, shift, scale, w_t, tile_rows, tm, rows_per_mod):
    m, d = x.shape
    return pl.pallas_call(
        _front_kernel,
        out_shape=(jax.ShapeDtypeStruct((m, W16), BF16), jax.ShapeDtypeStruct((m, W32), F32)),
        grid=(m // tm, N_FRONT_TILES),
        in_specs=[pl.BlockSpec((tm, d), lambda i, n: (i, 0)),
                  pl.BlockSpec((1, d), lambda i, n: (0, 0)),
                  _mod_spec(shift, tm, rows_per_mod),
                  _mod_spec(scale, tm, rows_per_mod),
                  pl.BlockSpec((pl.Element(FRONT_TN), pl.Element(d)),
                               lambda i, n: (_lookup(n, [r // 8 for r in tile_rows]) * 8, 0))],
        out_specs=(pl.BlockSpec((tm, FRONT_TN), lambda i, n: (i, _o16_block(n))),
                   pl.BlockSpec((tm, FRONT_TN), lambda i, n: (i, jnp.maximum(n - F32_TILE0, 0)))),
        scratch_shapes=[pltpu.VMEM((tm, d), BF16)],
        compiler_params=_cparams(2),
        name="front",
    )(x, g1, shift, scale, w_t)


def _pool_d_prompt_kernel(pin_ref, halo_ref, o_ref, buf_ref, *, tp, group):
    i = pl.program_id(1)
    cur = pin_ref[...]
    halo = jnp.where(i > 0, halo_ref[...], 0.0)
    hb = halo.shape[0]
    buf_ref[0:hb, :] = halo
    buf_ref[hb:hb + tp, :] = cur
    pos = i * tp + lax.broadcasted_iota(jnp.int32, (tp, 1), 0)
    for gi, w in enumerate(POOL_WINDOWS):
        sl = slice(gi * group, (gi + 1) * group)
        acc = cur[:, sl]
        for j in range(1, w):
            acc = acc + buf_ref[hb - j:hb - j + tp, sl]
        cnt = jnp.minimum(pos + 1, w).astype(F32)
        o_ref[:, sl] = (acc / cnt - cur[:, sl]).astype(BF16)


def _pool_d_prompt(p32, batch, seq, width):
    tp, hb = 512, 16
    group = width // len(POOL_WINDOWS)
    p32b = p32.reshape(batch, seq, p32.shape[-1])
    return pl.pallas_call(
        functools.partial(_pool_d_prompt_kernel, tp=tp, group=group),
        out_shape=jax.ShapeDtypeStruct((batch, seq, width), BF16),
        grid=(batch, seq // tp),
        in_specs=[pl.BlockSpec((None, tp, width), lambda b, i: (b, i, 0)),
                  pl.BlockSpec((None, hb, width), lambda b, i: (b, jnp.maximum(i * (tp // hb) - 1, 0), 0))],
        out_specs=pl.BlockSpec((None, tp, width), lambda b, i: (b, i, 0)),
        scratch_shapes=[pltpu.VMEM((tp + hb, width), F32)],
        compiler_params=_cparams(2),
        name="pool_d_prompt",
    )(p32b, p32b)


def _pool_d_sample_kernel(seq_ref, o_ref, *, n_new, group):
    for t in range(n_new):
        r = POOL_BUF + t
        for gi, w in enumerate(POOL_WINDOWS):
            sl = slice(gi * group, (gi + 1) * group)
            acc = seq_ref[r, :, sl]
            for j in range(1, w):
                acc = acc + seq_ref[r - j, :, sl]
            o_ref[t, :, sl] = (acc / float(w) - seq_ref[r, :, sl]).astype(BF16)


def _pool_d_sample(seq_t, n_new):
    rows, nb, width = seq_t.shape
    group = width // len(POOL_WINDOWS)
    return pl.pallas_call(
        functools.partial(_pool_d_sample_kernel, n_new=n_new, group=group),
        out_shape=jax.ShapeDtypeStruct((n_new, nb, width), BF16),
        grid=(1,),
        in_specs=[pl.BlockSpec((rows, nb, width), lambda i: (0, 0, 0))],
        out_specs=pl.BlockSpec((n_new, nb, width), lambda i: (0, 0, 0)),
        compiler_params=_cparams(1),
        name="pool_d_sample",
    )(seq_t)


def _key_to_float(key):
    bits = key ^ ((key >> 31) & jnp.int32(0x7FFFFFFF))
    return lax.bitcast_convert_type(bits, F32)


def _count_cols(sc_ref, n_chunks, chunk, flag):
    n_acc = 4
    lanes = sc_ref.shape[1]

    def chunk_body(c, cnts):
        c0 = pl.multiple_of(c * chunk, chunk)
        s = sc_ref[pl.ds(c0, chunk), :]
        cnts = list(cnts)
        for r in range(chunk // 8):
            cnts[r % n_acc] = cnts[r % n_acc] + flag(s[r * 8:(r + 1) * 8], c0 + r * 8)
        return tuple(cnts)

    cnts = lax.fori_loop(0, n_chunks, chunk_body, tuple(jnp.zeros((8, lanes), jnp.int32) for _ in range(n_acc)))
    return jnp.sum(functools.reduce(jnp.add, cnts), axis=0, keepdims=True)


def _kth_largest_cols(sc_ref, n_chunks, chunk, k):
    int_min = jnp.int32(-(2 ** 31))
    lanes = sc_ref.shape[1]

    def bit_body(it, key):
        cand = key ^ lax.shift_left(jnp.int32(1), 31 - it)
        cf = jnp.broadcast_to(_key_to_float(cand), (8, lanes))
        tot = _count_cols(sc_ref, n_chunks, chunk, lambda s, r0: jnp.where(s >= cf, 1, 0))
        return jnp.where(tot >= k, cand, key)

    key = lax.fori_loop(0, 32, bit_body, jnp.full((1, lanes), int_min, jnp.int32))
    return _key_to_float(key)


def _break_ties_cols(sc_ref, n_chunks, chunk, k, thr, n_rows):
    lanes = sc_ref.shape[1]
    thr8 = jnp.broadcast_to(thr, (8, lanes))
    n_ge = _count_cols(sc_ref, n_chunks, chunk, lambda s, r0: jnp.where(s >= thr8, 1, 0))

    @pl.when(jnp.max(n_ge) > k)
    def _():
        need = k - _count_cols(sc_ref, n_chunks, chunk, lambda s, r0: jnp.where(s > thr8, 1, 0))
        row8 = lax.broadcasted_iota(jnp.int32, (8, lanes), 0)
        n_bits = (n_rows - 1).bit_length()

        def bit_body(it, last):
            cand = last | lax.shift_left(jnp.int32(1), n_bits - 1 - it)
            cand8 = jnp.broadcast_to(cand, (8, lanes))
            before = _count_cols(sc_ref, n_chunks, chunk,
                                 lambda s, r0: jnp.where(s == thr8, jnp.where(row8 + r0 < cand8, 1, 0), 0))
            return jnp.where(before < need, cand, last)

        last = lax.fori_loop(0, n_bits, bit_body, jnp.zeros((1, lanes), jnp.int32))

        def drop(c, carry):
            c0 = pl.multiple_of(c * chunk, chunk)
            s = sc_ref[pl.ds(c0, chunk), :]
            row = c0 + lax.broadcasted_iota(jnp.int32, (chunk, lanes), 0)
            sc_ref[pl.ds(c0, chunk), :] = jnp.where(s == thr, jnp.where(row > last, -jnp.inf, s), s)
            return carry

        lax.fori_loop(0, n_chunks, drop, 0)


PA_CHUNK = 512
PA_QUERIES = 256
PA_QSUB = LANES


def _p_attn_kernel(q_ref, qi_ref, wit_ref, kia_ref, kib_ref, k_ref, vt_ref, o_ref, sc_ref, *, topk):
    nq = PA_QUERIES
    blk = pl.program_id(1)
    t0 = blk * nq
    n_ch = (t0 + nq + PA_CHUNK - 1) // PA_CHUNK
    tok = t0 + lax.broadcasted_iota(jnp.int32, (1, nq), 1)
    n_grp = (q_ref.shape[1] // HEAD_DIM) // N_KV_HEADS
    n_pairs = N_IDX_HEADS // 2

    wit = wit_ref[...] * (IDX_DIM ** -0.5 * N_IDX_HEADS ** -0.5)

    def score_chunk(c, carry):
        c0 = pl.multiple_of(c * PA_CHUNK, PA_CHUNK)
        ka = kia_ref[pl.ds(c0, PA_CHUNK), :]
        kb = kib_ref[pl.ds(c0, PA_CHUNK), :]
        acc = jnp.zeros((PA_CHUNK, nq), F32)
        for pp in range(n_pairs // 2):
            qp = jnp.concatenate([qi_ref[:, (2 * pp + i) * LANES:(2 * pp + i + 1) * LANES] for i in range(2)], axis=0)
            da = lax.dot_general(ka, qp, NT_DIMS, preferred_element_type=F32)
            db = lax.dot_general(kb, qp, NT_DIMS, preferred_element_type=F32)
            for i in range(2):
                h = 2 * (2 * pp + i)
                sl = slice(i * nq, (i + 1) * nq)
                acc = acc + jnp.maximum(da[:, sl], 0.0) * wit[h:h + 1, :]
                acc = acc + jnp.maximum(db[:, sl], 0.0) * wit[h + 1:h + 2, :]
        keypos = c0 + lax.broadcasted_iota(jnp.int32, (PA_CHUNK, nq), 0)
        sc_ref[pl.ds(c0, PA_CHUNK), :] = jnp.where(keypos <= tok, acc, -jnp.inf)
        return carry

    lax.fori_loop(0, n_ch, score_chunk, 0)

    thr = _kth_largest_cols(sc_ref, n_ch, PA_CHUNK, topk)
    thr = jnp.where(tok + 1 <= topk, jnp.finfo(F32).min, thr)
    _break_ties_cols(sc_ref, n_ch, PA_CHUNK, topk, thr, sc_ref.shape[0])

    items = [(j, u) for j in range(N_KV_HEADS) for u in range(nq // PA_QSUB)]
    width = n_grp * PA_QSUB

    def attn_chunk(c, carry):
        c0 = pl.multiple_of(c * PA_CHUNK, PA_CHUNK)
        biases = []
        for u in range(nq // PA_QSUB):
            qs = slice(u * PA_QSUB, (u + 1) * PA_QSUB)
            bias = jnp.where(sc_ref[pl.ds(c0, PA_CHUNK), qs] >= thr[:, qs], 0.0, NEG)
            biases.append(jnp.concatenate([bias] * n_grp, axis=1))

        def logits(i):
            j, u = items[i]
            qj = jnp.concatenate([q_ref[u * PA_QSUB:(u + 1) * PA_QSUB,
                                        (j * n_grp + g) * HEAD_DIM:(j * n_grp + g + 1) * HEAD_DIM]
                                  for g in range(n_grp)], axis=0)
            kc = k_ref[pl.ds(c0, PA_CHUNK), j * HEAD_DIM:(j + 1) * HEAD_DIM]
            return lax.dot_general(kc, qj, NT_DIMS, preferred_element_type=F32) + biases[u]

        def weights(i, s):
            m = carry[i][0]
            m_new = jnp.maximum(m, jnp.max(s, axis=0, keepdims=True))
            return m_new, jnp.exp2(m - m_new), jnp.exp2(s - m_new).astype(BF16)

        def accumulate(i, alpha, p):
            j = items[i][0]
            vt = vt_ref[j * V_ROWS:(j + 1) * V_ROWS, pl.ds(c0, PA_CHUNK)]
            return alpha * carry[i][1] + jnp.dot(vt, p, preferred_element_type=F32)

        s, w, out = {}, {}, []
        for step in range(len(items) + 2):
            if step < len(items):
                s[step] = logits(step)
            if 0 <= step - 1 < len(items):
                w[step - 1] = weights(step - 1, s.pop(step - 1))
            if 0 <= step - 2 < len(items):
                m_new, alpha, p = w.pop(step - 2)
                out.append((m_new, accumulate(step - 2, alpha, p)))
        return tuple(out)

    init = tuple((jnp.full((1, width), NEG, F32), jnp.zeros((V_ROWS, width), F32)) for _ in items)
    final = lax.fori_loop(0, n_ch, attn_chunk, init)
    for (j, u), (_, acc) in zip(items, final):
        o_t = acc[:HEAD_DIM] * (1.0 / acc[HEAD_DIM:HEAD_DIM + 1])
        for g in range(n_grp):
            h = j * n_grp + g
            o_ref[u * PA_QSUB:(u + 1) * PA_QSUB, h * HEAD_DIM:(h + 1) * HEAD_DIM] = (
                o_t[:, g * PA_QSUB:(g + 1) * PA_QSUB].T.astype(BF16))


def _p_attn(p16, wit, kia, kib, vt, batch, seq, topk):
    d_attn = 2048
    p16b = p16.reshape(batch, seq, W16)
    kv_w = N_KV_HEADS * HEAD_DIM
    return pl.pallas_call(
        functools.partial(_p_attn_kernel, topk=topk),
        out_shape=jax.ShapeDtypeStruct((batch, seq, d_attn), BF16),
        grid=(batch, seq // PA_QUERIES),
        in_specs=[pl.BlockSpec((None, PA_QUERIES, d_attn), lambda b, i: (b, i, 0)),
                  pl.BlockSpec((None, PA_QUERIES, 1024), lambda b, i: (b, i, 6)),
                  pl.BlockSpec((None, N_IDX_HEADS, PA_QUERIES), lambda b, i: (b, 0, i)),
                  pl.BlockSpec((None, seq, LANES), lambda b, i: (b, 0, 0)),
                  pl.BlockSpec((None, seq, LANES), lambda b, i: (b, 0, 0)),
                  pl.BlockSpec((None, seq, kv_w), lambda b, i: (b, 0, 14)),
                  pl.BlockSpec((None, N_KV_HEADS * V_ROWS, seq), lambda b, i: (b, 0, 0))],
        out_specs=pl.BlockSpec((None, PA_QUERIES, d_attn), lambda b, i: (b, i, 0)),
        scratch_shapes=[pltpu.VMEM((seq, PA_QUERIES), F32)],
        compiler_params=_cparams(2),
        name="p_attn",
    )(p16b, p16b, wit, kia, kib, p16b, vt)


S_SCORE_GROUP = 4


def _s_score_kernel(pt_ref, qi_ref, w_ref, kin_ref, *rest, n_pages, page, n_new, group):
    del pt_ref
    ik_refs, o_ref = rest[:group * n_pages], rest[group * n_pages]
    for g in range(group):
        qi = qi_ref[g]
        w = w_ref[g]

        def piece(ik_t, qi=qi, w=w):
            d = jnp.dot(qi, ik_t.astype(BF16), preferred_element_type=F32)
            x = jnp.maximum(d, 0.0) * w
            return jnp.concatenate([jnp.sum(x[t * N_IDX_HEADS:(t + 1) * N_IDX_HEADS], axis=0, keepdims=True)
                                    for t in range(n_new)], axis=0)

        for p in range(n_pages):
            o_ref[g, :, p * page:(p + 1) * page] = piece(ik_refs[g * n_pages + p][...])
        new = piece(kin_ref[g])
        col = lax.broadcasted_iota(jnp.int32, new.shape, 1)
        tok = lax.broadcasted_iota(jnp.int32, new.shape, 0)
        o_ref[g, :, n_pages * page:(n_pages + 1) * page] = jnp.where(col <= tok, new, -jnp.inf)


def _page_specs(n_pages, shape, group=1):
    zeros = (0,) * len(shape)
    return [pl.BlockSpec((None,) + shape,
                         functools.partial(lambda b, pt, g, p: (pt[b * group + g, p],) + zeros, g=g, p=p))
            for g in range(group) for p in range(n_pages)]


def _s_score(page_table, qi_th, w_th, ki_new_t, cache_ik_t, n_new):
    nb, n_pages = page_table.shape
    page = cache_ik_t.shape[2]
    width = (n_pages + 1) * page
    hx = qi_th.shape[1]
    group = S_SCORE_GROUP
    assert nb % group == 0
    return pl.pallas_call(
        functools.partial(_s_score_kernel, n_pages=n_pages, page=page, n_new=n_new, group=group),
        out_shape=jax.ShapeDtypeStruct((nb, n_new, width), F32),
        grid_spec=pltpu.PrefetchScalarGridSpec(
            num_scalar_prefetch=1,
            grid=(nb // group,),
            in_specs=[pl.BlockSpec((group, hx, IDX_DIM), lambda b, pt: (b, 0, 0)),
                      pl.BlockSpec((group, hx, 1), lambda b, pt: (b, 0, 0)),
                      pl.BlockSpec((group, IDX_DIM, page), lambda b, pt: (b, 0, 0))]
            + _page_specs(n_pages, (IDX_DIM, page), group),
            out_specs=pl.BlockSpec((group, n_new, width), lambda b, pt: (b, 0, 0))),
        compiler_params=_cparams(1),
        name="s_score",
    )(page_table, qi_th, w_th, ki_new_t, *([cache_ik_t] * (group * n_pages)))


def _s_sel_kernel(sc_ref, o_ref, buf_ref, *, topk, chunk):
    n_rows = sc_ref.shape[0]
    buf_ref[...] = sc_ref[...]
    thr = _kth_largest_cols(buf_ref, n_rows // chunk, chunk, topk)
    _break_ties_cols(buf_ref, n_rows // chunk, chunk, topk, thr, n_rows)
    o_ref[...] = jnp.where(buf_ref[...] >= thr, 1.0, 0.0)


def _s_sel(scores_t, topk):
    n_rows, n_q = scores_t.shape
    return pl.pallas_call(
        functools.partial(_s_sel_kernel, topk=topk, chunk=LANES),
        out_shape=jax.ShapeDtypeStruct((n_rows, n_q), F32),
        grid=(n_q // LANES,),
        in_specs=[pl.BlockSpec((n_rows, LANES), lambda i: (0, i))],
        out_specs=pl.BlockSpec((n_rows, LANES), lambda i: (0, i)),
        scratch_shapes=[pltpu.VMEM((n_rows, LANES), F32)],
        compiler_params=_cparams(1),
        name="s_sel",
    )(scores_t)


def _s_attn_kernel(pt_ref, q_ref, sel_ref, kn_ref, vn_ref, *rest, n_pages, n_grp, n_new):
    del pt_ref
    k_refs, v_refs, o_ref = rest[:n_pages], rest[n_pages:2 * n_pages], rest[2 * n_pages]
    q = q_ref[...]
    rows = q.shape[0]
    pr = k_refs[0].shape[0]
    page = pr // N_KV_HEADS
    sel = sel_ref[...].astype(BF16)
    sel = jnp.concatenate([sel[:, p * page:(p + 1) * page] for p in range(n_pages + 1)], axis=0)
    spread = (lax.broadcasted_iota(jnp.int32, (page, pr), 1) // N_KV_HEADS
              == lax.broadcasted_iota(jnp.int32, (page, pr), 0)).astype(BF16)
    sel = jnp.dot(sel, spread, preferred_element_type=F32)
    lane_kv = lax.broadcasted_iota(jnp.int32, (rows, pr), 1) % N_KV_HEADS
    row_kv = lax.broadcasted_iota(jnp.int32, (rows, pr), 0) // (n_grp * n_new)
    head_bias = jnp.where(lane_kv == row_kv, 0.0, NEG)
    pieces = []
    for p in range(n_pages + 1):
        kp = kn_ref[...] if p == n_pages else k_refs[p][...].astype(BF16)
        w = kp.shape[0]
        s = lax.dot_general(q, kp, NT_DIMS, preferred_element_type=F32)
        bias = jnp.concatenate([(sel[p * 8:(p + 1) * 8, :w] - 1.0) * -NEG] * (rows // 8), axis=0)
        pieces.append(s + (bias + head_bias[:, :w]))
    m = functools.reduce(jnp.maximum, [jnp.max(s, axis=1, keepdims=True) for s in pieces])
    l = jnp.zeros((rows, 1), F32)
    o = jnp.zeros((rows, HEAD_DIM), F32)
    for p in range(n_pages + 1):
        e = jnp.exp2(pieces[p] - m)
        l = l + jnp.sum(e, axis=1, keepdims=True)
        vp = vn_ref[...] if p == n_pages else v_refs[p][...].astype(BF16)
        o = o + jnp.dot(e.astype(BF16), vp, preferred_element_type=F32)
    o_ref[...] = (o / l).astype(BF16)


def _s_attn(page_table, q_ht, sel8, k_new2d, v_new2d, cache_k2d, cache_v2d, n_grp, n_new):
    nb, n_pages = page_table.shape
    pr = cache_k2d.shape[1]
    pr_new = k_new2d.shape[1]
    rows = q_ht.shape[1]
    width = sel8.shape[-1]
    return pl.pallas_call(
        functools.partial(_s_attn_kernel, n_pages=n_pages, n_grp=n_grp, n_new=n_new),
        out_shape=jax.ShapeDtypeStruct((nb, rows, HEAD_DIM), BF16),
        grid_spec=pltpu.PrefetchScalarGridSpec(
            num_scalar_prefetch=1,
            grid=(nb,),
            in_specs=[pl.BlockSpec((None, rows, HEAD_DIM), lambda b, pt: (b, 0, 0)),
                      pl.BlockSpec((None, 8, width), lambda b, pt: (b, 0, 0)),
                      pl.BlockSpec((None, pr_new, HEAD_DIM), lambda b, pt: (b, 0, 0)),
                      pl.BlockSpec((None, pr_new, HEAD_DIM), lambda b, pt: (b, 0, 0))]
            + _page_specs(n_pages, (pr, HEAD_DIM)) + _page_specs(n_pages, (pr, HEAD_DIM)),
            out_specs=pl.BlockSpec((None, rows, HEAD_DIM), lambda b, pt: (b, 0, 0))),
        compiler_params=_cparams(1),
        name="s_attn",
    )(page_table, q_ht, sel8, k_new2d, v_new2d, *([cache_k2d] * n_pages), *([cache_v2d] * n_pages))


def _mix_kernel(d_ref, a_ref, ga_ref, gb_ref, wg_ref, ps_ref, wup_ref, wua_ref, o_ref):
    n_groups, group = wg_ref.shape[0], wg_ref.shape[1]
    y = jnp.concatenate(
        [jnp.dot(d_ref[:, g * group:(g + 1) * group], wg_ref[g], preferred_element_type=F32)
         for g in range(n_groups)], axis=1) * ps_ref[...]
    yp = jnp.dot(y.astype(BF16), wup_ref[...], preferred_element_type=F32)
    ya = jnp.dot(a_ref[...], wua_ref[...], preferred_element_type=F32)
    o_ref[...] = (jax.nn.sigmoid(ga_ref[...].astype(F32)) * yp
                  + jax.nn.sigmoid(gb_ref[...].astype(F32)) * ya).astype(BF16)


def _mix(d, attn, p16, w_grp, pool_scale, w_up_pool, w_up_attn, tm):
    m, dm = attn.shape
    pw = d.shape[1]
    const = lambda *shape: pl.BlockSpec(shape, lambda i: (0,) * len(shape))
    return pl.pallas_call(
        _mix_kernel,
        out_shape=jax.ShapeDtypeStruct((m, dm), BF16),
        grid=(m // tm,),
        in_specs=[pl.BlockSpec((tm, pw), lambda i: (i, 0)),
                  pl.BlockSpec((tm, dm), lambda i: (i, 0)),
                  pl.BlockSpec((tm, dm), lambda i: (i, 1)),
                  pl.BlockSpec((tm, dm), lambda i: (i, 2)),
                  const(*w_grp.shape), const(1, pw), const(*w_up_pool.shape), const(*w_up_attn.shape)],
        out_specs=pl.BlockSpec((tm, dm), lambda i: (i, 0)),
        compiler_params=_cparams(1),
        name="mix",
    )(d, attn, p16, p16, w_grp, pool_scale, w_up_pool, w_up_attn)


def _out_kernel(mix_ref, x_ref, gate_ref, sh_ref, sc_ref, g_ref, w_ref, h_ref, hn_ref):
    h = x_ref[...] + gate_ref[...] * jnp.dot(mix_ref[...], w_ref[...], preferred_element_type=F32)
    h_ref[...] = h
    hn_ref[...] = (_rms(h, g_ref[...]) * (1.0 + sc_ref[...]) + sh_ref[...]).astype(BF16)


def _out(mix, x, gate, shift, scale, g2, w_out, tm, rows_per_mod):
    m, d = x.shape
    return pl.pallas_call(
        _out_kernel,
        out_shape=(jax.ShapeDtypeStruct((m, d), F32), jax.ShapeDtypeStruct((m, d), BF16)),
        grid=(m // tm,),
        in_specs=[pl.BlockSpec((tm, d), lambda i: (i, 0)),
                  pl.BlockSpec((tm, d), lambda i: (i, 0)),
                  _mod_spec(gate, tm, rows_per_mod),
                  _mod_spec(shift, tm, rows_per_mod),
                  _mod_spec(scale, tm, rows_per_mod),
                  pl.BlockSpec((1, d), lambda i: (0, 0)),
                  pl.BlockSpec((d, d), lambda i: (0, 0))],
        out_specs=(pl.BlockSpec((tm, d), lambda i: (i, 0)), pl.BlockSpec((tm, d), lambda i: (i, 0))),
        compiler_params=_cparams(1),
        name="out",
    )(mix, x, gate, shift, scale, g2, w_out)


FFN_TF = 512


def _ffn_kernel(hn_ref, h_ref, gate_ref, gf_ref, wg_ref, wu_ref, wo_ref, o_ref, acc_ref):
    f = pl.program_id(1)

    @pl.when(f == 0)
    def _():
        acc_ref[...] = jnp.zeros_like(acc_ref)

    hn = hn_ref[...]
    a = jnp.dot(hn, wg_ref[...], preferred_element_type=F32)
    u = jnp.dot(hn, wu_ref[...], preferred_element_type=F32)
    z = (a * jax.nn.sigmoid(a) * u).astype(BF16)
    acc_ref[...] += jnp.dot(z, wo_ref[...], preferred_element_type=F32)

    @pl.when(f == pl.num_programs(1) - 1)
    def _():
        o_ref[...] = _rms(h_ref[...] + gate_ref[...] * acc_ref[...], gf_ref[...])


def _ffn(hn, h, gate, g_final, w_ffn_in, w_ffn_out, tm, rows_per_mod):
    m, d = h.shape
    d_ff = w_ffn_out.shape[0]
    n_f = d_ff // FFN_TF
    return pl.pallas_call(
        _ffn_kernel,
        out_shape=jax.ShapeDtypeStruct((m, d), F32),
        grid=(m // tm, n_f),
        in_specs=[pl.BlockSpec((tm, d), lambda i, f: (i, 0)),
                  pl.BlockSpec((tm, d), lambda i, f: (i, 0)),
                  _mod_spec(gate, tm, rows_per_mod),
                  pl.BlockSpec((1, d), lambda i, f: (0, 0)),
                  pl.BlockSpec((d, FFN_TF), lambda i, f: (0, f)),
                  pl.BlockSpec((d, FFN_TF), lambda i, f: (0, f + n_f)),
                  pl.BlockSpec((FFN_TF, d), lambda i, f: (f, 0))],
        out_specs=pl.BlockSpec((tm, d), lambda i, f: (i, 0)),
        scratch_shapes=[pltpu.VMEM((tm, d), F32)],
        compiler_params=_cparams(2),
        name="ffn",
    )(hn, h, gate, g_final, w_ffn_in, w_ffn_in, w_ffn_out)


def _back(x2, p16, d, attn, mods, lw, tm, rows_per_mod):
    mix = _mix(d, attn, p16, lw["w_grp"], lw["pool_scale"], lw["w_up_pool"], lw["w_up_attn"], tm)
    h, hn = _out(mix, x2, mods[2], mods[3], mods[4], lw["g2"], lw["w_out"], tm, rows_per_mod)
    return _ffn(hn, h, mods[5], lw["g_final"], lw["w_ffn_in"], lw["w_ffn_out"], tm, rows_per_mod)


def kernel(x_prompt, x_sample, cache_k, cache_v, cache_idx_k, state_pool, page_table, c_prompt, c_sample,
           w_ada, b_ada, g_norm1, w_in, w_pool_grp, pool_scale, w_up_pool, w_up_attn, w_out, g_norm2,
           w_ffn_in, w_ffn_out, g_final):
    batch, seq, dm = x_prompt.shape
    nb, n_new, _ = x_sample.shape
    depth = w_ada.shape[0]
    assert depth == 1, "single-layer step"
    n_phys, page = cache_k.shape[1], cache_k.shape[2]
    n_pages = page_table.shape[1]
    past = n_pages * page
    pool_w = state_pool.shape[-1]
    kv_w = N_KV_HEADS * HEAD_DIM
    n_heads = dm // HEAD_DIM
    n_grp = n_heads // N_KV_HEADS
    idx_w = N_IDX_HEADS * IDX_DIM
    assert page == LANES and n_new == 4 and seq % PA_CHUNK == 0

    w_t = jnp.swapaxes(w_in[0], 0, 1)
    assert (pool_w, dm, kv_w, idx_w) == (2 * FRONT_TN, 4 * FRONT_TN, FRONT_TN, 2 * FRONT_TN)
    o_q, o_k, o_v, o_qi = pool_w, pool_w + dm, pool_w + dm + kv_w, pool_w + dm + 2 * kv_w
    o_ki = o_qi + idx_w
    o_ga = o_ki + IDX_DIM + N_IDX_HEADS
    o_gb = o_ga + dm
    assert o_gb + dm == w_t.shape[0] and o_ki + FRONT_TN <= w_t.shape[0]
    tile_rows = ([o_q + i * FRONT_TN for i in range(4)] + [o_ga + i * FRONT_TN for i in range(4)]
                 + [o_gb + i * FRONT_TN for i in range(4)] + [o_qi, o_qi + FRONT_TN, 0, FRONT_TN, o_k, o_v, o_ki])
    assert len(tile_rows) == N_FRONT_TILES and all(r % 8 == 0 for r in tile_rows)
    lw = {"w_grp": w_pool_grp[0].astype(BF16), "pool_scale": pool_scale[0].reshape(1, pool_w),
          "w_up_pool": w_up_pool[0].astype(BF16), "w_up_attn": w_up_attn[0].astype(BF16),
          "w_out": w_out[0].astype(BF16), "g2": g_norm2[0].reshape(1, dm), "g_final": g_final.reshape(1, dm),
          "w_ffn_in": w_ffn_in[0].astype(BF16), "w_ffn_out": w_ffn_out[0].astype(BF16)}
    g1 = g_norm1[0].reshape(1, dm)

    n_c = batch + nb
    c_all = jnp.concatenate([c_prompt, c_sample, jnp.zeros((-n_c % 8, dm), F32)], axis=0)
    mod = _ada(c_all, w_ada[0], b_ada[0])
    mods_p = [mod[:batch, i * dm:(i + 1) * dm].reshape(batch, 1, dm) for i in range(6)]
    mods_s = [jnp.repeat(mod[batch:n_c, i * dm:(i + 1) * dm], n_new, axis=0) for i in range(6)]

    xp = x_prompt.reshape(batch * seq, dm)
    p16, p32 = _front(xp, g1, mods_p[0], mods_p[1], w_t, tile_rows, 1024, seq)
    p32b = p32.reshape(batch, seq, W32)
    k_prompt = p32b[:, :, pool_w:pool_w + kv_w].reshape(1, batch, seq, N_KV_HEADS, HEAD_DIM)
    v_prompt = p32b[:, :, pool_w + kv_w:pool_w + 2 * kv_w].reshape(1, batch, seq, N_KV_HEADS, HEAD_DIM)
    o_small = pool_w + 2 * kv_w
    ki_p = p32b[:, :, o_small:o_small + IDX_DIM]
    wi_p = p32b[:, :, o_small + IDX_DIM:o_small + IDX_DIM + N_IDX_HEADS]
    pool_prompt = p32b[:, seq - POOL_BUF:, :pool_w][None]
    ki16 = ki_p.astype(BF16)
    zeros = jnp.zeros_like(ki16)
    kia = jnp.concatenate([ki16, zeros], axis=-1)
    kib = jnp.concatenate([zeros, ki16], axis=-1)
    d_p = _pool_d_prompt(p32, batch, seq, pool_w).reshape(batch * seq, pool_w)
    wit_p = jnp.swapaxes(wi_p, 1, 2)
    vt_p = jnp.swapaxes(p16.reshape(batch, seq, W16)[:, :, W16 - kv_w:], 1, 2)
    vt_p = jnp.concatenate([vt_p.reshape(batch, N_KV_HEADS, HEAD_DIM, seq),
                            jnp.ones((batch, N_KV_HEADS, V_ROWS - HEAD_DIM, seq), BF16)], axis=2)
    vt_p = vt_p.reshape(batch, N_KV_HEADS * V_ROWS, seq)
    attn_p = _p_attn(p16, wit_p, kia, kib, vt_p, batch, seq, min(TOPK_MAX, seq // 4)).reshape(batch * seq, dm)
    y_prompt = _back(xp, p16, d_p, attn_p, mods_p, lw, 512, seq).reshape(batch, seq, dm)

    xs = x_sample.reshape(nb * n_new, dm)
    s16, s32 = _front(xs, g1, mods_s[0], mods_s[1], w_t, tile_rows, nb * n_new, 0)
    s32b = s32.reshape(nb, n_new, W32)
    k_new = s32b[:, :, pool_w:pool_w + kv_w]
    v_new = s32b[:, :, pool_w + kv_w:pool_w + 2 * kv_w]
    ki_new = s32b[:, :, o_small:o_small + IDX_DIM]
    wi_s = s32b[:, :, o_small + IDX_DIM:o_small + IDX_DIM + N_IDX_HEADS]
    seq_s = jnp.concatenate([state_pool[0], s32b[:, :, :pool_w]], axis=1)
    pool_sample = seq_s[:, n_new:][None]
    d_s = _pool_d_sample(jnp.swapaxes(seq_s, 0, 1), n_new)
    d_s = jnp.swapaxes(d_s, 0, 1).reshape(nb * n_new, pool_w)

    s16b = s16.reshape(nb, n_new, W16)
    hx = N_IDX_HEADS * n_new
    qi_th = s16b[:, :, 3 * dm:3 * dm + idx_w].reshape(nb, hx, IDX_DIM)
    w_th = (wi_s * (IDX_DIM ** -0.5 * N_IDX_HEADS ** -0.5)).reshape(nb, hx, 1)
    ki_new_t = jnp.swapaxes(jnp.pad(ki_new, ((0, 0), (0, page - n_new), (0, 0))), 1, 2)
    cache_ik_t = jnp.swapaxes(cache_idx_k[0], 1, 2)
    scores = _s_score(page_table, qi_th, w_th, ki_new_t, cache_ik_t, n_new)
    topk_s = min(TOPK_MAX, (past + n_new) // 4)
    width = scores.shape[-1]
    sel_t = _s_sel(scores.reshape(nb * n_new, width).T, topk_s)
    sel4 = sel_t.T.reshape(nb, n_new, width)
    sel8 = jnp.concatenate([sel4, sel4], axis=1)

    q_ht = jnp.swapaxes(s16b[:, :, :dm].reshape(nb, n_new, n_heads, HEAD_DIM), 1, 2)
    q_ht = q_ht.reshape(nb, n_heads * n_new, HEAD_DIM)
    pr = page * N_KV_HEADS
    k_new2d = jnp.pad(k_new.astype(BF16), ((0, 0), (0, 8 - n_new), (0, 0))).reshape(nb, 8 * N_KV_HEADS, HEAD_DIM)
    v_new2d = jnp.pad(v_new.astype(BF16), ((0, 0), (0, 8 - n_new), (0, 0))).reshape(nb, 8 * N_KV_HEADS, HEAD_DIM)
    ck2d = cache_k[0].reshape(n_phys, pr, HEAD_DIM)
    cv2d = cache_v[0].reshape(n_phys, pr, HEAD_DIM)
    o_s = _s_attn(page_table, q_ht, sel8, k_new2d, v_new2d, ck2d, cv2d, n_grp, n_new)
    attn_s = jnp.swapaxes(o_s.reshape(nb, n_heads, n_new, HEAD_DIM), 1, 2).reshape(nb * n_new, dm)
    y_sample = _back(xs, s16, d_s, attn_s, mods_s, lw, nb * n_new, 0).reshape(nb, n_new, dm)

    k_sample = k_new.reshape(1, nb, n_new, N_KV_HEADS, HEAD_DIM)
    v_sample = v_new.reshape(1, nb, n_new, N_KV_HEADS, HEAD_DIM)
    return (y_prompt, y_sample, k_prompt, v_prompt, ki_p[None], pool_prompt,
            k_sample, v_sample, ki_new[None], pool_sample)
```

```python
import functools
import math

import jax
import jax.numpy as jnp
from jax import lax
from jax.experimental import pallas as pl
from jax.experimental.pallas import tpu as pltpu

F32 = jnp.float32
BF16 = jnp.bfloat16

RMS_EPS = 1e-6
POOL_WINDOWS = (2, 4, 8, 16)
POOL_BUF = max(POOL_WINDOWS) - 1
HEAD_DIM = 128
N_KV_HEADS = 4
N_IDX_HEADS = 16
IDX_DIM = 64
TOPK_MAX = 256
Q_BLOCK = 128
LANES = 128
NEG = -1e30
Q_SCALE = HEAD_DIM ** -0.5 * math.log2(math.e)
V_ROWS = HEAD_DIM + 16
NT_DIMS = (((1,), (1,)), ((), ()))
VMEM_LIMIT = 56 * 1024 * 1024
FRONT_ROWS = 1024
ROW_TILE = 512


def _cparams(n_axes):
    return pltpu.CompilerParams(dimension_semantics=("arbitrary",) * n_axes, vmem_limit_bytes=VMEM_LIMIT)


def _rms(x, g):
    return x * lax.rsqrt(jnp.mean(x * x, axis=-1, keepdims=True) + RMS_EPS) * g


def _ada_kernel(c_ref, w_ref, b_ref, o_ref):
    c = c_ref[...]
    s = (c * jax.nn.sigmoid(c)).astype(BF16)
    o_ref[...] = jnp.dot(s, w_ref[...].astype(BF16), preferred_element_type=F32) + b_ref[...]


def _ada(c_all, w_ada, b_ada):
    rows, d = c_all.shape
    n = w_ada.shape[1]
    tn = 512
    return pl.pallas_call(
        _ada_kernel,
        out_shape=jax.ShapeDtypeStruct((rows, n), F32),
        grid=(n // tn,),
        in_specs=[pl.BlockSpec((rows, d), lambda j: (0, 0)),
                  pl.BlockSpec((d, tn), lambda j: (0, j)),
                  pl.BlockSpec((1, tn), lambda j: (0, j))],
        out_specs=pl.BlockSpec((rows, tn), lambda j: (0, j)),
        compiler_params=_cparams(1),
        name="ada",
    )(c_all, w_ada, b_ada.reshape(1, n))


FRONT_TN = 512
N_FRONT_TILES = 19
N_Q_TILES = 4
F32_TILE0 = 14
K_TILE, V_TILE = 16, 17
W16 = 16 * FRONT_TN
W32 = 5 * FRONT_TN


def _front_kernel(x_ref, g_ref, sh_ref, sc_ref, wt_ref, o16_ref, o32_ref, u_ref):
    n = pl.program_id(1)

    @pl.when(n == 0)
    def _():
        u_ref[...] = (_rms(x_ref[...], g_ref[...]) * (1.0 + sc_ref[...]) + sh_ref[...]).astype(BF16)

    r = lax.dot_general(u_ref[...], wt_ref[...].astype(BF16), NT_DIMS, preferred_element_type=F32)

    @pl.when((n < F32_TILE0) | (n == K_TILE) | (n == V_TILE))
    def _():
        o16_ref[...] = (r * jnp.where(n < N_Q_TILES, Q_SCALE, 1.0)).astype(BF16)

    @pl.when(n >= F32_TILE0)
    def _():
        o32_ref[...] = r


def _lookup(n, table):
    out = jnp.int32(table[-1])
    for i, v in enumerate(table[:-1]):
        out = jnp.where(n == i, v, out)
    return out


def _o16_block(n):
    return jnp.where(n < F32_TILE0, n,
                     jnp.where(n < K_TILE, F32_TILE0 - 1, jnp.where(n <= V_TILE, n - 2, V_TILE - 2)))


def _mod_spec(mod, tm, rows_per_mod):
    d = mod.shape[-1]
    if mod.ndim == 3:
        tiles_per_batch = rows_per_mod // tm
        return pl.BlockSpec((None, 1, d), lambda i, *_: (i // tiles_per_batch, 0, 0))
    return pl.BlockSpec((tm, d), lambda i, *_: (i, 0))


def _front(x, g1, shift, scale, w_t, tile_rows, tm, rows_per_mod):
    m, d = x.shape
    return pl.pallas_call(
        _front_kernel,
        out_shape=(jax.ShapeDtypeStruct((m, W16), BF16), jax.ShapeDtypeStruct((m, W32), F32)),
        grid=(m // tm, N_FRONT_TILES),
        in_specs=[pl.BlockSpec((tm, d), lambda i, n: (i, 0)),
                  pl.BlockSpec((1, d), lambda i, n: (0, 0)),
                  _mod_spec(shift, tm, rows_per_mod),
                  _mod_spec(scale, tm, rows_per_mod),
                  pl.BlockSpec((pl.Element(FRONT_TN), pl.Element(d)),
                               lambda i, n: (_lookup(n, [r // 8 for r in tile_rows]) * 8, 0))],
        out_specs=(pl.BlockSpec((tm, FRONT_TN), lambda i, n: (i, _o16_block(n))),
                   pl.BlockSpec((tm, FRONT_TN), lambda i, n: (i, jnp.maximum(n - F32_TILE0, 0)))),
        scratch_shapes=[pltpu.VMEM((tm, d), BF16)],
        compiler_params=_cparams(2),
        name="front",
    )(x, g1, shift, scale, w_t)


POOL_HALO = 16


def _pool_d_tile(cur, halo, pos, buf_ref):
    tp, width = cur.shape
    group = width // len(POOL_WINDOWS)
    buf_ref[0:POOL_HALO, :] = halo
    buf_ref[POOL_HALO:POOL_HALO + tp, :] = cur
    out = []
    for gi, w in enumerate(POOL_WINDOWS):
        sl = slice(gi * group, (gi + 1) * group)
        acc = cur[:, sl]
        for j in range(1, w):
            acc = acc + buf_ref[POOL_HALO - j:POOL_HALO - j + tp, sl]
        cnt = jnp.minimum(pos + 1, w).astype(F32)
        out.append((acc / cnt - cur[:, sl]).astype(BF16))
    return out


def _pool_d_sample_kernel(seq_ref, o_ref, *, n_new, group):
    for t in range(n_new):
        r = POOL_BUF + t
        for gi, w in enumerate(POOL_WINDOWS):
            sl = slice(gi * group, (gi + 1) * group)
            acc = seq_ref[r, :, sl]
            for j in range(1, w):
                acc = acc + seq_ref[r - j, :, sl]
            o_ref[t, :, sl] = (acc / float(w) - seq_ref[r, :, sl]).astype(BF16)


def _pool_d_sample(seq_t, n_new):
    rows, nb, width = seq_t.shape
    group = width // len(POOL_WINDOWS)
    return pl.pallas_call(
        functools.partial(_pool_d_sample_kernel, n_new=n_new, group=group),
        out_shape=jax.ShapeDtypeStruct((n_new, nb, width), BF16),
        grid=(1,),
        in_specs=[pl.BlockSpec((rows, nb, width), lambda i: (0, 0, 0))],
        out_specs=pl.BlockSpec((n_new, nb, width), lambda i: (0, 0, 0)),
        compiler_params=_cparams(1),
        name="pool_d_sample",
    )(seq_t)


def _key_to_float(key):
    bits = key ^ ((key >> 31) & jnp.int32(0x7FFFFFFF))
    return lax.bitcast_convert_type(bits, F32)


def _count_cols(sc_ref, n_chunks, chunk, flag):
    n_acc = 4
    lanes = sc_ref.shape[1]

    def chunk_body(c, cnts):
        c0 = pl.multiple_of(c * chunk, chunk)
        s = sc_ref[pl.ds(c0, chunk), :]
        cnts = list(cnts)
        for r in range(chunk // 8):
            cnts[r % n_acc] = cnts[r % n_acc] + flag(s[r * 8:(r + 1) * 8], c0 + r * 8)
        return tuple(cnts)

    cnts = lax.fori_loop(0, n_chunks, chunk_body, tuple(jnp.zeros((8, lanes), jnp.int32) for _ in range(n_acc)))
    return jnp.sum(functools.reduce(jnp.add, cnts), axis=0, keepdims=True)


def _kth_largest_cols(sc_ref, n_chunks, chunk, k):
    int_min = jnp.int32(-(2 ** 31))
    lanes = sc_ref.shape[1]

    def bit_body(it, key):
        cand = key ^ lax.shift_left(jnp.int32(1), 31 - it)
        cf = jnp.broadcast_to(_key_to_float(cand), (8, lanes))
        tot = _count_cols(sc_ref, n_chunks, chunk, lambda s, r0: jnp.where(s >= cf, 1, 0))
        return jnp.where(tot >= k, cand, key)

    key = lax.fori_loop(0, 32, bit_body, jnp.full((1, lanes), int_min, jnp.int32))
    return _key_to_float(key)


def _break_ties_cols(sc_ref, n_chunks, chunk, k, thr, n_rows):
    lanes = sc_ref.shape[1]
    thr8 = jnp.broadcast_to(thr, (8, lanes))
    n_ge = _count_cols(sc_ref, n_chunks, chunk, lambda s, r0: jnp.where(s >= thr8, 1, 0))

    @pl.when(jnp.max(n_ge) > k)
    def _():
        need = k - _count_cols(sc_ref, n_chunks, chunk, lambda s, r0: jnp.where(s > thr8, 1, 0))
        row8 = lax.broadcasted_iota(jnp.int32, (8, lanes), 0)
        n_bits = (n_rows - 1).bit_length()

        def bit_body(it, last):
            cand = last | lax.shift_left(jnp.int32(1), n_bits - 1 - it)
            cand8 = jnp.broadcast_to(cand, (8, lanes))
            before = _count_cols(sc_ref, n_chunks, chunk,
                                 lambda s, r0: jnp.where(s == thr8, jnp.where(row8 + r0 < cand8, 1, 0), 0))
            return jnp.where(before < need, cand, last)

        last = lax.fori_loop(0, n_bits, bit_body, jnp.zeros((1, lanes), jnp.int32))

        def drop(c, carry):
            c0 = pl.multiple_of(c * chunk, chunk)
            s = sc_ref[pl.ds(c0, chunk), :]
            row = c0 + lax.broadcasted_iota(jnp.int32, (chunk, lanes), 0)
            sc_ref[pl.ds(c0, chunk), :] = jnp.where(s == thr, jnp.where(row > last, -jnp.inf, s), s)
            return carry

        lax.fori_loop(0, n_chunks, drop, 0)


PA_CHUNK = 512
PA_QUERIES = 512
PA_QSUB = LANES


def _p_attn_kernel(q_ref, qi_ref, wit_ref, kia_ref, kib_ref, k_ref, vt_ref, o_ref, sc_ref, *, topk):
    nq = PA_QUERIES
    blk = pl.program_id(1)
    t0 = blk * nq
    n_ch = (t0 + nq + PA_CHUNK - 1) // PA_CHUNK
    tok = t0 + lax.broadcasted_iota(jnp.int32, (1, nq), 1)
    n_grp = (q_ref.shape[1] // HEAD_DIM) // N_KV_HEADS
    n_pairs = N_IDX_HEADS // 2

    wit = wit_ref[...] * (IDX_DIM ** -0.5 * N_IDX_HEADS ** -0.5)

    def score_chunk(c, carry):
        c0 = pl.multiple_of(c * PA_CHUNK, PA_CHUNK)
        ka = kia_ref[pl.ds(c0, PA_CHUNK), :]
        kb = kib_ref[pl.ds(c0, PA_CHUNK), :]
        acc = jnp.zeros((PA_CHUNK, nq), F32)
        for pp in range(n_pairs // 2):
            qp = jnp.concatenate([qi_ref[:, (2 * pp + i) * LANES:(2 * pp + i + 1) * LANES] for i in range(2)], axis=0)
            da = lax.dot_general(ka, qp, NT_DIMS, preferred_element_type=F32)
            db = lax.dot_general(kb, qp, NT_DIMS, preferred_element_type=F32)
            for i in range(2):
                h = 2 * (2 * pp + i)
                sl = slice(i * nq, (i + 1) * nq)
                acc = acc + jnp.maximum(da[:, sl], 0.0) * wit[h:h + 1, :]
                acc = acc + jnp.maximum(db[:, sl], 0.0) * wit[h + 1:h + 2, :]
        keypos = c0 + lax.broadcasted_iota(jnp.int32, (PA_CHUNK, nq), 0)
        sc_ref[pl.ds(c0, PA_CHUNK), :] = jnp.where(keypos <= tok, acc, -jnp.inf)
        return carry

    lax.fori_loop(0, n_ch, score_chunk, 0)

    thr = _kth_largest_cols(sc_ref, n_ch, PA_CHUNK, topk)
    thr = jnp.where(tok + 1 <= topk, jnp.finfo(F32).min, thr)
    _break_ties_cols(sc_ref, n_ch, PA_CHUNK, topk, thr, sc_ref.shape[0])

    items = [(j, u) for j in range(N_KV_HEADS) for u in range(nq // PA_QSUB)]
    width = n_grp * PA_QSUB

    def attn_chunk(c, carry):
        c0 = pl.multiple_of(c * PA_CHUNK, PA_CHUNK)
        biases = []
        for u in range(nq // PA_QSUB):
            qs = slice(u * PA_QSUB, (u + 1) * PA_QSUB)
            bias = jnp.where(sc_ref[pl.ds(c0, PA_CHUNK), qs] >= thr[:, qs], 0.0, NEG)
            biases.append(jnp.concatenate([bias] * n_grp, axis=1))

        def logits(i):
            j, u = items[i]
            qj = jnp.concatenate([q_ref[u * PA_QSUB:(u + 1) * PA_QSUB,
                                        (j * n_grp + g) * HEAD_DIM:(j * n_grp + g + 1) * HEAD_DIM]
                                  for g in range(n_grp)], axis=0)
            kc = k_ref[pl.ds(c0, PA_CHUNK), j * HEAD_DIM:(j + 1) * HEAD_DIM]
            return lax.dot_general(kc, qj, NT_DIMS, preferred_element_type=F32) + biases[u]

        def weights(i, s):
            m = carry[i][0]
            m_new = jnp.maximum(m, jnp.max(s, axis=0, keepdims=True))
            return m_new, jnp.exp2(m - m_new), jnp.exp2(s - m_new).astype(BF16)

        def accumulate(i, alpha, p):
            j = items[i][0]
            vt = vt_ref[j * V_ROWS:(j + 1) * V_ROWS, pl.ds(c0, PA_CHUNK)]
            return alpha * carry[i][1] + jnp.dot(vt, p, preferred_element_type=F32)

        s, w, out = {}, {}, []
        for step in range(len(items) + 2):
            if step < len(items):
                s[step] = logits(step)
            if 0 <= step - 1 < len(items):
                w[step - 1] = weights(step - 1, s.pop(step - 1))
            if 0 <= step - 2 < len(items):
                m_new, alpha, p = w.pop(step - 2)
                out.append((m_new, accumulate(step - 2, alpha, p)))
        return tuple(out)

    init = tuple((jnp.full((1, width), NEG, F32), jnp.zeros((V_ROWS, width), F32)) for _ in items)
    final = lax.fori_loop(0, n_ch, attn_chunk, init)
    for (j, u), (_, acc) in zip(items, final):
        o_t = acc[:HEAD_DIM] * (1.0 / acc[HEAD_DIM:HEAD_DIM + 1])
        for g in range(n_grp):
            h = j * n_grp + g
            o_ref[u * PA_QSUB:(u + 1) * PA_QSUB, h * HEAD_DIM:(h + 1) * HEAD_DIM] = (
                o_t[:, g * PA_QSUB:(g + 1) * PA_QSUB].T.astype(BF16))


def _p_attn(p16, wit, kia, kib, vt, batch, seq, topk):
    d_attn = 2048
    p16b = p16.reshape(batch, seq, W16)
    kv_w = N_KV_HEADS * HEAD_DIM
    return pl.pallas_call(
        functools.partial(_p_attn_kernel, topk=topk),
        out_shape=jax.ShapeDtypeStruct((batch, seq, d_attn), BF16),
        grid=(batch, seq // PA_QUERIES),
        in_specs=[pl.BlockSpec((None, PA_QUERIES, d_attn), lambda b, i: (b, i, 0)),
                  pl.BlockSpec((None, PA_QUERIES, 1024), lambda b, i: (b, i, 6)),
                  pl.BlockSpec((None, N_IDX_HEADS, PA_QUERIES), lambda b, i: (b, 0, i)),
                  pl.BlockSpec((None, seq, LANES), lambda b, i: (b, 0, 0)),
                  pl.BlockSpec((None, seq, LANES), lambda b, i: (b, 0, 0)),
                  pl.BlockSpec((None, seq, kv_w), lambda b, i: (b, 0, 14)),
                  pl.BlockSpec((None, N_KV_HEADS * V_ROWS, seq), lambda b, i: (b, 0, 0))],
        out_specs=pl.BlockSpec((None, PA_QUERIES, d_attn), lambda b, i: (b, i, 0)),
        scratch_shapes=[pltpu.VMEM((seq, PA_QUERIES), F32)],
        compiler_params=_cparams(2),
        name="p_attn",
    )(p16b, p16b, wit, kia, kib, p16b, vt)


S_SCORE_GROUP = 4


def _s_score_kernel(pt_ref, qi_ref, w_ref, kin_ref, *rest, n_pages, page, n_new, group):
    del pt_ref
    ik_refs, o_ref = rest[:group * n_pages], rest[group * n_pages]
    for g in range(group):
        qi = qi_ref[g]
        w = w_ref[g]

        def piece(ik_t, qi=qi, w=w):
            d = jnp.dot(qi, ik_t.astype(BF16), preferred_element_type=F32)
            x = jnp.maximum(d, 0.0) * w
            return jnp.concatenate([jnp.sum(x[t * N_IDX_HEADS:(t + 1) * N_IDX_HEADS], axis=0, keepdims=True)
                                    for t in range(n_new)], axis=0)

        for p in range(n_pages):
            o_ref[g, :, p * page:(p + 1) * page] = piece(ik_refs[g * n_pages + p][...])
        new = piece(kin_ref[g])
        col = lax.broadcasted_iota(jnp.int32, new.shape, 1)
        tok = lax.broadcasted_iota(jnp.int32, new.shape, 0)
        o_ref[g, :, n_pages * page:(n_pages + 1) * page] = jnp.where(col <= tok, new, -jnp.inf)


def _page_specs(n_pages, shape, group=1):
    zeros = (0,) * len(shape)
    return [pl.BlockSpec((None,) + shape,
                         functools.partial(lambda b, pt, g, p: (pt[b * group + g, p],) + zeros, g=g, p=p))
            for g in range(group) for p in range(n_pages)]


def _s_score(page_table, qi_th, w_th, ki_new_t, cache_ik_t, n_new):
    nb, n_pages = page_table.shape
    page = cache_ik_t.shape[2]
    width = (n_pages + 1) * page
    hx = qi_th.shape[1]
    group = S_SCORE_GROUP
    assert nb % group == 0
    return pl.pallas_call(
        functools.partial(_s_score_kernel, n_pages=n_pages, page=page, n_new=n_new, group=group),
        out_shape=jax.ShapeDtypeStruct((nb, n_new, width), F32),
        grid_spec=pltpu.PrefetchScalarGridSpec(
            num_scalar_prefetch=1,
            grid=(nb // group,),
            in_specs=[pl.BlockSpec((group, hx, IDX_DIM), lambda b, pt: (b, 0, 0)),
                      pl.BlockSpec((group, hx, 1), lambda b, pt: (b, 0, 0)),
                      pl.BlockSpec((group, IDX_DIM, page), lambda b, pt: (b, 0, 0))]
            + _page_specs(n_pages, (IDX_DIM, page), group),
            out_specs=pl.BlockSpec((group, n_new, width), lambda b, pt: (b, 0, 0))),
        compiler_params=_cparams(1),
        name="s_score",
    )(page_table, qi_th, w_th, ki_new_t, *([cache_ik_t] * (group * n_pages)))


def _s_sel_kernel(sc_ref, o_ref, buf_ref, *, topk, chunk):
    n_rows = sc_ref.shape[0]
    buf_ref[...] = sc_ref[...]
    thr = _kth_largest_cols(buf_ref, n_rows // chunk, chunk, topk)
    _break_ties_cols(buf_ref, n_rows // chunk, chunk, topk, thr, n_rows)
    o_ref[...] = jnp.where(buf_ref[...] >= thr, 1.0, 0.0)


def _s_sel(scores_t, topk):
    n_rows, n_q = scores_t.shape
    return pl.pallas_call(
        functools.partial(_s_sel_kernel, topk=topk, chunk=LANES),
        out_shape=jax.ShapeDtypeStruct((n_rows, n_q), F32),
        grid=(n_q // LANES,),
        in_specs=[pl.BlockSpec((n_rows, LANES), lambda i: (0, i))],
        out_specs=pl.BlockSpec((n_rows, LANES), lambda i: (0, i)),
        scratch_shapes=[pltpu.VMEM((n_rows, LANES), F32)],
        compiler_params=_cparams(1),
        name="s_sel",
    )(scores_t)


def _s_attn_kernel(pt_ref, q_ref, sel_ref, kn_ref, vn_ref, *rest, n_pages, n_grp, n_new):
    del pt_ref
    k_refs, v_refs, o_ref = rest[:n_pages], rest[n_pages:2 * n_pages], rest[2 * n_pages]
    q = q_ref[...]
    rows = q.shape[0]
    pr = k_refs[0].shape[0]
    page = pr // N_KV_HEADS
    sel = sel_ref[...].astype(BF16)
    sel = jnp.concatenate([sel[:, p * page:(p + 1) * page] for p in range(n_pages + 1)], axis=0)
    spread = (lax.broadcasted_iota(jnp.int32, (page, pr), 1) // N_KV_HEADS
              == lax.broadcasted_iota(jnp.int32, (page, pr), 0)).astype(BF16)
    sel = jnp.dot(sel, spread, preferred_element_type=F32)
    lane_kv = lax.broadcasted_iota(jnp.int32, (rows, pr), 1) % N_KV_HEADS
    row_kv = lax.broadcasted_iota(jnp.int32, (rows, pr), 0) // (n_grp * n_new)
    head_bias = jnp.where(lane_kv == row_kv, 0.0, NEG)
    pieces = []
    for p in range(n_pages + 1):
        kp = kn_ref[...] if p == n_pages else k_refs[p][...].astype(BF16)
        w = kp.shape[0]
        s = lax.dot_general(q, kp, NT_DIMS, preferred_element_type=F32)
        bias = jnp.concatenate([(sel[p * 8:(p + 1) * 8, :w] - 1.0) * -NEG] * (rows // 8), axis=0)
        pieces.append(s + (bias + head_bias[:, :w]))
    m = functools.reduce(jnp.maximum, [jnp.max(s, axis=1, keepdims=True) for s in pieces])
    l = jnp.zeros((rows, 1), F32)
    o = jnp.zeros((rows, HEAD_DIM), F32)
    for p in range(n_pages + 1):
        e = jnp.exp2(pieces[p] - m)
        l = l + jnp.sum(e, axis=1, keepdims=True)
        vp = vn_ref[...] if p == n_pages else v_refs[p][...].astype(BF16)
        o = o + jnp.dot(e.astype(BF16), vp, preferred_element_type=F32)
    o_ref[...] = (o / l).astype(BF16)


def _s_attn(page_table, q_ht, sel8, k_new2d, v_new2d, cache_k2d, cache_v2d, n_grp, n_new):
    nb, n_pages = page_table.shape
    pr = cache_k2d.shape[1]
    pr_new = k_new2d.shape[1]
    rows = q_ht.shape[1]
    width = sel8.shape[-1]
    return pl.pallas_call(
        functools.partial(_s_attn_kernel, n_pages=n_pages, n_grp=n_grp, n_new=n_new),
        out_shape=jax.ShapeDtypeStruct((nb, rows, HEAD_DIM), BF16),
        grid_spec=pltpu.PrefetchScalarGridSpec(
            num_scalar_prefetch=1,
            grid=(nb,),
            in_specs=[pl.BlockSpec((None, rows, HEAD_DIM), lambda b, pt: (b, 0, 0)),
                      pl.BlockSpec((None, 8, width), lambda b, pt: (b, 0, 0)),
                      pl.BlockSpec((None, pr_new, HEAD_DIM), lambda b, pt: (b, 0, 0)),
                      pl.BlockSpec((None, pr_new, HEAD_DIM), lambda b, pt: (b, 0, 0))]
            + _page_specs(n_pages, (pr, HEAD_DIM)) + _page_specs(n_pages, (pr, HEAD_DIM)),
            out_specs=pl.BlockSpec((None, rows, HEAD_DIM), lambda b, pt: (b, 0, 0))),
        compiler_params=_cparams(1),
        name="s_attn",
    )(page_table, q_ht, sel8, k_new2d, v_new2d, *([cache_k2d] * n_pages), *([cache_v2d] * n_pages))


def _mix_kernel(*refs, tiles_per_seq):
    if tiles_per_seq is None:
        d_ref, a_ref, ga_ref, gb_ref, wg_ref, ps_ref, wup_ref, wua_ref, o_ref = refs
        group = wg_ref.shape[1]
        d = [d_ref[:, g * group:(g + 1) * group] for g in range(wg_ref.shape[0])]
    else:
        pin_ref, halo_ref, a_ref, ga_ref, gb_ref, wg_ref, ps_ref, wup_ref, wua_ref, o_ref, buf_ref = refs
        t = pl.program_id(0) % tiles_per_seq
        tm = pin_ref.shape[0]
        halo = jnp.where(t > 0, halo_ref[...], 0.0)
        pos = t * tm + lax.broadcasted_iota(jnp.int32, (tm, 1), 0)
        d = _pool_d_tile(pin_ref[...], halo, pos, buf_ref)
    y = jnp.concatenate([jnp.dot(d[g], wg_ref[g], preferred_element_type=F32) for g in range(len(d))],
                        axis=1) * ps_ref[...]
    yp = jnp.dot(y.astype(BF16), wup_ref[...], preferred_element_type=F32)
    ya = jnp.dot(a_ref[...], wua_ref[...], preferred_element_type=F32)
    o_ref[...] = (jax.nn.sigmoid(ga_ref[...].astype(F32)) * yp
                  + jax.nn.sigmoid(gb_ref[...].astype(F32)) * ya).astype(BF16)


def _mix(pool_in, attn, p16, w_grp, pool_scale, w_up_pool, w_up_attn, tm, tiles_per_seq=None):
    m, dm = attn.shape
    pw = pool_scale.shape[1]
    const = lambda *shape: pl.BlockSpec(shape, lambda i: (0,) * len(shape))
    if tiles_per_seq is None:
        pool_specs, pool_args, scratch = [pl.BlockSpec((tm, pw), lambda i: (i, 0))], [pool_in], []
    else:
        halo_blocks = tm // POOL_HALO
        pool_specs = [pl.BlockSpec((tm, pw), lambda i: (i, 0)),
                      pl.BlockSpec((POOL_HALO, pw), lambda i: (jnp.maximum(i * halo_blocks - 1, 0), 0))]
        pool_args, scratch = [pool_in, pool_in], [pltpu.VMEM((tm + POOL_HALO, pw), F32)]
    return pl.pallas_call(
        functools.partial(_mix_kernel, tiles_per_seq=tiles_per_seq),
        out_shape=jax.ShapeDtypeStruct((m, dm), BF16),
        grid=(m // tm,),
        in_specs=pool_specs + [
            pl.BlockSpec((tm, dm), lambda i: (i, 0)),
            pl.BlockSpec((tm, dm), lambda i: (i, 1)),
            pl.BlockSpec((tm, dm), lambda i: (i, 2)),
            const(*w_grp.shape), const(1, pw), const(*w_up_pool.shape), const(*w_up_attn.shape)],
        out_specs=pl.BlockSpec((tm, dm), lambda i: (i, 0)),
        scratch_shapes=scratch,
        compiler_params=_cparams(1),
        name="mix",
    )(*pool_args, attn, p16, p16, w_grp, pool_scale, w_up_pool, w_up_attn)


def _out_kernel(mix_ref, x_ref, gate_ref, sh_ref, sc_ref, g_ref, w_ref, h_ref, hn_ref):
    h = x_ref[...] + gate_ref[...] * jnp.dot(mix_ref[...], w_ref[...], preferred_element_type=F32)
    h_ref[...] = h
    hn_ref[...] = (_rms(h, g_ref[...]) * (1.0 + sc_ref[...]) + sh_ref[...]).astype(BF16)


def _out(mix, x, gate, shift, scale, g2, w_out, tm, rows_per_mod):
    m, d = x.shape
    return pl.pallas_call(
        _out_kernel,
        out_shape=(jax.ShapeDtypeStruct((m, d), F32), jax.ShapeDtypeStruct((m, d), BF16)),
        grid=(m // tm,),
        in_specs=[pl.BlockSpec((tm, d), lambda i: (i, 0)),
                  pl.BlockSpec((tm, d), lambda i: (i, 0)),
                  _mod_spec(gate, tm, rows_per_mod),
                  _mod_spec(shift, tm, rows_per_mod),
                  _mod_spec(scale, tm, rows_per_mod),
                  pl.BlockSpec((1, d), lambda i: (0, 0)),
                  pl.BlockSpec((d, d), lambda i: (0, 0))],
        out_specs=(pl.BlockSpec((tm, d), lambda i: (i, 0)), pl.BlockSpec((tm, d), lambda i: (i, 0))),
        compiler_params=_cparams(1),
        name="out",
    )(mix, x, gate, shift, scale, g2, w_out)


FFN_TF = 512


def _ffn_kernel(hn_ref, h_ref, gate_ref, gf_ref, wg_ref, wu_ref, wo_ref, o_ref, acc_ref):
    f = pl.program_id(1)

    @pl.when(f == 0)
    def _():
        acc_ref[...] = jnp.zeros_like(acc_ref)

    hn = hn_ref[...]
    a = jnp.dot(hn, wg_ref[...], preferred_element_type=F32)
    u = jnp.dot(hn, wu_ref[...], preferred_element_type=F32)
    z = (a * jax.nn.sigmoid(a) * u).astype(BF16)
    acc_ref[...] += jnp.dot(z, wo_ref[...], preferred_element_type=F32)

    @pl.when(f == pl.num_programs(1) - 1)
    def _():
        o_ref[...] = _rms(h_ref[...] + gate_ref[...] * acc_ref[...], gf_ref[...])


def _ffn(hn, h, gate, g_final, w_ffn_in, w_ffn_out, tm, rows_per_mod):
    m, d = h.shape
    d_ff = w_ffn_out.shape[0]
    n_f = d_ff // FFN_TF
    return pl.pallas_call(
        _ffn_kernel,
        out_shape=jax.ShapeDtypeStruct((m, d), F32),
        grid=(m // tm, n_f),
        in_specs=[pl.BlockSpec((tm, d), lambda i, f: (i, 0)),
                  pl.BlockSpec((tm, d), lambda i, f: (i, 0)),
                  _mod_spec(gate, tm, rows_per_mod),
                  pl.BlockSpec((1, d), lambda i, f: (0, 0)),
                  pl.BlockSpec((d, FFN_TF), lambda i, f: (0, f)),
                  pl.BlockSpec((d, FFN_TF), lambda i, f: (0, f + n_f)),
                  pl.BlockSpec((FFN_TF, d), lambda i, f: (f, 0))],
        out_specs=pl.BlockSpec((tm, d), lambda i, f: (i, 0)),
        scratch_shapes=[pltpu.VMEM((tm, d), F32)],
        compiler_params=_cparams(2),
        name="ffn",
    )(hn, h, gate, g_final, w_ffn_in, w_ffn_in, w_ffn_out)


def _back(x2, p16, pool_in, attn, mods, lw, tm, rows_per_mod, tiles_per_seq=None):
    mix = _mix(pool_in, attn, p16, lw["w_grp"], lw["pool_scale"], lw["w_up_pool"], lw["w_up_attn"], tm,
               tiles_per_seq)
    h, hn = _out(mix, x2, mods[2], mods[3], mods[4], lw["g2"], lw["w_out"], tm, rows_per_mod)
    return _ffn(hn, h, mods[5], lw["g_final"], lw["w_ffn_in"], lw["w_ffn_out"], tm, rows_per_mod)


def kernel(x_prompt, x_sample, cache_k, cache_v, cache_idx_k, state_pool, page_table, c_prompt, c_sample,
           w_ada, b_ada, g_norm1, w_in, w_pool_grp, pool_scale, w_up_pool, w_up_attn, w_out, g_norm2,
           w_ffn_in, w_ffn_out, g_final):
    batch, seq, dm = x_prompt.shape
    nb, n_new, _ = x_sample.shape
    depth = w_ada.shape[0]
    assert depth == 1, "single-layer step"
    n_phys, page = cache_k.shape[1], cache_k.shape[2]
    n_pages = page_table.shape[1]
    past = n_pages * page
    pool_w = state_pool.shape[-1]
    kv_w = N_KV_HEADS * HEAD_DIM
    n_heads = dm // HEAD_DIM
    n_grp = n_heads // N_KV_HEADS
    idx_w = N_IDX_HEADS * IDX_DIM
    assert page == LANES and n_new == 4 and seq % PA_CHUNK == 0

    w_t = jnp.swapaxes(w_in[0], 0, 1)
    assert (pool_w, dm, kv_w, idx_w) == (2 * FRONT_TN, 4 * FRONT_TN, FRONT_TN, 2 * FRONT_TN)
    o_q, o_k, o_v, o_qi = pool_w, pool_w + dm, pool_w + dm + kv_w, pool_w + dm + 2 * kv_w
    o_ki = o_qi + idx_w
    o_ga = o_ki + IDX_DIM + N_IDX_HEADS
    o_gb = o_ga + dm
    assert o_gb + dm == w_t.shape[0] and o_ki + FRONT_TN <= w_t.shape[0]
    tile_rows = ([o_q + i * FRONT_TN for i in range(4)] + [o_ga + i * FRONT_TN for i in range(4)]
                 + [o_gb + i * FRONT_TN for i in range(4)] + [o_qi, o_qi + FRONT_TN, 0, FRONT_TN, o_k, o_v, o_ki])
    assert len(tile_rows) == N_FRONT_TILES and all(r % 8 == 0 for r in tile_rows)
    lw = {"w_grp": w_pool_grp[0].astype(BF16), "pool_scale": pool_scale[0].reshape(1, pool_w),
          "w_up_pool": w_up_pool[0].astype(BF16), "w_up_attn": w_up_attn[0].astype(BF16),
          "w_out": w_out[0].astype(BF16), "g2": g_norm2[0].reshape(1, dm), "g_final": g_final.reshape(1, dm),
          "w_ffn_in": w_ffn_in[0].astype(BF16), "w_ffn_out": w_ffn_out[0].astype(BF16)}
    g1 = g_norm1[0].reshape(1, dm)

    n_c = batch + nb
    c_all = jnp.concatenate([c_prompt, c_sample, jnp.zeros((-n_c % 8, dm), F32)], axis=0)
    mod = _ada(c_all, w_ada[0], b_ada[0])
    mods_p = [mod[:batch, i * dm:(i + 1) * dm].reshape(batch, 1, dm) for i in range(6)]
    mods_s = [jnp.repeat(mod[batch:n_c, i * dm:(i + 1) * dm], n_new, axis=0) for i in range(6)]

    xp = x_prompt.reshape(batch * seq, dm)
    p16, p32 = _front(xp, g1, mods_p[0], mods_p[1], w_t, tile_rows, FRONT_ROWS, seq)
    p32b = p32.reshape(batch, seq, W32)
    k_prompt = p32b[:, :, pool_w:pool_w + kv_w].reshape(1, batch, seq, N_KV_HEADS, HEAD_DIM)
    v_prompt = p32b[:, :, pool_w + kv_w:pool_w + 2 * kv_w].reshape(1, batch, seq, N_KV_HEADS, HEAD_DIM)
    o_small = pool_w + 2 * kv_w
    ki_p = p32b[:, :, o_small:o_small + IDX_DIM]
    wi_p = p32b[:, :, o_small + IDX_DIM:o_small + IDX_DIM + N_IDX_HEADS]
    pool_prompt = p32b[:, seq - POOL_BUF:, :pool_w][None]
    ki16 = ki_p.astype(BF16)
    zeros = jnp.zeros_like(ki16)
    kia = jnp.concatenate([ki16, zeros], axis=-1)
    kib = jnp.concatenate([zeros, ki16], axis=-1)
    wit_p = jnp.swapaxes(wi_p, 1, 2)
    vt_p = jnp.swapaxes(p16.reshape(batch, seq, W16)[:, :, W16 - kv_w:], 1, 2)
    vt_p = jnp.concatenate([vt_p.reshape(batch, N_KV_HEADS, HEAD_DIM, seq),
                            jnp.ones((batch, N_KV_HEADS, V_ROWS - HEAD_DIM, seq), BF16)], axis=2)
    vt_p = vt_p.reshape(batch, N_KV_HEADS * V_ROWS, seq)
    attn_p = _p_attn(p16, wit_p, kia, kib, vt_p, batch, seq, min(TOPK_MAX, seq // 4)).reshape(batch * seq, dm)
    assert seq % ROW_TILE == 0 and ROW_TILE % POOL_HALO == 0 and pool_w == FRONT_TN * 2
    y_prompt = _back(xp, p16, p32, attn_p, mods_p, lw, ROW_TILE, seq, seq // ROW_TILE).reshape(batch, seq, dm)

    xs = x_sample.reshape(nb * n_new, dm)
    s16, s32 = _front(xs, g1, mods_s[0], mods_s[1], w_t, tile_rows, nb * n_new, 0)
    s32b = s32.reshape(nb, n_new, W32)
    k_new = s32b[:, :, pool_w:pool_w + kv_w]
    v_new = s32b[:, :, pool_w + kv_w:pool_w + 2 * kv_w]
    ki_new = s32b[:, :, o_small:o_small + IDX_DIM]
    wi_s = s32b[:, :, o_small + IDX_DIM:o_small + IDX_DIM + N_IDX_HEADS]
    seq_s = jnp.concatenate([state_pool[0], s32b[:, :, :pool_w]], axis=1)
    pool_sample = seq_s[:, n_new:][None]
    d_s = _pool_d_sample(jnp.swapaxes(seq_s, 0, 1), n_new)
    d_s = jnp.swapaxes(d_s, 0, 1).reshape(nb * n_new, pool_w)

    s16b = s16.reshape(nb, n_new, W16)
    hx = N_IDX_HEADS * n_new
    qi_th = s16b[:, :, 3 * dm:3 * dm + idx_w].reshape(nb, hx, IDX_DIM)
    w_th = (wi_s * (IDX_DIM ** -0.5 * N_IDX_HEADS ** -0.5)).reshape(nb, hx, 1)
    ki_new_t = jnp.swapaxes(jnp.pad(ki_new, ((0, 0), (0, page - n_new), (0, 0))), 1, 2)
    cache_ik_t = jnp.swapaxes(cache_idx_k[0], 1, 2)
    scores = _s_score(page_table, qi_th, w_th, ki_new_t, cache_ik_t, n_new)
    topk_s = min(TOPK_MAX, (past + n_new) // 4)
    width = scores.shape[-1]
    sel_t = _s_sel(scores.reshape(nb * n_new, width).T, topk_s)
    sel4 = sel_t.T.reshape(nb, n_new, width)
    sel8 = jnp.concatenate([sel4, sel4], axis=1)

    q_ht = jnp.swapaxes(s16b[:, :, :dm].reshape(nb, n_new, n_heads, HEAD_DIM), 1, 2)
    q_ht = q_ht.reshape(nb, n_heads * n_new, HEAD_DIM)
    pr = page * N_KV_HEADS
    k_new2d = jnp.pad(k_new.astype(BF16), ((0, 0), (0, 8 - n_new), (0, 0))).reshape(nb, 8 * N_KV_HEADS, HEAD_DIM)
    v_new2d = jnp.pad(v_new.astype(BF16), ((0, 0), (0, 8 - n_new), (0, 0))).reshape(nb, 8 * N_KV_HEADS, HEAD_DIM)
    ck2d = cache_k[0].reshape(n_phys, pr, HEAD_DIM)
    cv2d = cache_v[0].reshape(n_phys, pr, HEAD_DIM)
    o_s = _s_attn(page_table, q_ht, sel8, k_new2d, v_new2d, ck2d, cv2d, n_grp, n_new)
    attn_s = jnp.swapaxes(o_s.reshape(nb, n_heads, n_new, HEAD_DIM), 1, 2).reshape(nb * n_new, dm)
    y_sample = _back(xs, s16, d_s, attn_s, mods_s, lw, nb * n_new, 0).reshape(nb, n_new, dm)

    k_sample = k_new.reshape(1, nb, n_new, N_KV_HEADS, HEAD_DIM)
    v_sample = v_new.reshape(1, nb, n_new, N_KV_HEADS, HEAD_DIM)
    return (y_prompt, y_sample, k_prompt, v_prompt, ki_p[None], pool_prompt,
            k_sample, v_sample, ki_new[None], pool_sample)
```

```python
import functools
import math

import jax
import jax.numpy as jnp
from jax import lax
from jax.experimental import pallas as pl
from jax.experimental.pallas import tpu as pltpu

F32 = jnp.float32
BF16 = jnp.bfloat16

RMS_EPS = 1e-6
POOL_WINDOWS = (2, 4, 8, 16)
POOL_BUF = max(POOL_WINDOWS) - 1
HEAD_DIM = 128
N_KV_HEADS = 4
N_IDX_HEADS = 16
IDX_DIM = 64
TOPK_MAX = 256
Q_BLOCK = 128
LANES = 128
NEG = -1e30
Q_SCALE = HEAD_DIM ** -0.5 * math.log2(math.e)
V_ROWS = HEAD_DIM + 16
NT_DIMS = (((1,), (1,)), ((), ()))
VMEM_LIMIT = 56 * 1024 * 1024
FRONT_ROWS = 1024
ROW_TILE = 512


def _cparams(n_axes):
    return pltpu.CompilerParams(dimension_semantics=("arbitrary",) * n_axes, vmem_limit_bytes=VMEM_LIMIT)


def _rms(x, g):
    return x * lax.rsqrt(jnp.mean(x * x, axis=-1, keepdims=True) + RMS_EPS) * g


def _ada_kernel(c_ref, w_ref, b_ref, o_ref):
    c = c_ref[...]
    s = (c * jax.nn.sigmoid(c)).astype(BF16)
    o_ref[...] = jnp.dot(s, w_ref[...].astype(BF16), preferred_element_type=F32) + b_ref[...]


def _ada(c_all, w_ada, b_ada):
    rows, d = c_all.shape
    n = w_ada.shape[1]
    tn = 512
    return pl.pallas_call(
        _ada_kernel,
        out_shape=jax.ShapeDtypeStruct((rows, n), F32),
        grid=(n // tn,),
        in_specs=[pl.BlockSpec((rows, d), lambda j: (0, 0)),
                  pl.BlockSpec((d, tn), lambda j: (0, j)),
                  pl.BlockSpec((1, tn), lambda j: (0, j))],
        out_specs=pl.BlockSpec((rows, tn), lambda j: (0, j)),
        compiler_params=_cparams(1),
        name="ada",
    )(c_all, w_ada, b_ada.reshape(1, n))


FRONT_TN = 512
N_FRONT_TILES = 19
N_Q_TILES = 4
F32_TILE0 = 14
K_TILE, V_TILE = 16, 17
W16 = 16 * FRONT_TN
W32 = 5 * FRONT_TN


def _front_kernel(x_ref, g_ref, sh_ref, sc_ref, wt_ref, o16_ref, o32_ref, u_ref):
    n = pl.program_id(1)

    @pl.when(n == 0)
    def _():
        u_ref[...] = (_rms(x_ref[...], g_ref[...]) * (1.0 + sc_ref[...]) + sh_ref[...]).astype(BF16)

    r = lax.dot_general(u_ref[...], wt_ref[...].astype(BF16), NT_DIMS, preferred_element_type=F32)

    @pl.when((n < F32_TILE0) | (n == K_TILE) | (n == V_TILE))
    def _():
        o16_ref[...] = (r * jnp.where(n < N_Q_TILES, Q_SCALE, 1.0)).astype(BF16)

    @pl.when(n >= F32_TILE0)
    def _():
        o32_ref[...] = r


def _lookup(n, table):
    out = jnp.int32(table[-1])
    for i, v in enumerate(table[:-1]):
        out = jnp.where(n == i, v, out)
    return out


def _o16_block(n):
    return jnp.where(n < F32_TILE0, n,
                     jnp.where(n < K_TILE, F32_TILE0 - 1, jnp.where(n <= V_TILE, n - 2, V_TILE - 2)))


def _mod_spec(mod, tm, rows_per_mod):
    d = mod.shape[-1]
    if mod.ndim == 3:
        tiles_per_batch = rows_per_mod // tm
        return pl.BlockSpec((None, 1, d), lambda i, *_: (i // tiles_per_batch, 0, 0))
    return pl.BlockSpec((tm, d), lambda i, *_: (i, 0))


def _front(x, g1, shift, scale, w_t, tile_rows, tm, rows_per_mod):
    m, d = x.shape
    return pl.pallas_call(
        _front_kernel,
        out_shape=(jax.ShapeDtypeStruct((m, W16), BF16), jax.ShapeDtypeStruct((m, W32), F32)),
        grid=(m // tm, N_FRONT_TILES),
        in_specs=[pl.BlockSpec((tm, d), lambda i, n: (i, 0)),
                  pl.BlockSpec((1, d), lambda i, n: (0, 0)),
                  _mod_spec(shift, tm, rows_per_mod),
                  _mod_spec(scale, tm, rows_per_mod),
                  pl.BlockSpec((pl.Element(FRONT_TN), pl.Element(d)),
                               lambda i, n: (_lookup(n, [r // 8 for r in tile_rows]) * 8, 0))],
        out_specs=(pl.BlockSpec((tm, FRONT_TN), lambda i, n: (i, _o16_block(n))),
                   pl.BlockSpec((tm, FRONT_TN), lambda i, n: (i, jnp.maximum(n - F32_TILE0, 0)))),
        scratch_shapes=[pltpu.VMEM((tm, d), BF16)],
        compiler_params=_cparams(2),
        name="front",
    )(x, g1, shift, scale, w_t)


POOL_HALO = 16


def _pool_d_tile(cur, halo, pos, buf_ref):
    tp, width = cur.shape
    group = width // len(POOL_WINDOWS)
    buf_ref[0:POOL_HALO, :] = halo
    buf_ref[POOL_HALO:POOL_HALO + tp, :] = cur
    out = []
    for gi, w in enumerate(POOL_WINDOWS):
        sl = slice(gi * group, (gi + 1) * group)
        acc = cur[:, sl]
        for j in range(1, w):
            acc = acc + buf_ref[POOL_HALO - j:POOL_HALO - j + tp, sl]
        cnt = jnp.minimum(pos + 1, w).astype(F32)
        out.append((acc / cnt - cur[:, sl]).astype(BF16))
    return out


def _pool_d_sample_kernel(seq_ref, o_ref, *, n_new, group):
    for t in range(n_new):
        r = POOL_BUF + t
        for gi, w in enumerate(POOL_WINDOWS):
            sl = slice(gi * group, (gi + 1) * group)
            acc = seq_ref[r, :, sl]
            for j in range(1, w):
                acc = acc + seq_ref[r - j, :, sl]
            o_ref[t, :, sl] = (acc / float(w) - seq_ref[r, :, sl]).astype(BF16)


def _pool_d_sample(seq_t, n_new):
    rows, nb, width = seq_t.shape
    group = width // len(POOL_WINDOWS)
    return pl.pallas_call(
        functools.partial(_pool_d_sample_kernel, n_new=n_new, group=group),
        out_shape=jax.ShapeDtypeStruct((n_new, nb, width), BF16),
        grid=(1,),
        in_specs=[pl.BlockSpec((rows, nb, width), lambda i: (0, 0, 0))],
        out_specs=pl.BlockSpec((n_new, nb, width), lambda i: (0, 0, 0)),
        compiler_params=_cparams(1),
        name="pool_d_sample",
    )(seq_t)


def _key_to_float(key):
    bits = key ^ ((key >> 31) & jnp.int32(0x7FFFFFFF))
    return lax.bitcast_convert_type(bits, F32)


def _count_cols(sc_ref, n_chunks, chunk, flag):
    n_acc = 4
    lanes = sc_ref.shape[1]

    def chunk_body(c, cnts):
        c0 = pl.multiple_of(c * chunk, chunk)
        s = sc_ref[pl.ds(c0, chunk), :]
        cnts = list(cnts)
        for r in range(chunk // 8):
            cnts[r % n_acc] = cnts[r % n_acc] + flag(s[r * 8:(r + 1) * 8], c0 + r * 8)
        return tuple(cnts)

    cnts = lax.fori_loop(0, n_chunks, chunk_body, tuple(jnp.zeros((8, lanes), jnp.int32) for _ in range(n_acc)))
    return jnp.sum(functools.reduce(jnp.add, cnts), axis=0, keepdims=True)


def _kth_largest_cols(sc_ref, n_chunks, chunk, k):
    int_min = jnp.int32(-(2 ** 31))
    lanes = sc_ref.shape[1]

    def bit_body(it, key):
        cand = key ^ lax.shift_left(jnp.int32(1), 31 - it)
        cf = jnp.broadcast_to(_key_to_float(cand), (8, lanes))
        tot = _count_cols(sc_ref, n_chunks, chunk, lambda s, r0: jnp.where(s >= cf, 1, 0))
        return jnp.where(tot >= k, cand, key)

    key = lax.fori_loop(0, 32, bit_body, jnp.full((1, lanes), int_min, jnp.int32))
    return _key_to_float(key)


def _break_ties_cols(sc_ref, n_chunks, chunk, k, thr, n_rows):
    lanes = sc_ref.shape[1]
    thr8 = jnp.broadcast_to(thr, (8, lanes))
    n_ge = _count_cols(sc_ref, n_chunks, chunk, lambda s, r0: jnp.where(s >= thr8, 1, 0))

    @pl.when(jnp.max(n_ge) > k)
    def _():
        need = k - _count_cols(sc_ref, n_chunks, chunk, lambda s, r0: jnp.where(s > thr8, 1, 0))
        row8 = lax.broadcasted_iota(jnp.int32, (8, lanes), 0)
        n_bits = (n_rows - 1).bit_length()

        def bit_body(it, last):
            cand = last | lax.shift_left(jnp.int32(1), n_bits - 1 - it)
            cand8 = jnp.broadcast_to(cand, (8, lanes))
            before = _count_cols(sc_ref, n_chunks, chunk,
                                 lambda s, r0: jnp.where(s == thr8, jnp.where(row8 + r0 < cand8, 1, 0), 0))
            return jnp.where(before < need, cand, last)

        last = lax.fori_loop(0, n_bits, bit_body, jnp.zeros((1, lanes), jnp.int32))

        def drop(c, carry):
            c0 = pl.multiple_of(c * chunk, chunk)
            s = sc_ref[pl.ds(c0, chunk), :]
            row = c0 + lax.broadcasted_iota(jnp.int32, (chunk, lanes), 0)
            sc_ref[pl.ds(c0, chunk), :] = jnp.where(s == thr, jnp.where(row > last, -jnp.inf, s), s)
            return carry

        lax.fori_loop(0, n_chunks, drop, 0)


PA_CHUNK = 512
PA_QUERIES = 512
PA_QSUB = LANES


def _p_attn_kernel(q_ref, qi_ref, wit_ref, kia_ref, kib_ref, k_ref, vt_ref, o_ref, sc_ref, *, topk):
    nq = PA_QUERIES
    blk = pl.program_id(1)
    t0 = blk * nq
    n_ch = (t0 + nq + PA_CHUNK - 1) // PA_CHUNK
    tok = t0 + lax.broadcasted_iota(jnp.int32, (1, nq), 1)
    n_grp = (q_ref.shape[1] // HEAD_DIM) // N_KV_HEADS
    n_pairs = N_IDX_HEADS // 2

    wit = wit_ref[...] * (IDX_DIM ** -0.5 * N_IDX_HEADS ** -0.5)

    def score_chunk(c, carry):
        c0 = pl.multiple_of(c * PA_CHUNK, PA_CHUNK)
        ka = kia_ref[pl.ds(c0, PA_CHUNK), :]
        kb = kib_ref[pl.ds(c0, PA_CHUNK), :]
        acc = jnp.zeros((PA_CHUNK, nq), F32)
        for pp in range(n_pairs // 2):
            qp = jnp.concatenate([qi_ref[:, (2 * pp + i) * LANES:(2 * pp + i + 1) * LANES] for i in range(2)], axis=0)
            da = lax.dot_general(ka, qp, NT_DIMS, preferred_element_type=F32)
            db = lax.dot_general(kb, qp, NT_DIMS, preferred_element_type=F32)
            for i in range(2):
                h = 2 * (2 * pp + i)
                sl = slice(i * nq, (i + 1) * nq)
                acc = acc + jnp.maximum(da[:, sl], 0.0) * wit[h:h + 1, :]
                acc = acc + jnp.maximum(db[:, sl], 0.0) * wit[h + 1:h + 2, :]
        keypos = c0 + lax.broadcasted_iota(jnp.int32, (PA_CHUNK, nq), 0)
        sc_ref[pl.ds(c0, PA_CHUNK), :] = jnp.where(keypos <= tok, acc, -jnp.inf)
        return carry

    lax.fori_loop(0, n_ch, score_chunk, 0)

    thr = _kth_largest_cols(sc_ref, n_ch, PA_CHUNK, topk)
    thr = jnp.where(tok + 1 <= topk, jnp.finfo(F32).min, thr)
    _break_ties_cols(sc_ref, n_ch, PA_CHUNK, topk, thr, sc_ref.shape[0])

    items = [(j, u) for j in range(N_KV_HEADS) for u in range(nq // PA_QSUB)]
    width = n_grp * PA_QSUB

    def attn_chunk(c, carry):
        c0 = pl.multiple_of(c * PA_CHUNK, PA_CHUNK)
        biases = []
        for u in range(nq // PA_QSUB):
            qs = slice(u * PA_QSUB, (u + 1) * PA_QSUB)
            bias = jnp.where(sc_ref[pl.ds(c0, PA_CHUNK), qs] >= thr[:, qs], 0.0, NEG)
            biases.append(jnp.concatenate([bias] * n_grp, axis=1))

        def logits(i):
            j, u = items[i]
            qj = jnp.concatenate([q_ref[u * PA_QSUB:(u + 1) * PA_QSUB,
                                        (j * n_grp + g) * HEAD_DIM:(j * n_grp + g + 1) * HEAD_DIM]
                                  for g in range(n_grp)], axis=0)
            kc = k_ref[pl.ds(c0, PA_CHUNK), j * HEAD_DIM:(j + 1) * HEAD_DIM]
            return lax.dot_general(kc, qj, NT_DIMS, preferred_element_type=F32) + biases[u]

        def weights(i, s):
            m = carry[i][0]
            m_new = jnp.maximum(m, jnp.max(s, axis=0, keepdims=True))
            return m_new, jnp.exp2(m - m_new), jnp.exp2(s - m_new).astype(BF16)

        def accumulate(i, alpha, p):
            j = items[i][0]
            vt = vt_ref[j * V_ROWS:(j + 1) * V_ROWS, pl.ds(c0, PA_CHUNK)]
            return alpha * carry[i][1] + jnp.dot(vt, p, preferred_element_type=F32)

        s, w, out = {}, {}, []
        for step in range(len(items) + 2):
            if step < len(items):
                s[step] = logits(step)
            if 0 <= step - 1 < len(items):
                w[step - 1] = weights(step - 1, s.pop(step - 1))
            if 0 <= step - 2 < len(items):
                m_new, alpha, p = w.pop(step - 2)
                out.append((m_new, accumulate(step - 2, alpha, p)))
        return tuple(out)

    init = tuple((jnp.full((1, width), NEG, F32), jnp.zeros((V_ROWS, width), F32)) for _ in items)
    final = lax.fori_loop(0, n_ch, attn_chunk, init)
    for (j, u), (_, acc) in zip(items, final):
        o_t = acc[:HEAD_DIM] * (1.0 / acc[HEAD_DIM:HEAD_DIM + 1])
        for g in range(n_grp):
            h = j * n_grp + g
            o_ref[u * PA_QSUB:(u + 1) * PA_QSUB, h * HEAD_DIM:(h + 1) * HEAD_DIM] = (
                o_t[:, g * PA_QSUB:(g + 1) * PA_QSUB].T.astype(BF16))


def _p_attn(p16, wit, kia, kib, vt, batch, seq, topk):
    d_attn = 2048
    p16b = p16.reshape(batch, seq, W16)
    kv_w = N_KV_HEADS * HEAD_DIM
    return pl.pallas_call(
        functools.partial(_p_attn_kernel, topk=topk),
        out_shape=jax.ShapeDtypeStruct((batch, seq, d_attn), BF16),
        grid=(batch, seq // PA_QUERIES),
        in_specs=[pl.BlockSpec((None, PA_QUERIES, d_attn), lambda b, i: (b, i, 0)),
                  pl.BlockSpec((None, PA_QUERIES, 1024), lambda b, i: (b, i, 6)),
                  pl.BlockSpec((None, N_IDX_HEADS, PA_QUERIES), lambda b, i: (b, 0, i)),
                  pl.BlockSpec((None, seq, LANES), lambda b, i: (b, 0, 0)),
                  pl.BlockSpec((None, seq, LANES), lambda b, i: (b, 0, 0)),
                  pl.BlockSpec((None, seq, kv_w), lambda b, i: (b, 0, 14)),
                  pl.BlockSpec((None, N_KV_HEADS * V_ROWS, seq), lambda b, i: (b, 0, 0))],
        out_specs=pl.BlockSpec((None, PA_QUERIES, d_attn), lambda b, i: (b, i, 0)),
        scratch_shapes=[pltpu.VMEM((seq, PA_QUERIES), F32)],
        compiler_params=_cparams(2),
        name="p_attn",
    )(p16b, p16b, wit, kia, kib, p16b, vt)


S_SCORE_GROUP = 4


def _s_score_kernel(pt_ref, qi_ref, w_ref, kin_ref, *rest, n_pages, page, n_new, group):
    del pt_ref
    ik_refs, o_ref = rest[:group * n_pages], rest[group * n_pages]
    for g in range(group):
        qi = qi_ref[g]
        w = w_ref[g]

        def piece(ik_t, qi=qi, w=w):
            d = jnp.dot(qi, ik_t.astype(BF16), preferred_element_type=F32)
            x = jnp.maximum(d, 0.0) * w
            return jnp.concatenate([jnp.sum(x[t * N_IDX_HEADS:(t + 1) * N_IDX_HEADS], axis=0, keepdims=True)
                                    for t in range(n_new)], axis=0)

        for p in range(n_pages):
            o_ref[g, :, p * page:(p + 1) * page] = piece(ik_refs[g * n_pages + p][...])
        new = piece(kin_ref[g])
        col = lax.broadcasted_iota(jnp.int32, new.shape, 1)
        tok = lax.broadcasted_iota(jnp.int32, new.shape, 0)
        o_ref[g, :, n_pages * page:(n_pages + 1) * page] = jnp.where(col <= tok, new, -jnp.inf)


def _page_specs(n_pages, shape, group=1):
    zeros = (0,) * len(shape)
    return [pl.BlockSpec((None,) + shape,
                         functools.partial(lambda b, pt, g, p: (pt[b * group + g, p],) + zeros, g=g, p=p))
            for g in range(group) for p in range(n_pages)]


def _s_score(page_table, qi_th, w_th, ki_new_t, cache_ik_t, n_new):
    nb, n_pages = page_table.shape
    page = cache_ik_t.shape[2]
    width = (n_pages + 1) * page
    hx = qi_th.shape[1]
    group = S_SCORE_GROUP
    assert nb % group == 0
    return pl.pallas_call(
        functools.partial(_s_score_kernel, n_pages=n_pages, page=page, n_new=n_new, group=group),
        out_shape=jax.ShapeDtypeStruct((nb, n_new, width), F32),
        grid_spec=pltpu.PrefetchScalarGridSpec(
            num_scalar_prefetch=1,
            grid=(nb // group,),
            in_specs=[pl.BlockSpec((group, hx, IDX_DIM), lambda b, pt: (b, 0, 0)),
                      pl.BlockSpec((group, hx, 1), lambda b, pt: (b, 0, 0)),
                      pl.BlockSpec((group, IDX_DIM, page), lambda b, pt: (b, 0, 0))]
            + _page_specs(n_pages, (IDX_DIM, page), group),
            out_specs=pl.BlockSpec((group, n_new, width), lambda b, pt: (b, 0, 0))),
        compiler_params=_cparams(1),
        name="s_score",
    )(page_table, qi_th, w_th, ki_new_t, *([cache_ik_t] * (group * n_pages)))


def _s_sel_kernel(sc_ref, o_ref, buf_ref, *, topk, chunk):
    n_rows = sc_ref.shape[0]
    buf_ref[...] = sc_ref[...]
    thr = _kth_largest_cols(buf_ref, n_rows // chunk, chunk, topk)
    _break_ties_cols(buf_ref, n_rows // chunk, chunk, topk, thr, n_rows)
    o_ref[...] = jnp.where(buf_ref[...] >= thr, 1.0, 0.0)


def _s_sel(scores_t, topk):
    n_rows, n_q = scores_t.shape
    return pl.pallas_call(
        functools.partial(_s_sel_kernel, topk=topk, chunk=LANES),
        out_shape=jax.ShapeDtypeStruct((n_rows, n_q), F32),
        grid=(n_q // LANES,),
        in_specs=[pl.BlockSpec((n_rows, LANES), lambda i: (0, i))],
        out_specs=pl.BlockSpec((n_rows, LANES), lambda i: (0, i)),
        scratch_shapes=[pltpu.VMEM((n_rows, LANES), F32)],
        compiler_params=_cparams(1),
        name="s_sel",
    )(scores_t)


def _s_attn_kernel(pt_ref, q_ref, sel_ref, kn_ref, vn_ref, *rest, n_pages, n_grp, n_new, group):
    del pt_ref
    gp = group * n_pages
    k_refs, v_refs, o_ref = rest[:gp], rest[gp:2 * gp], rest[2 * gp]
    rows = q_ref.shape[1]
    pr = k_refs[0].shape[0]
    page = pr // N_KV_HEADS
    spread = (lax.broadcasted_iota(jnp.int32, (page, pr), 1) // N_KV_HEADS
              == lax.broadcasted_iota(jnp.int32, (page, pr), 0)).astype(BF16)
    lane_kv = lax.broadcasted_iota(jnp.int32, (rows, pr), 1) % N_KV_HEADS
    row_kv = lax.broadcasted_iota(jnp.int32, (rows, pr), 0) // (n_grp * n_new)
    head_bias = jnp.where(lane_kv == row_kv, 0.0, NEG)
    for g in range(group):
        q = q_ref[g]
        sel = sel_ref[g].astype(BF16)
        sel = jnp.concatenate([sel[:, p * page:(p + 1) * page] for p in range(n_pages + 1)], axis=0)
        sel = jnp.dot(sel, spread, preferred_element_type=F32)
        pieces = []
        for p in range(n_pages + 1):
            kp = kn_ref[g] if p == n_pages else k_refs[g * n_pages + p][...].astype(BF16)
            w = kp.shape[0]
            s = lax.dot_general(q, kp, NT_DIMS, preferred_element_type=F32)
            bias = jnp.concatenate([(sel[p * 8:(p + 1) * 8, :w] - 1.0) * -NEG] * (rows // 8), axis=0)
            pieces.append(s + (bias + head_bias[:, :w]))
        m = functools.reduce(jnp.maximum, [jnp.max(s, axis=1, keepdims=True) for s in pieces])
        l = jnp.zeros((rows, 1), F32)
        o = jnp.zeros((rows, HEAD_DIM), F32)
        for p in range(n_pages + 1):
            e = jnp.exp2(pieces[p] - m)
            l = l + jnp.sum(e, axis=1, keepdims=True)
            vp = vn_ref[g] if p == n_pages else v_refs[g * n_pages + p][...].astype(BF16)
            o = o + jnp.dot(e.astype(BF16), vp, preferred_element_type=F32)
        o_ref[g] = (o / l).astype(BF16)


S_ATTN_GROUP = 2


def _s_attn(page_table, q_ht, sel8, k_new2d, v_new2d, cache_k2d, cache_v2d, n_grp, n_new):
    nb, n_pages = page_table.shape
    pr = cache_k2d.shape[1]
    pr_new = k_new2d.shape[1]
    rows = q_ht.shape[1]
    width = sel8.shape[-1]
    group = S_ATTN_GROUP
    assert nb % group == 0
    return pl.pallas_call(
        functools.partial(_s_attn_kernel, n_pages=n_pages, n_grp=n_grp, n_new=n_new, group=group),
        out_shape=jax.ShapeDtypeStruct((nb, rows, HEAD_DIM), BF16),
        grid_spec=pltpu.PrefetchScalarGridSpec(
            num_scalar_prefetch=1,
            grid=(nb // group,),
            in_specs=[pl.BlockSpec((group, rows, HEAD_DIM), lambda b, pt: (b, 0, 0)),
                      pl.BlockSpec((group, 8, width), lambda b, pt: (b, 0, 0)),
                      pl.BlockSpec((group, pr_new, HEAD_DIM), lambda b, pt: (b, 0, 0)),
                      pl.BlockSpec((group, pr_new, HEAD_DIM), lambda b, pt: (b, 0, 0))]
            + _page_specs(n_pages, (pr, HEAD_DIM), group) + _page_specs(n_pages, (pr, HEAD_DIM), group),
            out_specs=pl.BlockSpec((group, rows, HEAD_DIM), lambda b, pt: (b, 0, 0))),
        compiler_params=_cparams(1),
        name="s_attn",
    )(page_table, q_ht, sel8, k_new2d, v_new2d,
      *([cache_k2d] * (group * n_pages)), *([cache_v2d] * (group * n_pages)))


def _mix_kernel(*refs, tiles_per_seq):
    if tiles_per_seq is None:
        d_ref, a_ref, ga_ref, gb_ref, wg_ref, ps_ref, wup_ref, wua_ref, o_ref = refs
    else:
        pin_ref, halo_ref, a_ref, ga_ref, gb_ref, wg_ref, ps_ref, wup_ref, wua_ref, o_ref, buf_ref = refs
    ya = jnp.dot(a_ref[...], wua_ref[...], preferred_element_type=F32)
    if tiles_per_seq is None:
        group = wg_ref.shape[1]
        d = [d_ref[:, g * group:(g + 1) * group] for g in range(wg_ref.shape[0])]
    else:
        t = pl.program_id(0) % tiles_per_seq
        tm = pin_ref.shape[0]
        halo = jnp.where(t > 0, halo_ref[...], 0.0)
        pos = t * tm + lax.broadcasted_iota(jnp.int32, (tm, 1), 0)
        d = _pool_d_tile(pin_ref[...], halo, pos, buf_ref)
    y = jnp.concatenate([jnp.dot(d[g], wg_ref[g], preferred_element_type=F32) for g in range(len(d))],
                        axis=1) * ps_ref[...]
    yp = jnp.dot(y.astype(BF16), wup_ref[...], preferred_element_type=F32)
    o_ref[...] = (jax.nn.sigmoid(ga_ref[...].astype(F32)) * yp
                  + jax.nn.sigmoid(gb_ref[...].astype(F32)) * ya).astype(BF16)


def _mix(pool_in, attn, p16, w_grp, pool_scale, w_up_pool, w_up_attn, tm, tiles_per_seq=None):
    m, dm = attn.shape
    pw = pool_scale.shape[1]
    const = lambda *shape: pl.BlockSpec(shape, lambda i: (0,) * len(shape))
    if tiles_per_seq is None:
        pool_specs, pool_args, scratch = [pl.BlockSpec((tm, pw), lambda i: (i, 0))], [pool_in], []
    else:
        halo_blocks = tm // POOL_HALO
        pool_specs = [pl.BlockSpec((tm, pw), lambda i: (i, 0)),
                      pl.BlockSpec((POOL_HALO, pw), lambda i: (jnp.maximum(i * halo_blocks - 1, 0), 0))]
        pool_args, scratch = [pool_in, pool_in], [pltpu.VMEM((tm + POOL_HALO, pw), F32)]
    return pl.pallas_call(
        functools.partial(_mix_kernel, tiles_per_seq=tiles_per_seq),
        out_shape=jax.ShapeDtypeStruct((m, dm), BF16),
        grid=(m // tm,),
        in_specs=pool_specs + [
            pl.BlockSpec((tm, dm), lambda i: (i, 0)),
            pl.BlockSpec((tm, dm), lambda i: (i, 1)),
            pl.BlockSpec((tm, dm), lambda i: (i, 2)),
            const(*w_grp.shape), const(1, pw), const(*w_up_pool.shape), const(*w_up_attn.shape)],
        out_specs=pl.BlockSpec((tm, dm), lambda i: (i, 0)),
        scratch_shapes=scratch,
        compiler_params=_cparams(1),
        name="mix",
    )(*pool_args, attn, p16, p16, w_grp, pool_scale, w_up_pool, w_up_attn)


def _out_kernel(mix_ref, x_ref, gate_ref, sh_ref, sc_ref, g_ref, w_ref, h_ref, hn_ref):
    h = x_ref[...] + gate_ref[...] * jnp.dot(mix_ref[...], w_ref[...], preferred_element_type=F32)
    h_ref[...] = h
    hn_ref[...] = (_rms(h, g_ref[...]) * (1.0 + sc_ref[...]) + sh_ref[...]).astype(BF16)


def _out(mix, x, gate, shift, scale, g2, w_out, tm, rows_per_mod):
    m, d = x.shape
    return pl.pallas_call(
        _out_kernel,
        out_shape=(jax.ShapeDtypeStruct((m, d), F32), jax.ShapeDtypeStruct((m, d), BF16)),
        grid=(m // tm,),
        in_specs=[pl.BlockSpec((tm, d), lambda i: (i, 0)),
                  pl.BlockSpec((tm, d), lambda i: (i, 0)),
                  _mod_spec(gate, tm, rows_per_mod),
                  _mod_spec(shift, tm, rows_per_mod),
                  _mod_spec(scale, tm, rows_per_mod),
                  pl.BlockSpec((1, d), lambda i: (0, 0)),
                  pl.BlockSpec((d, d), lambda i: (0, 0))],
        out_specs=(pl.BlockSpec((tm, d), lambda i: (i, 0)), pl.BlockSpec((tm, d), lambda i: (i, 0))),
        compiler_params=_cparams(1),
        name="out",
    )(mix, x, gate, shift, scale, g2, w_out)


FFN_TF = 512


def _ffn_kernel(hn_ref, h_ref, gate_ref, gf_ref, wg_ref, wu_ref, wo_ref, o_ref, acc_ref):
    f = pl.program_id(1)

    @pl.when(f == 0)
    def _():
        acc_ref[...] = jnp.zeros_like(acc_ref)

    hn = hn_ref[...]
    a = jnp.dot(hn, wg_ref[...], preferred_element_type=F32)
    u = jnp.dot(hn, wu_ref[...], preferred_element_type=F32)
    z = (a * jax.nn.sigmoid(a) * u).astype(BF16)
    acc_ref[...] += jnp.dot(z, wo_ref[...], preferred_element_type=F32)

    @pl.when(f == pl.num_programs(1) - 1)
    def _():
        o_ref[...] = _rms(h_ref[...] + gate_ref[...] * acc_ref[...], gf_ref[...])


def _ffn(hn, h, gate, g_final, w_ffn_in, w_ffn_out, tm, rows_per_mod):
    m, d = h.shape
    d_ff = w_ffn_out.shape[0]
    n_f = d_ff // FFN_TF
    return pl.pallas_call(
        _ffn_kernel,
        out_shape=jax.ShapeDtypeStruct((m, d), F32),
        grid=(m // tm, n_f),
        in_specs=[pl.BlockSpec((tm, d), lambda i, f: (i, 0)),
                  pl.BlockSpec((tm, d), lambda i, f: (i, 0)),
                  _mod_spec(gate, tm, rows_per_mod),
                  pl.BlockSpec((1, d), lambda i, f: (0, 0)),
                  pl.BlockSpec((d, FFN_TF), lambda i, f: (0, f)),
                  pl.BlockSpec((d, FFN_TF), lambda i, f: (0, f + n_f)),
                  pl.BlockSpec((FFN_TF, d), lambda i, f: (f, 0))],
        out_specs=pl.BlockSpec((tm, d), lambda i, f: (i, 0)),
        scratch_shapes=[pltpu.VMEM((tm, d), F32)],
        compiler_params=_cparams(2),
        name="ffn",
    )(hn, h, gate, g_final, w_ffn_in, w_ffn_in, w_ffn_out)


def _back(x2, p16, pool_in, attn, mods, lw, tm, rows_per_mod, tiles_per_seq=None):
    mix = _mix(pool_in, attn, p16, lw["w_grp"], lw["pool_scale"], lw["w_up_pool"], lw["w_up_attn"], tm,
               tiles_per_seq)
    h, hn = _out(mix, x2, mods[2], mods[3], mods[4], lw["g2"], lw["w_out"], tm, rows_per_mod)
    return _ffn(hn, h, mods[5], lw["g_final"], lw["w_ffn_in"], lw["w_ffn_out"], tm, rows_per_mod)


def kernel(x_prompt, x_sample, cache_k, cache_v, cache_idx_k, state_pool, page_table, c_prompt, c_sample,
           w_ada, b_ada, g_norm1, w_in, w_pool_grp, pool_scale, w_up_pool, w_up_attn, w_out, g_norm2,
           w_ffn_in, w_ffn_out, g_final):
    batch, seq, dm = x_prompt.shape
    nb, n_new, _ = x_sample.shape
    depth = w_ada.shape[0]
    assert depth == 1, "single-layer step"
    n_phys, page = cache_k.shape[1], cache_k.shape[2]
    n_pages = page_table.shape[1]
    past = n_pages * page
    pool_w = state_pool.shape[-1]
    kv_w = N_KV_HEADS * HEAD_DIM
    n_heads = dm // HEAD_DIM
    n_grp = n_heads // N_KV_HEADS
    idx_w = N_IDX_HEADS * IDX_DIM
    assert page == LANES and n_new == 4 and seq % PA_CHUNK == 0

    w_t = jnp.swapaxes(w_in[0], 0, 1)
    assert (pool_w, dm, kv_w, idx_w) == (2 * FRONT_TN, 4 * FRONT_TN, FRONT_TN, 2 * FRONT_TN)
    o_q, o_k, o_v, o_qi = pool_w, pool_w + dm, pool_w + dm + kv_w, pool_w + dm + 2 * kv_w
    o_ki = o_qi + idx_w
    o_ga = o_ki + IDX_DIM + N_IDX_HEADS
    o_gb = o_ga + dm
    assert o_gb + dm == w_t.shape[0] and o_ki + FRONT_TN <= w_t.shape[0]
    tile_rows = ([o_q + i * FRONT_TN for i in range(4)] + [o_ga + i * FRONT_TN for i in range(4)]
                 + [o_gb + i * FRONT_TN for i in range(4)] + [o_qi, o_qi + FRONT_TN, 0, FRONT_TN, o_k, o_v, o_ki])
    assert len(tile_rows) == N_FRONT_TILES and all(r % 8 == 0 for r in tile_rows)
    lw = {"w_grp": w_pool_grp[0].astype(BF16), "pool_scale": pool_scale[0].reshape(1, pool_w),
          "w_up_pool": w_up_pool[0].astype(BF16), "w_up_attn": w_up_attn[0].astype(BF16),
          "w_out": w_out[0].astype(BF16), "g2": g_norm2[0].reshape(1, dm), "g_final": g_final.reshape(1, dm),
          "w_ffn_in": w_ffn_in[0].astype(BF16), "w_ffn_out": w_ffn_out[0].astype(BF16)}
    g1 = g_norm1[0].reshape(1, dm)

    n_c = batch + nb
    c_all = jnp.concatenate([c_prompt, c_sample, jnp.zeros((-n_c % 8, dm), F32)], axis=0)
    mod = _ada(c_all, w_ada[0], b_ada[0])
    mods_p = [mod[:batch, i * dm:(i + 1) * dm].reshape(batch, 1, dm) for i in range(6)]
    mods_s = [jnp.repeat(mod[batch:n_c, i * dm:(i + 1) * dm], n_new, axis=0) for i in range(6)]

    xp = x_prompt.reshape(batch * seq, dm)
    p16, p32 = _front(xp, g1, mods_p[0], mods_p[1], w_t, tile_rows, FRONT_ROWS, seq)
    p32b = p32.reshape(batch, seq, W32)
    k_prompt = p32b[:, :, pool_w:pool_w + kv_w].reshape(1, batch, seq, N_KV_HEADS, HEAD_DIM)
    v_prompt = p32b[:, :, pool_w + kv_w:pool_w + 2 * kv_w].reshape(1, batch, seq, N_KV_HEADS, HEAD_DIM)
    o_small = pool_w + 2 * kv_w
    ki_p = p32b[:, :, o_small:o_small + IDX_DIM]
    wi_p = p32b[:, :, o_small + IDX_DIM:o_small + IDX_DIM + N_IDX_HEADS]
    pool_prompt = p32b[:, seq - POOL_BUF:, :pool_w][None]
    ki16 = ki_p.astype(BF16)
    zeros = jnp.zeros_like(ki16)
    kia = jnp.concatenate([ki16, zeros], axis=-1)
    kib = jnp.concatenate([zeros, ki16], axis=-1)
    wit_p = jnp.swapaxes(wi_p, 1, 2)
    vt_p = jnp.swapaxes(p16.reshape(batch, seq, W16)[:, :, W16 - kv_w:], 1, 2)
    vt_p = jnp.concatenate([vt_p.reshape(batch, N_KV_HEADS, HEAD_DIM, seq),
                            jnp.ones((batch, N_KV_HEADS, V_ROWS - HEAD_DIM, seq), BF16)], axis=2)
    vt_p = vt_p.reshape(batch, N_KV_HEADS * V_ROWS, seq)
    attn_p = _p_attn(p16, wit_p, kia, kib, vt_p, batch, seq, min(TOPK_MAX, seq // 4)).reshape(batch * seq, dm)
    assert seq % ROW_TILE == 0 and ROW_TILE % POOL_HALO == 0 and pool_w == FRONT_TN * 2
    y_prompt = _back(xp, p16, p32, attn_p, mods_p, lw, ROW_TILE, seq, seq // ROW_TILE).reshape(batch, seq, dm)

    xs = x_sample.reshape(nb * n_new, dm)
    s16, s32 = _front(xs, g1, mods_s[0], mods_s[1], w_t, tile_rows, nb * n_new, 0)
    s32b = s32.reshape(nb, n_new, W32)
    k_new = s32b[:, :, pool_w:pool_w + kv_w]
    v_new = s32b[:, :, pool_w + kv_w:pool_w + 2 * kv_w]
    ki_new = s32b[:, :, o_small:o_small + IDX_DIM]
    wi_s = s32b[:, :, o_small + IDX_DIM:o_small + IDX_DIM + N_IDX_HEADS]
    seq_s = jnp.concatenate([state_pool[0], s32b[:, :, :pool_w]], axis=1)
    pool_sample = seq_s[:, n_new:][None]
    d_s = _pool_d_sample(jnp.swapaxes(seq_s, 0, 1), n_new)
    d_s = jnp.swapaxes(d_s, 0, 1).reshape(nb * n_new, pool_w)

    s16b = s16.reshape(nb, n_new, W16)
    hx = N_IDX_HEADS * n_new
    qi_th = s16b[:, :, 3 * dm:3 * dm + idx_w].reshape(nb, hx, IDX_DIM)
    w_th = (wi_s * (IDX_DIM ** -0.5 * N_IDX_HEADS ** -0.5)).reshape(nb, hx, 1)
    ki_new_t = jnp.swapaxes(jnp.pad(ki_new, ((0, 0), (0, page - n_new), (0, 0))), 1, 2)
    cache_ik_t = jnp.swapaxes(cache_idx_k[0], 1, 2)
    scores = _s_score(page_table, qi_th, w_th, ki_new_t, cache_ik_t, n_new)
    topk_s = min(TOPK_MAX, (past + n_new) // 4)
    width = scores.shape[-1]
    sel_t = _s_sel(scores.reshape(nb * n_new, width).T, topk_s)
    sel4 = sel_t.T.reshape(nb, n_new, width)
    sel8 = jnp.concatenate([sel4, sel4], axis=1)

    q_ht = jnp.swapaxes(s16b[:, :, :dm].reshape(nb, n_new, n_heads, HEAD_DIM), 1, 2)
    q_ht = q_ht.reshape(nb, n_heads * n_new, HEAD_DIM)
    pr = page * N_KV_HEADS
    k_new2d = jnp.pad(k_new.astype(BF16), ((0, 0), (0, 8 - n_new), (0, 0))).reshape(nb, 8 * N_KV_HEADS, HEAD_DIM)
    v_new2d = jnp.pad(v_new.astype(BF16), ((0, 0), (0, 8 - n_new), (0, 0))).reshape(nb, 8 * N_KV_HEADS, HEAD_DIM)
    ck2d = cache_k[0].reshape(n_phys, pr, HEAD_DIM)
    cv2d = cache_v[0].reshape(n_phys, pr, HEAD_DIM)
    o_s = _s_attn(page_table, q_ht, sel8, k_new2d, v_new2d, ck2d, cv2d, n_grp, n_new)
    attn_s = jnp.swapaxes(o_s.reshape(nb, n_heads, n_new, HEAD_DIM), 1, 2).reshape(nb * n_new, dm)
    y_sample = _back(xs, s16, d_s, attn_s, mods_s, lw, nb * n_new, 0).reshape(nb, n_new, dm)

    k_sample = k_new.reshape(1, nb, n_new, N_KV_HEADS, HEAD_DIM)
    v_sample = v_new.reshape(1, nb, n_new, N_KV_HEADS, HEAD_DIM)
    return (y_prompt, y_sample, k_prompt, v_prompt, ki_p[None], pool_prompt,
            k_sample, v_sample, ki_new[None], pool_sample)
```

```python
import functools
import math

import jax
import jax.numpy as jnp
from jax import lax
from jax.experimental import pallas as pl
from jax.experimental.pallas import tpu as pltpu

F32 = jnp.float32
BF16 = jnp.bfloat16

RMS_EPS = 1e-6
POOL_WINDOWS = (2, 4, 8, 16)
POOL_BUF = max(POOL_WINDOWS) - 1
HEAD_DIM = 128
N_KV_HEADS = 4
N_IDX_HEADS = 16
IDX_DIM = 64
TOPK_MAX = 256
LANES = 128
NEG = -1e30
Q_SCALE = HEAD_DIM ** -0.5 * math.log2(math.e)
V_ROWS = HEAD_DIM + 16
NT_DIMS = (((1,), (1,)), ((), ()))
VMEM_LIMIT = 56 * 1024 * 1024
FRONT_ROWS = 1024
ROW_TILE = 512


def _cparams(n_axes):
    return pltpu.CompilerParams(dimension_semantics=("arbitrary",) * n_axes, vmem_limit_bytes=VMEM_LIMIT)


def _rms(x, g):
    return x * lax.rsqrt(jnp.mean(x * x, axis=-1, keepdims=True) + RMS_EPS) * g


def _ada_kernel(c_ref, w_ref, b_ref, o_ref):
    c = c_ref[...]
    s = (c * jax.nn.sigmoid(c)).astype(BF16)
    o_ref[...] = jnp.dot(s, w_ref[...].astype(BF16), preferred_element_type=F32) + b_ref[...]


def _ada(c_all, w_ada, b_ada):
    rows, d = c_all.shape
    n = w_ada.shape[1]
    tn = 512
    return pl.pallas_call(
        _ada_kernel,
        out_shape=jax.ShapeDtypeStruct((rows, n), F32),
        grid=(n // tn,),
        in_specs=[pl.BlockSpec((rows, d), lambda j: (0, 0)),
                  pl.BlockSpec((d, tn), lambda j: (0, j)),
                  pl.BlockSpec((1, tn), lambda j: (0, j))],
        out_specs=pl.BlockSpec((rows, tn), lambda j: (0, j)),
        compiler_params=_cparams(1),
        name="ada",
    )(c_all, w_ada, b_ada.reshape(1, n))


FRONT_TN = 512
N_FRONT_TILES = 19
N_Q_TILES = 4
F32_TILE0 = 14
K_TILE, V_TILE = 16, 17
W16 = 16 * FRONT_TN
W32 = 5 * FRONT_TN


def _front_kernel(x_ref, g_ref, sh_ref, sc_ref, wt_ref, o16_ref, o32_ref, u_ref):
    n = pl.program_id(1)

    @pl.when(n == 0)
    def _():
        u_ref[...] = (_rms(x_ref[...], g_ref[...]) * (1.0 + sc_ref[...]) + sh_ref[...]).astype(BF16)

    r = lax.dot_general(u_ref[...], wt_ref[...].astype(BF16), NT_DIMS, preferred_element_type=F32)

    @pl.when((n < F32_TILE0) | (n == K_TILE) | (n == V_TILE))
    def _():
        o16_ref[...] = (r * jnp.where(n < N_Q_TILES, Q_SCALE, 1.0)).astype(BF16)

    @pl.when(n >= F32_TILE0)
    def _():
        o32_ref[...] = r


def _lookup(n, table):
    out = jnp.int32(table[-1])
    for i, v in enumerate(table[:-1]):
        out = jnp.where(n == i, v, out)
    return out


def _o16_block(n):
    return jnp.where(n < F32_TILE0, n,
                     jnp.where(n < K_TILE, F32_TILE0 - 1, jnp.where(n <= V_TILE, n - 2, V_TILE - 2)))


def _mod_spec(mod, tm, rows_per_mod):
    d = mod.shape[-1]
    if mod.ndim == 3:
        tiles_per_batch = rows_per_mod // tm
        return pl.BlockSpec((None, 1, d), lambda i, *_: (i // tiles_per_batch, 0, 0))
    return pl.BlockSpec((tm, d), lambda i, *_: (i, 0))


def _front(x, g1, shift, scale, w_t, tile_rows, tm, rows_per_mod):
    m, d = x.shape
    return pl.pallas_call(
        _front_kernel,
        out_shape=(jax.ShapeDtypeStruct((m, W16), BF16), jax.ShapeDtypeStruct((m, W32), F32)),
        grid=(m // tm, N_FRONT_TILES),
        in_specs=[pl.BlockSpec((tm, d), lambda i, n: (i, 0)),
                  pl.BlockSpec((1, d), lambda i, n: (0, 0)),
                  _mod_spec(shift, tm, rows_per_mod),
                  _mod_spec(scale, tm, rows_per_mod),
                  pl.BlockSpec((pl.Element(FRONT_TN), pl.Element(d)),
                               lambda i, n: (_lookup(n, [r // 8 for r in tile_rows]) * 8, 0))],
        out_specs=(pl.BlockSpec((tm, FRONT_TN), lambda i, n: (i, _o16_block(n))),
                   pl.BlockSpec((tm, FRONT_TN), lambda i, n: (i, jnp.maximum(n - F32_TILE0, 0)))),
        scratch_shapes=[pltpu.VMEM((tm, d), BF16)],
        compiler_params=_cparams(2),
        name="front",
    )(x, g1, shift, scale, w_t)


POOL_HALO = 16


def _pool_d_tile(cur, halo, pos, buf_ref):
    tp, width = cur.shape
    group = width // len(POOL_WINDOWS)
    buf_ref[0:POOL_HALO, :] = halo
    buf_ref[POOL_HALO:POOL_HALO + tp, :] = cur
    out = []
    for gi, w in enumerate(POOL_WINDOWS):
        sl = slice(gi * group, (gi + 1) * group)
        acc = cur[:, sl]
        for j in range(1, w):
            acc = acc + buf_ref[POOL_HALO - j:POOL_HALO - j + tp, sl]
        cnt = jnp.minimum(pos + 1, w).astype(F32)
        out.append((acc / cnt - cur[:, sl]).astype(BF16))
    return out


def _pool_d_sample_kernel(seq_ref, o_ref, *, n_new, group):
    for t in range(n_new):
        r = POOL_BUF + t
        for gi, w in enumerate(POOL_WINDOWS):
            sl = slice(gi * group, (gi + 1) * group)
            acc = seq_ref[r, :, sl]
            for j in range(1, w):
                acc = acc + seq_ref[r - j, :, sl]
            o_ref[t, :, sl] = (acc / float(w) - seq_ref[r, :, sl]).astype(BF16)


def _pool_d_sample(seq_t, n_new):
    rows, nb, width = seq_t.shape
    group = width // len(POOL_WINDOWS)
    return pl.pallas_call(
        functools.partial(_pool_d_sample_kernel, n_new=n_new, group=group),
        out_shape=jax.ShapeDtypeStruct((n_new, nb, width), BF16),
        grid=(1,),
        in_specs=[pl.BlockSpec((rows, nb, width), lambda i: (0, 0, 0))],
        out_specs=pl.BlockSpec((n_new, nb, width), lambda i: (0, 0, 0)),
        compiler_params=_cparams(1),
        name="pool_d_sample",
    )(seq_t)


def _key_to_float(key):
    bits = key ^ ((key >> 31) & jnp.int32(0x7FFFFFFF))
    return lax.bitcast_convert_type(bits, F32)


def _count_cols(sc_ref, n_chunks, chunk, flag):
    n_acc = 4
    lanes = sc_ref.shape[1]

    def chunk_body(c, cnts):
        c0 = pl.multiple_of(c * chunk, chunk)
        s = sc_ref[pl.ds(c0, chunk), :]
        cnts = list(cnts)
        for r in range(chunk // 8):
            cnts[r % n_acc] = cnts[r % n_acc] + flag(s[r * 8:(r + 1) * 8], c0 + r * 8)
        return tuple(cnts)

    cnts = lax.fori_loop(0, n_chunks, chunk_body, tuple(jnp.zeros((8, lanes), jnp.int32) for _ in range(n_acc)))
    return jnp.sum(functools.reduce(jnp.add, cnts), axis=0, keepdims=True)


def _kth_largest_cols(sc_ref, n_chunks, chunk, k):
    int_min = jnp.int32(-(2 ** 31))
    lanes = sc_ref.shape[1]

    def bit_body(it, key):
        cand = key ^ lax.shift_left(jnp.int32(1), 31 - it)
        cf = jnp.broadcast_to(_key_to_float(cand), (8, lanes))
        tot = _count_cols(sc_ref, n_chunks, chunk, lambda s, r0: jnp.where(s >= cf, 1, 0))
        return jnp.where(tot >= k, cand, key)

    key = lax.fori_loop(0, 32, bit_body, jnp.full((1, lanes), int_min, jnp.int32))
    return _key_to_float(key)


def _break_ties_cols(sc_ref, n_chunks, chunk, k, thr, n_rows):
    lanes = sc_ref.shape[1]
    thr8 = jnp.broadcast_to(thr, (8, lanes))
    n_ge = _count_cols(sc_ref, n_chunks, chunk, lambda s, r0: jnp.where(s >= thr8, 1, 0))

    @pl.when(jnp.max(n_ge) > k)
    def _():
        need = k - _count_cols(sc_ref, n_chunks, chunk, lambda s, r0: jnp.where(s > thr8, 1, 0))
        row8 = lax.broadcasted_iota(jnp.int32, (8, lanes), 0)
        n_bits = (n_rows - 1).bit_length()

        def bit_body(it, last):
            cand = last | lax.shift_left(jnp.int32(1), n_bits - 1 - it)
            cand8 = jnp.broadcast_to(cand, (8, lanes))
            before = _count_cols(sc_ref, n_chunks, chunk,
                                 lambda s, r0: jnp.where(s == thr8, jnp.where(row8 + r0 < cand8, 1, 0), 0))
            return jnp.where(before < need, cand, last)

        last = lax.fori_loop(0, n_bits, bit_body, jnp.zeros((1, lanes), jnp.int32))

        def drop(c, carry):
            c0 = pl.multiple_of(c * chunk, chunk)
            s = sc_ref[pl.ds(c0, chunk), :]
            row = c0 + lax.broadcasted_iota(jnp.int32, (chunk, lanes), 0)
            sc_ref[pl.ds(c0, chunk), :] = jnp.where(s == thr, jnp.where(row > last, -jnp.inf, s), s)
            return carry

        lax.fori_loop(0, n_chunks, drop, 0)


PA_CHUNK = 512
PA_QUERIES = 512
PA_QSUB = LANES


def _p_attn_kernel(q_ref, qi_ref, wit_ref, kia_ref, kib_ref, k_ref, vt_ref, o_ref, sc_ref, *, topk):
    nq = PA_QUERIES
    blk = pl.program_id(1)
    t0 = blk * nq
    n_ch = (t0 + nq + PA_CHUNK - 1) // PA_CHUNK
    tok = t0 + lax.broadcasted_iota(jnp.int32, (1, nq), 1)
    n_grp = (q_ref.shape[1] // HEAD_DIM) // N_KV_HEADS
    n_pairs = N_IDX_HEADS // 2

    wit = wit_ref[...] * (IDX_DIM ** -0.5 * N_IDX_HEADS ** -0.5)

    def score_chunk(c, carry):
        c0 = pl.multiple_of(c * PA_CHUNK, PA_CHUNK)
        ka = kia_ref[pl.ds(c0, PA_CHUNK), :]
        kb = kib_ref[pl.ds(c0, PA_CHUNK), :]
        acc = jnp.zeros((PA_CHUNK, nq), F32)
        for pp in range(n_pairs // 2):
            qp = jnp.concatenate([qi_ref[:, (2 * pp + i) * LANES:(2 * pp + i + 1) * LANES] for i in range(2)], axis=0)
            da = lax.dot_general(ka, qp, NT_DIMS, preferred_element_type=F32)
            db = lax.dot_general(kb, qp, NT_DIMS, preferred_element_type=F32)
            for i in range(2):
                h = 2 * (2 * pp + i)
                sl = slice(i * nq, (i + 1) * nq)
                acc = acc + jnp.maximum(da[:, sl], 0.0) * wit[h:h + 1, :]
                acc = acc + jnp.maximum(db[:, sl], 0.0) * wit[h + 1:h + 2, :]
        keypos = c0 + lax.broadcasted_iota(jnp.int32, (PA_CHUNK, nq), 0)
        sc_ref[pl.ds(c0, PA_CHUNK), :] = jnp.where(keypos <= tok, acc, -jnp.inf)
        return carry

    lax.fori_loop(0, n_ch, score_chunk, 0)

    thr = _kth_largest_cols(sc_ref, n_ch, PA_CHUNK, topk)
    thr = jnp.where(tok + 1 <= topk, jnp.finfo(F32).min, thr)
    _break_ties_cols(sc_ref, n_ch, PA_CHUNK, topk, thr, sc_ref.shape[0])

    items = [(j, u) for j in range(N_KV_HEADS) for u in range(nq // PA_QSUB)]
    width = n_grp * PA_QSUB

    def attn_chunk(c, carry):
        c0 = pl.multiple_of(c * PA_CHUNK, PA_CHUNK)
        biases = []
        for u in range(nq // PA_QSUB):
            qs = slice(u * PA_QSUB, (u + 1) * PA_QSUB)
            bias = jnp.where(sc_ref[pl.ds(c0, PA_CHUNK), qs] >= thr[:, qs], 0.0, NEG)
            biases.append(jnp.concatenate([bias] * n_grp, axis=1))

        def logits(i):
            j, u = items[i]
            qj = jnp.concatenate([q_ref[u * PA_QSUB:(u + 1) * PA_QSUB,
                                        (j * n_grp + g) * HEAD_DIM:(j * n_grp + g + 1) * HEAD_DIM]
                                  for g in range(n_grp)], axis=0)
            kc = k_ref[pl.ds(c0, PA_CHUNK), j * HEAD_DIM:(j + 1) * HEAD_DIM]
            return lax.dot_general(kc, qj, NT_DIMS, preferred_element_type=F32) + biases[u]

        def weights(i, s):
            m = carry[i][0]
            m_new = jnp.maximum(m, jnp.max(s, axis=0, keepdims=True))
            return m_new, jnp.exp2(m - m_new), jnp.exp2(s - m_new).astype(BF16)

        def accumulate(i, alpha, p):
            j = items[i][0]
            vt = vt_ref[j * V_ROWS:(j + 1) * V_ROWS, pl.ds(c0, PA_CHUNK)]
            return alpha * carry[i][1] + jnp.dot(vt, p, preferred_element_type=F32)

        s, w, out = {}, {}, []
        for step in range(len(items) + 2):
            if step < len(items):
                s[step] = logits(step)
            if 0 <= step - 1 < len(items):
                w[step - 1] = weights(step - 1, s.pop(step - 1))
            if 0 <= step - 2 < len(items):
                m_new, alpha, p = w.pop(step - 2)
                out.append((m_new, accumulate(step - 2, alpha, p)))
        return tuple(out)

    init = tuple((jnp.full((1, width), NEG, F32), jnp.zeros((V_ROWS, width), F32)) for _ in items)
    final = lax.fori_loop(0, n_ch, attn_chunk, init)
    for (j, u), (_, acc) in zip(items, final):
        o_t = acc[:HEAD_DIM] * (1.0 / acc[HEAD_DIM:HEAD_DIM + 1])
        for g in range(n_grp):
            h = j * n_grp + g
            o_ref[u * PA_QSUB:(u + 1) * PA_QSUB, h * HEAD_DIM:(h + 1) * HEAD_DIM] = (
                o_t[:, g * PA_QSUB:(g + 1) * PA_QSUB].T.astype(BF16))


def _p_attn(p16, wit, kia, kib, vt, batch, seq, topk, d_attn):
    p16b = p16.reshape(batch, seq, W16)
    kv_w = N_KV_HEADS * HEAD_DIM
    idx_w = N_IDX_HEADS * IDX_DIM
    o_idx_q = 3 * d_attn
    o_k = o_idx_q + idx_w
    assert o_idx_q % idx_w == 0 and o_k % kv_w == 0 and o_k + 2 * kv_w == W16
    return pl.pallas_call(
        functools.partial(_p_attn_kernel, topk=topk),
        out_shape=jax.ShapeDtypeStruct((batch, seq, d_attn), BF16),
        grid=(batch, seq // PA_QUERIES),
        in_specs=[pl.BlockSpec((None, PA_QUERIES, d_attn), lambda b, i: (b, i, 0)),
                  pl.BlockSpec((None, PA_QUERIES, idx_w), lambda b, i: (b, i, o_idx_q // idx_w)),
                  pl.BlockSpec((None, N_IDX_HEADS, PA_QUERIES), lambda b, i: (b, 0, i)),
                  pl.BlockSpec((None, seq, LANES), lambda b, i: (b, 0, 0)),
                  pl.BlockSpec((None, seq, LANES), lambda b, i: (b, 0, 0)),
                  pl.BlockSpec((None, seq, kv_w), lambda b, i: (b, 0, o_k // kv_w)),
                  pl.BlockSpec((None, N_KV_HEADS * V_ROWS, seq), lambda b, i: (b, 0, 0))],
        out_specs=pl.BlockSpec((None, PA_QUERIES, d_attn), lambda b, i: (b, i, 0)),
        scratch_shapes=[pltpu.VMEM((seq, PA_QUERIES), F32)],
        compiler_params=_cparams(2),
        name="p_attn",
    )(p16b, p16b, wit, kia, kib, p16b, vt)


S_SCORE_GROUP = 8


def _s_score_kernel(pt_ref, qi_ref, w_ref, kin_ref, *rest, n_pages, page, n_new, group):
    del pt_ref
    ik_refs, o_ref = rest[:group * n_pages], rest[group * n_pages]
    for g in range(group):
        qi = qi_ref[g]
        w = w_ref[g]

        def piece(ik_t, qi=qi, w=w):
            d = jnp.dot(qi, ik_t.astype(BF16), preferred_element_type=F32)
            x = jnp.maximum(d, 0.0) * w
            return jnp.concatenate([jnp.sum(x[t * N_IDX_HEADS:(t + 1) * N_IDX_HEADS], axis=0, keepdims=True)
                                    for t in range(n_new)], axis=0)

        for p in range(n_pages):
            o_ref[g, :, p * page:(p + 1) * page] = piece(ik_refs[g * n_pages + p][...])
        new = piece(kin_ref[g])
        col = lax.broadcasted_iota(jnp.int32, new.shape, 1)
        tok = lax.broadcasted_iota(jnp.int32, new.shape, 0)
        o_ref[g, :, n_pages * page:(n_pages + 1) * page] = jnp.where(col <= tok, new, -jnp.inf)


def _page_specs(n_pages, shape, group=1):
    zeros = (0,) * len(shape)
    return [pl.BlockSpec((None,) + shape,
                         functools.partial(lambda b, pt, g, p: (pt[b * group + g, p],) + zeros, g=g, p=p))
            for g in range(group) for p in range(n_pages)]


def _s_score(page_table, qi_th, w_th, ki_new_t, cache_ik_t, n_new):
    nb, n_pages = page_table.shape
    page = cache_ik_t.shape[2]
    width = (n_pages + 1) * page
    hx = qi_th.shape[1]
    group = S_SCORE_GROUP
    assert nb % group == 0
    return pl.pallas_call(
        functools.partial(_s_score_kernel, n_pages=n_pages, page=page, n_new=n_new, group=group),
        out_shape=jax.ShapeDtypeStruct((nb, n_new, width), F32),
        grid_spec=pltpu.PrefetchScalarGridSpec(
            num_scalar_prefetch=1,
            grid=(nb // group,),
            in_specs=[pl.BlockSpec((group, hx, IDX_DIM), lambda b, pt: (b, 0, 0)),
                      pl.BlockSpec((group, hx, 1), lambda b, pt: (b, 0, 0)),
                      pl.BlockSpec((group, IDX_DIM, page), lambda b, pt: (b, 0, 0))]
            + _page_specs(n_pages, (IDX_DIM, page), group),
            out_specs=pl.BlockSpec((group, n_new, width), lambda b, pt: (b, 0, 0))),
        compiler_params=_cparams(1),
        name="s_score",
    )(page_table, qi_th, w_th, ki_new_t, *([cache_ik_t] * (group * n_pages)))


def _s_sel_kernel(sc_ref, o_ref, buf_ref, *, topk, chunk):
    n_rows = sc_ref.shape[0]
    buf_ref[...] = sc_ref[...]
    thr = _kth_largest_cols(buf_ref, n_rows // chunk, chunk, topk)
    _break_ties_cols(buf_ref, n_rows // chunk, chunk, topk, thr, n_rows)
    o_ref[...] = jnp.where(buf_ref[...] >= thr, 1.0, 0.0)


def _s_sel(scores_t, topk):
    n_rows, n_q = scores_t.shape
    return pl.pallas_call(
        functools.partial(_s_sel_kernel, topk=topk, chunk=LANES),
        out_shape=jax.ShapeDtypeStruct((n_rows, n_q), F32),
        grid=(n_q // LANES,),
        in_specs=[pl.BlockSpec((n_rows, LANES), lambda i: (0, i))],
        out_specs=pl.BlockSpec((n_rows, LANES), lambda i: (0, i)),
        scratch_shapes=[pltpu.VMEM((n_rows, LANES), F32)],
        compiler_params=_cparams(1),
        name="s_sel",
    )(scores_t)


def _s_attn_kernel(pt_ref, q_ref, sel_ref, kn_ref, vn_ref, *rest, n_pages, n_grp, n_new, group):
    del pt_ref
    gp = group * n_pages
    k_refs, v_refs, o_ref = rest[:gp], rest[gp:2 * gp], rest[2 * gp]
    rows = q_ref.shape[1]
    pr = k_refs[0].shape[0]
    page = pr // N_KV_HEADS
    spread = (lax.broadcasted_iota(jnp.int32, (page, pr), 1) // N_KV_HEADS
              == lax.broadcasted_iota(jnp.int32, (page, pr), 0)).astype(BF16)
    lane_kv = lax.broadcasted_iota(jnp.int32, (rows, pr), 1) % N_KV_HEADS
    row_kv = lax.broadcasted_iota(jnp.int32, (rows, pr), 0) // (n_grp * n_new)
    head_bias = jnp.where(lane_kv == row_kv, 0.0, NEG)
    for g in range(group):
        q = q_ref[g]
        sel = sel_ref[g].astype(BF16)
        sel = jnp.concatenate([sel[:, p * page:(p + 1) * page] for p in range(n_pages + 1)], axis=0)
        sel = jnp.dot(sel, spread, preferred_element_type=F32)
        pieces = []
        for p in range(n_pages + 1):
            kp = kn_ref[g] if p == n_pages else k_refs[g * n_pages + p][...].astype(BF16)
            w = kp.shape[0]
            s = lax.dot_general(q, kp, NT_DIMS, preferred_element_type=F32)
            bias = jnp.concatenate([(sel[p * 8:(p + 1) * 8, :w] - 1.0) * -NEG] * (rows // 8), axis=0)
            pieces.append(s + (bias + head_bias[:, :w]))
        m = functools.reduce(jnp.maximum, [jnp.max(s, axis=1, keepdims=True) for s in pieces])
        l = jnp.zeros((rows, 1), F32)
        o = jnp.zeros((rows, HEAD_DIM), F32)
        for p in range(n_pages + 1):
            e = jnp.exp2(pieces[p] - m)
            l = l + jnp.sum(e, axis=1, keepdims=True)
            vp = vn_ref[g] if p == n_pages else v_refs[g * n_pages + p][...].astype(BF16)
            o = o + jnp.dot(e.astype(BF16), vp, preferred_element_type=F32)
        o_ref[g] = (o / l).astype(BF16)


S_ATTN_GROUP = 2


def _s_attn(page_table, q_ht, sel8, k_new2d, v_new2d, cache_k2d, cache_v2d, n_grp, n_new):
    nb, n_pages = page_table.shape
    pr = cache_k2d.shape[1]
    pr_new = k_new2d.shape[1]
    rows = q_ht.shape[1]
    width = sel8.shape[-1]
    group = S_ATTN_GROUP
    assert nb % group == 0
    return pl.pallas_call(
        functools.partial(_s_attn_kernel, n_pages=n_pages, n_grp=n_grp, n_new=n_new, group=group),
        out_shape=jax.ShapeDtypeStruct((nb, rows, HEAD_DIM), BF16),
        grid_spec=pltpu.PrefetchScalarGridSpec(
            num_scalar_prefetch=1,
            grid=(nb // group,),
            in_specs=[pl.BlockSpec((group, rows, HEAD_DIM), lambda b, pt: (b, 0, 0)),
                      pl.BlockSpec((group, 8, width), lambda b, pt: (b, 0, 0)),
                      pl.BlockSpec((group, pr_new, HEAD_DIM), lambda b, pt: (b, 0, 0)),
                      pl.BlockSpec((group, pr_new, HEAD_DIM), lambda b, pt: (b, 0, 0))]
            + _page_specs(n_pages, (pr, HEAD_DIM), group) + _page_specs(n_pages, (pr, HEAD_DIM), group),
            out_specs=pl.BlockSpec((group, rows, HEAD_DIM), lambda b, pt: (b, 0, 0))),
        compiler_params=_cparams(1),
        name="s_attn",
    )(page_table, q_ht, sel8, k_new2d, v_new2d,
      *([cache_k2d] * (group * n_pages)), *([cache_v2d] * (group * n_pages)))


def _mix_kernel(*refs, tiles_per_seq):
    if tiles_per_seq is None:
        d_ref, a_ref, ga_ref, gb_ref, wg_ref, ps_ref, wup_ref, wua_ref, o_ref = refs
    else:
        pin_ref, halo_ref, a_ref, ga_ref, gb_ref, wg_ref, ps_ref, wup_ref, wua_ref, o_ref, buf_ref = refs
    ya = jnp.dot(a_ref[...], wua_ref[...], preferred_element_type=F32)
    if tiles_per_seq is None:
        group = wg_ref.shape[1]
        d = [d_ref[:, g * group:(g + 1) * group] for g in range(wg_ref.shape[0])]
    else:
        t = pl.program_id(0) % tiles_per_seq
        tm = pin_ref.shape[0]
        halo = jnp.where(t > 0, halo_ref[...], 0.0)
        pos = t * tm + lax.broadcasted_iota(jnp.int32, (tm, 1), 0)
        d = _pool_d_tile(pin_ref[...], halo, pos, buf_ref)
    y = jnp.concatenate([jnp.dot(d[g], wg_ref[g], preferred_element_type=F32) for g in range(len(d))],
                        axis=1) * ps_ref[...]
    yp = jnp.dot(y.astype(BF16), wup_ref[...], preferred_element_type=F32)
    o_ref[...] = (jax.nn.sigmoid(ga_ref[...].astype(F32)) * yp
                  + jax.nn.sigmoid(gb_ref[...].astype(F32)) * ya).astype(BF16)


def _mix(pool_in, attn, p16, w_grp, pool_scale, w_up_pool, w_up_attn, tm, tiles_per_seq=None):
    m, dm = attn.shape
    pw = pool_scale.shape[1]
    const = lambda *shape: pl.BlockSpec(shape, lambda i: (0,) * len(shape))
    if tiles_per_seq is None:
        pool_specs, pool_args, scratch = [pl.BlockSpec((tm, pw), lambda i: (i, 0))], [pool_in], []
    else:
        halo_blocks = tm // POOL_HALO
        pool_specs = [pl.BlockSpec((tm, pw), lambda i: (i, 0)),
                      pl.BlockSpec((POOL_HALO, pw), lambda i: (jnp.maximum(i * halo_blocks - 1, 0), 0))]
        pool_args, scratch = [pool_in, pool_in], [pltpu.VMEM((tm + POOL_HALO, pw), F32)]
    return pl.pallas_call(
        functools.partial(_mix_kernel, tiles_per_seq=tiles_per_seq),
        out_shape=jax.ShapeDtypeStruct((m, dm), BF16),
        grid=(m // tm,),
        in_specs=pool_specs + [
            pl.BlockSpec((tm, dm), lambda i: (i, 0)),
            pl.BlockSpec((tm, dm), lambda i: (i, 1)),
            pl.BlockSpec((tm, dm), lambda i: (i, 2)),
            const(*w_grp.shape), const(1, pw), const(*w_up_pool.shape), const(*w_up_attn.shape)],
        out_specs=pl.BlockSpec((tm, dm), lambda i: (i, 0)),
        scratch_shapes=scratch,
        compiler_params=_cparams(1),
        name="mix",
    )(*pool_args, attn, p16, p16, w_grp, pool_scale, w_up_pool, w_up_attn)


def _out_kernel(mix_ref, x_ref, gate_ref, sh_ref, sc_ref, g_ref, w_ref, h_ref, hn_ref):
    h = x_ref[...] + gate_ref[...] * jnp.dot(mix_ref[...], w_ref[...], preferred_element_type=F32)
    h_ref[...] = h
    hn_ref[...] = (_rms(h, g_ref[...]) * (1.0 + sc_ref[...]) + sh_ref[...]).astype(BF16)


def _out(mix, x, gate, shift, scale, g2, w_out, tm, rows_per_mod):
    m, d = x.shape
    return pl.pallas_call(
        _out_kernel,
        out_shape=(jax.ShapeDtypeStruct((m, d), F32), jax.ShapeDtypeStruct((m, d), BF16)),
        grid=(m // tm,),
        in_specs=[pl.BlockSpec((tm, d), lambda i: (i, 0)),
                  pl.BlockSpec((tm, d), lambda i: (i, 0)),
                  _mod_spec(gate, tm, rows_per_mod),
                  _mod_spec(shift, tm, rows_per_mod),
                  _mod_spec(scale, tm, rows_per_mod),
                  pl.BlockSpec((1, d), lambda i: (0, 0)),
                  pl.BlockSpec((d, d), lambda i: (0, 0))],
        out_specs=(pl.BlockSpec((tm, d), lambda i: (i, 0)), pl.BlockSpec((tm, d), lambda i: (i, 0))),
        compiler_params=_cparams(1),
        name="out",
    )(mix, x, gate, shift, scale, g2, w_out)


FFN_TF = 512


def _ffn_kernel(hn_ref, h_ref, gate_ref, gf_ref, wg_ref, wu_ref, wo_ref, o_ref, acc_ref):
    f = pl.program_id(1)

    @pl.when(f == 0)
    def _():
        acc_ref[...] = jnp.zeros_like(acc_ref)

    hn = hn_ref[...]
    a = jnp.dot(hn, wg_ref[...], preferred_element_type=F32)
    u = jnp.dot(hn, wu_ref[...], preferred_element_type=F32)
    z = (a * jax.nn.sigmoid(a) * u).astype(BF16)
    acc_ref[...] += jnp.dot(z, wo_ref[...], preferred_element_type=F32)

    @pl.when(f == pl.num_programs(1) - 1)
    def _():
        o_ref[...] = _rms(h_ref[...] + gate_ref[...] * acc_ref[...], gf_ref[...])


def _ffn(hn, h, gate, g_final, w_ffn_in, w_ffn_out, tm, rows_per_mod):
    m, d = h.shape
    d_ff = w_ffn_out.shape[0]
    n_f = d_ff // FFN_TF
    return pl.pallas_call(
        _ffn_kernel,
        out_shape=jax.ShapeDtypeStruct((m, d), F32),
        grid=(m // tm, n_f),
        in_specs=[pl.BlockSpec((tm, d), lambda i, f: (i, 0)),
                  pl.BlockSpec((tm, d), lambda i, f: (i, 0)),
                  _mod_spec(gate, tm, rows_per_mod),
                  pl.BlockSpec((1, d), lambda i, f: (0, 0)),
                  pl.BlockSpec((d, FFN_TF), lambda i, f: (0, f)),
                  pl.BlockSpec((d, FFN_TF), lambda i, f: (0, f + n_f)),
                  pl.BlockSpec((FFN_TF, d), lambda i, f: (f, 0))],
        out_specs=pl.BlockSpec((tm, d), lambda i, f: (i, 0)),
        scratch_shapes=[pltpu.VMEM((tm, d), F32)],
        compiler_params=_cparams(2),
        name="ffn",
    )(hn, h, gate, g_final, w_ffn_in, w_ffn_in, w_ffn_out)


def _back(x2, p16, pool_in, attn, mods, lw, tm, rows_per_mod, tiles_per_seq=None):
    mix = _mix(pool_in, attn, p16, lw["w_grp"], lw["pool_scale"], lw["w_up_pool"], lw["w_up_attn"], tm,
               tiles_per_seq)
    h, hn = _out(mix, x2, mods[2], mods[3], mods[4], lw["g2"], lw["w_out"], tm, rows_per_mod)
    return _ffn(hn, h, mods[5], lw["g_final"], lw["w_ffn_in"], lw["w_ffn_out"], tm, rows_per_mod)


def kernel(x_prompt, x_sample, cache_k, cache_v, cache_idx_k, state_pool, page_table, c_prompt, c_sample,
           w_ada, b_ada, g_norm1, w_in, w_pool_grp, pool_scale, w_up_pool, w_up_attn, w_out, g_norm2,
           w_ffn_in, w_ffn_out, g_final):
    batch, seq, dm = x_prompt.shape
    nb, n_new, _ = x_sample.shape
    depth = w_ada.shape[0]
    assert depth == 1, "single-layer step"
    n_phys, page = cache_k.shape[1], cache_k.shape[2]
    n_pages = page_table.shape[1]
    past = n_pages * page
    pool_w = state_pool.shape[-1]
    kv_w = N_KV_HEADS * HEAD_DIM
    n_heads = dm // HEAD_DIM
    n_grp = n_heads // N_KV_HEADS
    idx_w = N_IDX_HEADS * IDX_DIM
    assert page == LANES and n_new == 4 and seq % PA_CHUNK == 0

    w_t = jnp.swapaxes(w_in[0], 0, 1)
    assert (pool_w, dm, kv_w, idx_w) == (2 * FRONT_TN, 4 * FRONT_TN, FRONT_TN, 2 * FRONT_TN)
    o_q, o_k, o_v, o_qi = pool_w, pool_w + dm, pool_w + dm + kv_w, pool_w + dm + 2 * kv_w
    o_ki = o_qi + idx_w
    o_ga = o_ki + IDX_DIM + N_IDX_HEADS
    o_gb = o_ga + dm
    assert o_gb + dm == w_t.shape[0] and o_ki + FRONT_TN <= w_t.shape[0]
    tile_rows = ([o_q + i * FRONT_TN for i in range(4)] + [o_ga + i * FRONT_TN for i in range(4)]
                 + [o_gb + i * FRONT_TN for i in range(4)] + [o_qi, o_qi + FRONT_TN, 0, FRONT_TN, o_k, o_v, o_ki])
    assert len(tile_rows) == N_FRONT_TILES and all(r % 8 == 0 for r in tile_rows)
    lw = {"w_grp": w_pool_grp[0].astype(BF16), "pool_scale": pool_scale[0].reshape(1, pool_w),
          "w_up_pool": w_up_pool[0].astype(BF16), "w_up_attn": w_up_attn[0].astype(BF16),
          "w_out": w_out[0].astype(BF16), "g2": g_norm2[0].reshape(1, dm), "g_final": g_final.reshape(1, dm),
          "w_ffn_in": w_ffn_in[0].astype(BF16), "w_ffn_out": w_ffn_out[0].astype(BF16)}
    g1 = g_norm1[0].reshape(1, dm)

    n_c = batch + nb
    c_all = jnp.concatenate([c_prompt, c_sample, jnp.zeros((-n_c % 8, dm), F32)], axis=0)
    mod = _ada(c_all, w_ada[0], b_ada[0])
    mods_p = [mod[:batch, i * dm:(i + 1) * dm].reshape(batch, 1, dm) for i in range(6)]
    mods_s = [jnp.repeat(mod[batch:n_c, i * dm:(i + 1) * dm], n_new, axis=0) for i in range(6)]

    xp = x_prompt.reshape(batch * seq, dm)
    p16, p32 = _front(xp, g1, mods_p[0], mods_p[1], w_t, tile_rows, FRONT_ROWS, seq)
    p32b = p32.reshape(batch, seq, W32)
    k_prompt = p32b[:, :, pool_w:pool_w + kv_w].reshape(1, batch, seq, N_KV_HEADS, HEAD_DIM)
    v_prompt = p32b[:, :, pool_w + kv_w:pool_w + 2 * kv_w].reshape(1, batch, seq, N_KV_HEADS, HEAD_DIM)
    o_small = pool_w + 2 * kv_w
    ki_p = p32b[:, :, o_small:o_small + IDX_DIM]
    wi_p = p32b[:, :, o_small + IDX_DIM:o_small + IDX_DIM + N_IDX_HEADS]
    pool_prompt = p32b[:, seq - POOL_BUF:, :pool_w][None]
    ki16 = ki_p.astype(BF16)
    zeros = jnp.zeros_like(ki16)
    kia = jnp.concatenate([ki16, zeros], axis=-1)
    kib = jnp.concatenate([zeros, ki16], axis=-1)
    wit_p = jnp.swapaxes(wi_p, 1, 2)
    vt_p = jnp.swapaxes(p16.reshape(batch, seq, W16)[:, :, W16 - kv_w:], 1, 2)
    vt_p = jnp.concatenate([vt_p.reshape(batch, N_KV_HEADS, HEAD_DIM, seq),
                            jnp.ones((batch, N_KV_HEADS, V_ROWS - HEAD_DIM, seq), BF16)], axis=2)
    vt_p = vt_p.reshape(batch, N_KV_HEADS * V_ROWS, seq)
    attn_p = _p_attn(p16, wit_p, kia, kib, vt_p, batch, seq, min(TOPK_MAX, seq // 4), dm).reshape(batch * seq, dm)
    assert seq % ROW_TILE == 0 and ROW_TILE % POOL_HALO == 0 and pool_w == FRONT_TN * 2
    y_prompt = _back(xp, p16, p32, attn_p, mods_p, lw, ROW_TILE, seq, seq // ROW_TILE).reshape(batch, seq, dm)

    xs = x_sample.reshape(nb * n_new, dm)
    s16, s32 = _front(xs, g1, mods_s[0], mods_s[1], w_t, tile_rows, nb * n_new, 0)
    s32b = s32.reshape(nb, n_new, W32)
    k_new = s32b[:, :, pool_w:pool_w + kv_w]
    v_new = s32b[:, :, pool_w + kv_w:pool_w + 2 * kv_w]
    ki_new = s32b[:, :, o_small:o_small + IDX_DIM]
    wi_s = s32b[:, :, o_small + IDX_DIM:o_small + IDX_DIM + N_IDX_HEADS]
    seq_s = jnp.concatenate([state_pool[0], s32b[:, :, :pool_w]], axis=1)
    pool_sample = seq_s[:, n_new:][None]
    d_s = _pool_d_sample(jnp.swapaxes(seq_s, 0, 1), n_new)
    d_s = jnp.swapaxes(d_s, 0, 1).reshape(nb * n_new, pool_w)

    s16b = s16.reshape(nb, n_new, W16)
    hx = N_IDX_HEADS * n_new
    qi_th = s16b[:, :, 3 * dm:3 * dm + idx_w].reshape(nb, hx, IDX_DIM)
    w_th = (wi_s * (IDX_DIM ** -0.5 * N_IDX_HEADS ** -0.5)).reshape(nb, hx, 1)
    ki_new_t = jnp.swapaxes(jnp.pad(ki_new, ((0, 0), (0, page - n_new), (0, 0))), 1, 2)
    cache_ik_t = jnp.swapaxes(cache_idx_k[0], 1, 2)
    scores = _s_score(page_table, qi_th, w_th, ki_new_t, cache_ik_t, n_new)
    topk_s = min(TOPK_MAX, (past + n_new) // 4)
    width = scores.shape[-1]
    sel_t = _s_sel(scores.reshape(nb * n_new, width).T, topk_s)
    sel4 = sel_t.T.reshape(nb, n_new, width)
    sel8 = jnp.concatenate([sel4, sel4], axis=1)

    q_ht = jnp.swapaxes(s16b[:, :, :dm].reshape(nb, n_new, n_heads, HEAD_DIM), 1, 2)
    q_ht = q_ht.reshape(nb, n_heads * n_new, HEAD_DIM)
    pr = page * N_KV_HEADS
    k_new2d = jnp.pad(k_new.astype(BF16), ((0, 0), (0, 8 - n_new), (0, 0))).reshape(nb, 8 * N_KV_HEADS, HEAD_DIM)
    v_new2d = jnp.pad(v_new.astype(BF16), ((0, 0), (0, 8 - n_new), (0, 0))).reshape(nb, 8 * N_KV_HEADS, HEAD_DIM)
    ck2d = cache_k[0].reshape(n_phys, pr, HEAD_DIM)
    cv2d = cache_v[0].reshape(n_phys, pr, HEAD_DIM)
    o_s = _s_attn(page_table, q_ht, sel8, k_new2d, v_new2d, ck2d, cv2d, n_grp, n_new)
    attn_s = jnp.swapaxes(o_s.reshape(nb, n_heads, n_new, HEAD_DIM), 1, 2).reshape(nb * n_new, dm)
    y_sample = _back(xs, s16, d_s, attn_s, mods_s, lw, nb * n_new, 0).reshape(nb, n_new, dm)

    k_sample = k_new.reshape(1, nb, n_new, N_KV_HEADS, HEAD_DIM)
    v_sample = v_new.reshape(1, nb, n_new, N_KV_HEADS, HEAD_DIM)
    return (y_prompt, y_sample, k_prompt, v_prompt, ki_p[None], pool_prompt,
            k_sample, v_sample, ki_new[None], pool_sample)
```

```python
import functools
import math

import jax
import jax.numpy as jnp
from jax import lax
from jax.experimental import pallas as pl
from jax.experimental.pallas import tpu as pltpu

F32 = jnp.float32
BF16 = jnp.bfloat16

RMS_EPS = 1e-6
POOL_WINDOWS = (2, 4, 8, 16)
POOL_BUF = max(POOL_WINDOWS) - 1
HEAD_DIM = 128
N_KV_HEADS = 4
N_IDX_HEADS = 16
IDX_DIM = 64
TOPK_MAX = 256
LANES = 128
NEG = -1e30
Q_SCALE = HEAD_DIM ** -0.5 * math.log2(math.e)
V_ROWS = HEAD_DIM + 16
NT_DIMS = (((1,), (1,)), ((), ()))
VMEM_LIMIT = 56 * 1024 * 1024
FRONT_ROWS = 1024
ROW_TILE = 512


def _cparams(n_axes):
    return pltpu.CompilerParams(dimension_semantics=("arbitrary",) * n_axes, vmem_limit_bytes=VMEM_LIMIT)


def _rms(x, g):
    return x * lax.rsqrt(jnp.mean(x * x, axis=-1, keepdims=True) + RMS_EPS) * g


def _ada_kernel(c_ref, w_ref, b_ref, o_ref):
    c = c_ref[...]
    s = (c * jax.nn.sigmoid(c)).astype(BF16)
    o_ref[...] = jnp.dot(s, w_ref[...].astype(BF16), preferred_element_type=F32) + b_ref[...]


def _ada(c_all, w_ada, b_ada):
    rows, d = c_all.shape
    n = w_ada.shape[1]
    tn = 512
    return pl.pallas_call(
        _ada_kernel,
        out_shape=jax.ShapeDtypeStruct((rows, n), F32),
        grid=(n // tn,),
        in_specs=[pl.BlockSpec((rows, d), lambda j: (0, 0)),
                  pl.BlockSpec((d, tn), lambda j: (0, j)),
                  pl.BlockSpec((1, tn), lambda j: (0, j))],
        out_specs=pl.BlockSpec((rows, tn), lambda j: (0, j)),
        compiler_params=_cparams(1),
        name="ada",
    )(c_all, w_ada, b_ada.reshape(1, n))


FRONT_TN = 512
N_FRONT_TILES = 19
N_Q_TILES = 4
F32_TILE0 = 14
K_TILE, V_TILE = 16, 17
W16 = 16 * FRONT_TN
W32 = 5 * FRONT_TN


def _front_kernel(x_ref, g_ref, sh_ref, sc_ref, wt_ref, o16_ref, o32_ref, u_ref):
    n = pl.program_id(1)

    @pl.when(n == 0)
    def _():
        u_ref[...] = (_rms(x_ref[...], g_ref[...]) * (1.0 + sc_ref[...]) + sh_ref[...]).astype(BF16)

    r = lax.dot_general(u_ref[...], wt_ref[...].astype(BF16), NT_DIMS, preferred_element_type=F32)

    @pl.when((n < F32_TILE0) | (n == K_TILE) | (n == V_TILE))
    def _():
        o16_ref[...] = (r * jnp.where(n < N_Q_TILES, Q_SCALE, 1.0)).astype(BF16)

    @pl.when(n >= F32_TILE0)
    def _():
        o32_ref[...] = r


def _lookup(n, table):
    out = jnp.int32(table[-1])
    for i, v in enumerate(table[:-1]):
        out = jnp.where(n == i, v, out)
    return out


def _o16_block(n):
    return jnp.where(n < F32_TILE0, n,
                     jnp.where(n < K_TILE, F32_TILE0 - 1, jnp.where(n <= V_TILE, n - 2, V_TILE - 2)))


def _mod_spec(mod, tm, rows_per_mod):
    d = mod.shape[-1]
    if mod.ndim == 3:
        tiles_per_batch = rows_per_mod // tm
        return pl.BlockSpec((None, 1, d), lambda i, *_: (i // tiles_per_batch, 0, 0))
    return pl.BlockSpec((tm, d), lambda i, *_: (i, 0))


def _front(x, g1, shift, scale, w_t, tile_rows, tm, rows_per_mod):
    m, d = x.shape
    return pl.pallas_call(
        _front_kernel,
        out_shape=(jax.ShapeDtypeStruct((m, W16), BF16), jax.ShapeDtypeStruct((m, W32), F32)),
        grid=(m // tm, N_FRONT_TILES),
        in_specs=[pl.BlockSpec((tm, d), lambda i, n: (i, 0)),
                  pl.BlockSpec((1, d), lambda i, n: (0, 0)),
                  _mod_spec(shift, tm, rows_per_mod),
                  _mod_spec(scale, tm, rows_per_mod),
                  pl.BlockSpec((pl.Element(FRONT_TN), pl.Element(d)),
                               lambda i, n: (_lookup(n, [r // 8 for r in tile_rows]) * 8, 0))],
        out_specs=(pl.BlockSpec((tm, FRONT_TN), lambda i, n: (i, _o16_block(n))),
                   pl.BlockSpec((tm, FRONT_TN), lambda i, n: (i, jnp.maximum(n - F32_TILE0, 0)))),
        scratch_shapes=[pltpu.VMEM((tm, d), BF16)],
        compiler_params=_cparams(2),
        name="front",
    )(x, g1, shift, scale, w_t)


POOL_HALO = 16


def _pool_d_tile(cur, halo, pos, buf_ref):
    tp, width = cur.shape
    group = width // len(POOL_WINDOWS)
    buf_ref[0:POOL_HALO, :] = halo
    buf_ref[POOL_HALO:POOL_HALO + tp, :] = cur
    out = []
    for gi, w in enumerate(POOL_WINDOWS):
        sl = slice(gi * group, (gi + 1) * group)
        acc = cur[:, sl]
        for j in range(1, w):
            acc = acc + buf_ref[POOL_HALO - j:POOL_HALO - j + tp, sl]
        cnt = jnp.minimum(pos + 1, w).astype(F32)
        out.append((acc / cnt - cur[:, sl]).astype(BF16))
    return out


def _pool_d_sample_kernel(seq_ref, o_ref, *, n_new, group):
    for t in range(n_new):
        r = POOL_BUF + t
        for gi, w in enumerate(POOL_WINDOWS):
            sl = slice(gi * group, (gi + 1) * group)
            acc = seq_ref[r, :, sl]
            for j in range(1, w):
                acc = acc + seq_ref[r - j, :, sl]
            o_ref[t, :, sl] = (acc / float(w) - seq_ref[r, :, sl]).astype(BF16)


def _pool_d_sample(seq_t, n_new):
    rows, nb, width = seq_t.shape
    group = width // len(POOL_WINDOWS)
    return pl.pallas_call(
        functools.partial(_pool_d_sample_kernel, n_new=n_new, group=group),
        out_shape=jax.ShapeDtypeStruct((n_new, nb, width), BF16),
        grid=(1,),
        in_specs=[pl.BlockSpec((rows, nb, width), lambda i: (0, 0, 0))],
        out_specs=pl.BlockSpec((n_new, nb, width), lambda i: (0, 0, 0)),
        compiler_params=_cparams(1),
        name="pool_d_sample",
    )(seq_t)


def _key_to_float(key):
    bits = key ^ ((key >> 31) & jnp.int32(0x7FFFFFFF))
    return lax.bitcast_convert_type(bits, F32)


def _count_cols(sc_ref, n_chunks, chunk, flag):
    n_acc = 4
    lanes = sc_ref.shape[1]

    def chunk_body(c, cnts):
        c0 = pl.multiple_of(c * chunk, chunk)
        s = sc_ref[pl.ds(c0, chunk), :]
        cnts = list(cnts)
        for r in range(chunk // 8):
            cnts[r % n_acc] = cnts[r % n_acc] + flag(s[r * 8:(r + 1) * 8], c0 + r * 8)
        return tuple(cnts)

    cnts = lax.fori_loop(0, n_chunks, chunk_body, tuple(jnp.zeros((8, lanes), jnp.int32) for _ in range(n_acc)))
    return jnp.sum(functools.reduce(jnp.add, cnts), axis=0, keepdims=True)


def _kth_largest_cols(sc_ref, n_chunks, chunk, k):
    int_min = jnp.int32(-(2 ** 31))
    lanes = sc_ref.shape[1]

    def bit_body(it, key):
        cand = key ^ lax.shift_left(jnp.int32(1), 31 - it)
        cf = jnp.broadcast_to(_key_to_float(cand), (8, lanes))
        tot = _count_cols(sc_ref, n_chunks, chunk, lambda s, r0: jnp.where(s >= cf, 1, 0))
        return jnp.where(tot >= k, cand, key)

    key = lax.fori_loop(0, 32, bit_body, jnp.full((1, lanes), int_min, jnp.int32))
    return _key_to_float(key)


def _break_ties_cols(sc_ref, n_chunks, chunk, k, thr, n_rows):
    lanes = sc_ref.shape[1]
    thr8 = jnp.broadcast_to(thr, (8, lanes))
    n_ge = _count_cols(sc_ref, n_chunks, chunk, lambda s, r0: jnp.where(s >= thr8, 1, 0))

    @pl.when(jnp.max(n_ge) > k)
    def _():
        need = k - _count_cols(sc_ref, n_chunks, chunk, lambda s, r0: jnp.where(s > thr8, 1, 0))
        row8 = lax.broadcasted_iota(jnp.int32, (8, lanes), 0)
        n_bits = (n_rows - 1).bit_length()

        def bit_body(it, last):
            cand = last | lax.shift_left(jnp.int32(1), n_bits - 1 - it)
            cand8 = jnp.broadcast_to(cand, (8, lanes))
            before = _count_cols(sc_ref, n_chunks, chunk,
                                 lambda s, r0: jnp.where(s == thr8, jnp.where(row8 + r0 < cand8, 1, 0), 0))
            return jnp.where(before < need, cand, last)

        last = lax.fori_loop(0, n_bits, bit_body, jnp.zeros((1, lanes), jnp.int32))

        def drop(c, carry):
            c0 = pl.multiple_of(c * chunk, chunk)
            s = sc_ref[pl.ds(c0, chunk), :]
            row = c0 + lax.broadcasted_iota(jnp.int32, (chunk, lanes), 0)
            sc_ref[pl.ds(c0, chunk), :] = jnp.where(s == thr, jnp.where(row > last, -jnp.inf, s), s)
            return carry

        lax.fori_loop(0, n_chunks, drop, 0)


PA_CHUNK = 512
PA_QUERIES = 512
PA_QSUB = LANES


def _p_attn_kernel(q_ref, qi_ref, wit_ref, kia_ref, kib_ref, k_ref, vt_ref, o_ref, sc_ref, *, topk):
    nq = PA_QUERIES
    blk = pl.program_id(1)
    t0 = blk * nq
    n_ch = (t0 + nq + PA_CHUNK - 1) // PA_CHUNK
    tok = t0 + lax.broadcasted_iota(jnp.int32, (1, nq), 1)
    n_grp = (q_ref.shape[1] // HEAD_DIM) // N_KV_HEADS
    n_pairs = N_IDX_HEADS // 2

    wit = wit_ref[...] * (IDX_DIM ** -0.5 * N_IDX_HEADS ** -0.5)

    def score_chunk(c, carry):
        c0 = pl.multiple_of(c * PA_CHUNK, PA_CHUNK)
        ka = kia_ref[pl.ds(c0, PA_CHUNK), :]
        kb = kib_ref[pl.ds(c0, PA_CHUNK), :]
        acc = jnp.zeros((PA_CHUNK, nq), F32)
        for pp in range(n_pairs // 2):
            qp = jnp.concatenate([qi_ref[:, (2 * pp + i) * LANES:(2 * pp + i + 1) * LANES] for i in range(2)], axis=0)
            da = lax.dot_general(ka, qp, NT_DIMS, preferred_element_type=F32)
            db = lax.dot_general(kb, qp, NT_DIMS, preferred_element_type=F32)
            for i in range(2):
                h = 2 * (2 * pp + i)
                sl = slice(i * nq, (i + 1) * nq)
                acc = acc + jnp.maximum(da[:, sl], 0.0) * wit[h:h + 1, :]
                acc = acc + jnp.maximum(db[:, sl], 0.0) * wit[h + 1:h + 2, :]
        keypos = c0 + lax.broadcasted_iota(jnp.int32, (PA_CHUNK, nq), 0)
        sc_ref[pl.ds(c0, PA_CHUNK), :] = jnp.where(keypos <= tok, acc, -jnp.inf)
        return carry

    lax.fori_loop(0, n_ch, score_chunk, 0)

    thr = _kth_largest_cols(sc_ref, n_ch, PA_CHUNK, topk)
    thr = jnp.where(tok + 1 <= topk, jnp.finfo(F32).min, thr)
    _break_ties_cols(sc_ref, n_ch, PA_CHUNK, topk, thr, sc_ref.shape[0])

    items = [(j, u) for j in range(N_KV_HEADS) for u in range(nq // PA_QSUB)]
    width = n_grp * PA_QSUB

    def attn_chunk(c, carry, diagonal=False):
        c0 = pl.multiple_of(c * PA_CHUNK, PA_CHUNK)
        n_keys = [(u + 1) * PA_QSUB if diagonal else PA_CHUNK for u in range(nq // PA_QSUB)]
        biases = []
        for u in range(nq // PA_QSUB):
            qs = slice(u * PA_QSUB, (u + 1) * PA_QSUB)
            bias = jnp.where(sc_ref[pl.ds(c0, n_keys[u]), qs] >= thr[:, qs], 0.0, NEG)
            biases.append(jnp.concatenate([bias] * n_grp, axis=1))

        def logits(i):
            j, u = items[i]
            qj = jnp.concatenate([q_ref[u * PA_QSUB:(u + 1) * PA_QSUB,
                                        (j * n_grp + g) * HEAD_DIM:(j * n_grp + g + 1) * HEAD_DIM]
                                  for g in range(n_grp)], axis=0)
            kc = k_ref[pl.ds(c0, n_keys[u]), j * HEAD_DIM:(j + 1) * HEAD_DIM]
            return lax.dot_general(kc, qj, NT_DIMS, preferred_element_type=F32) + biases[u]

        def weights(i, s):
            m = carry[i][0]
            m_new = jnp.maximum(m, jnp.max(s, axis=0, keepdims=True))
            return m_new, jnp.exp2(m - m_new), jnp.exp2(s - m_new).astype(BF16)

        def accumulate(i, alpha, p):
            j, u = items[i]
            vt = vt_ref[j * V_ROWS:(j + 1) * V_ROWS, pl.ds(c0, n_keys[u])]
            return alpha * carry[i][1] + jnp.dot(vt, p, preferred_element_type=F32)

        s, w, out = {}, {}, []
        for step in range(len(items) + 2):
            if step < len(items):
                s[step] = logits(step)
            if 0 <= step - 1 < len(items):
                w[step - 1] = weights(step - 1, s.pop(step - 1))
            if 0 <= step - 2 < len(items):
                m_new, alpha, p = w.pop(step - 2)
                out.append((m_new, accumulate(step - 2, alpha, p)))
        return tuple(out)

    init = tuple((jnp.full((1, width), NEG, F32), jnp.zeros((V_ROWS, width), F32)) for _ in items)
    assert PA_CHUNK == PA_QUERIES
    final = attn_chunk(n_ch - 1, lax.fori_loop(0, n_ch - 1, attn_chunk, init), diagonal=True)
    for (j, u), (_, acc) in zip(items, final):
        o_t = acc[:HEAD_DIM] * (1.0 / acc[HEAD_DIM:HEAD_DIM + 1])
        for g in range(n_grp):
            h = j * n_grp + g
            o_ref[u * PA_QSUB:(u + 1) * PA_QSUB, h * HEAD_DIM:(h + 1) * HEAD_DIM] = (
                o_t[:, g * PA_QSUB:(g + 1) * PA_QSUB].T.astype(BF16))


def _p_attn(p16, wit, kia, kib, vt, batch, seq, topk, d_attn):
    p16b = p16.reshape(batch, seq, W16)
    kv_w = N_KV_HEADS * HEAD_DIM
    idx_w = N_IDX_HEADS * IDX_DIM
    o_idx_q = 3 * d_attn
    o_k = o_idx_q + idx_w
    assert o_idx_q % idx_w == 0 and o_k % kv_w == 0 and o_k + 2 * kv_w == W16
    return pl.pallas_call(
        functools.partial(_p_attn_kernel, topk=topk),
        out_shape=jax.ShapeDtypeStruct((batch, seq, d_attn), BF16),
        grid=(batch, seq // PA_QUERIES),
        in_specs=[pl.BlockSpec((None, PA_QUERIES, d_attn), lambda b, i: (b, i, 0)),
                  pl.BlockSpec((None, PA_QUERIES, idx_w), lambda b, i: (b, i, o_idx_q // idx_w)),
                  pl.BlockSpec((None, N_IDX_HEADS, PA_QUERIES), lambda b, i: (b, 0, i)),
                  pl.BlockSpec((None, seq, LANES), lambda b, i: (b, 0, 0)),
                  pl.BlockSpec((None, seq, LANES), lambda b, i: (b, 0, 0)),
                  pl.BlockSpec((None, seq, kv_w), lambda b, i: (b, 0, o_k // kv_w)),
                  pl.BlockSpec((None, N_KV_HEADS * V_ROWS, seq), lambda b, i: (b, 0, 0))],
        out_specs=pl.BlockSpec((None, PA_QUERIES, d_attn), lambda b, i: (b, i, 0)),
        scratch_shapes=[pltpu.VMEM((seq, PA_QUERIES), F32)],
        compiler_params=_cparams(2),
        name="p_attn",
    )(p16b, p16b, wit, kia, kib, p16b, vt)


S_SCORE_GROUP = 8


def _s_score_kernel(pt_ref, qi_ref, w_ref, kin_ref, *rest, n_pages, page, n_new, group):
    del pt_ref
    ik_refs, o_ref = rest[:group * n_pages], rest[group * n_pages]
    for g in range(group):
        qi = qi_ref[g]
        w = w_ref[g]

        def piece(ik_t, qi=qi, w=w):
            d = jnp.dot(qi, ik_t.astype(BF16), preferred_element_type=F32)
            x = jnp.maximum(d, 0.0) * w
            return jnp.concatenate([jnp.sum(x[t * N_IDX_HEADS:(t + 1) * N_IDX_HEADS], axis=0, keepdims=True)
                                    for t in range(n_new)], axis=0)

        for p in range(n_pages):
            o_ref[g, :, p * page:(p + 1) * page] = piece(ik_refs[g * n_pages + p][...])
        new = piece(kin_ref[g])
        col = lax.broadcasted_iota(jnp.int32, new.shape, 1)
        tok = lax.broadcasted_iota(jnp.int32, new.shape, 0)
        o_ref[g, :, n_pages * page:(n_pages + 1) * page] = jnp.where(col <= tok, new, -jnp.inf)


def _page_specs(n_pages, shape, group=1):
    zeros = (0,) * len(shape)
    return [pl.BlockSpec((None,) + shape,
                         functools.partial(lambda b, pt, g, p: (pt[b * group + g, p],) + zeros, g=g, p=p))
            for g in range(group) for p in range(n_pages)]


def _s_score(page_table, qi_th, w_th, ki_new_t, cache_ik_t, n_new):
    nb, n_pages = page_table.shape
    page = cache_ik_t.shape[2]
    width = (n_pages + 1) * page
    hx = qi_th.shape[1]
    group = S_SCORE_GROUP
    assert nb % group == 0
    return pl.pallas_call(
        functools.partial(_s_score_kernel, n_pages=n_pages, page=page, n_new=n_new, group=group),
        out_shape=jax.ShapeDtypeStruct((nb, n_new, width), F32),
        grid_spec=pltpu.PrefetchScalarGridSpec(
            num_scalar_prefetch=1,
            grid=(nb // group,),
            in_specs=[pl.BlockSpec((group, hx, IDX_DIM), lambda b, pt: (b, 0, 0)),
                      pl.BlockSpec((group, hx, 1), lambda b, pt: (b, 0, 0)),
                      pl.BlockSpec((group, IDX_DIM, page), lambda b, pt: (b, 0, 0))]
            + _page_specs(n_pages, (IDX_DIM, page), group),
            out_specs=pl.BlockSpec((group, n_new, width), lambda b, pt: (b, 0, 0))),
        compiler_params=_cparams(1),
        name="s_score",
    )(page_table, qi_th, w_th, ki_new_t, *([cache_ik_t] * (group * n_pages)))


def _s_sel_kernel(sc_ref, o_ref, buf_ref, *, topk, chunk):
    n_rows = sc_ref.shape[0]
    buf_ref[...] = sc_ref[...]
    thr = _kth_largest_cols(buf_ref, n_rows // chunk, chunk, topk)
    _break_ties_cols(buf_ref, n_rows // chunk, chunk, topk, thr, n_rows)
    o_ref[...] = jnp.where(buf_ref[...] >= thr, 1.0, 0.0)


def _s_sel(scores_t, topk):
    n_rows, n_q = scores_t.shape
    return pl.pallas_call(
        functools.partial(_s_sel_kernel, topk=topk, chunk=LANES),
        out_shape=jax.ShapeDtypeStruct((n_rows, n_q), F32),
        grid=(n_q // LANES,),
        in_specs=[pl.BlockSpec((n_rows, LANES), lambda i: (0, i))],
        out_specs=pl.BlockSpec((n_rows, LANES), lambda i: (0, i)),
        scratch_shapes=[pltpu.VMEM((n_rows, LANES), F32)],
        compiler_params=_cparams(1),
        name="s_sel",
    )(scores_t)


def _s_attn_kernel(pt_ref, q_ref, sel_ref, kn_ref, vn_ref, *rest, n_pages, n_grp, n_new, group):
    del pt_ref
    gp = group * n_pages
    k_refs, v_refs, o_ref = rest[:gp], rest[gp:2 * gp], rest[2 * gp]
    rows = q_ref.shape[1]
    pr = k_refs[0].shape[0]
    page = pr // N_KV_HEADS
    spread = (lax.broadcasted_iota(jnp.int32, (page, pr), 1) // N_KV_HEADS
              == lax.broadcasted_iota(jnp.int32, (page, pr), 0)).astype(BF16)
    lane_kv = lax.broadcasted_iota(jnp.int32, (rows, pr), 1) % N_KV_HEADS
    row_kv = lax.broadcasted_iota(jnp.int32, (rows, pr), 0) // (n_grp * n_new)
    head_bias = jnp.where(lane_kv == row_kv, 0.0, NEG)
    for g in range(group):
        q = q_ref[g]
        sel = sel_ref[g].astype(BF16)
        sel = jnp.concatenate([sel[:, p * page:(p + 1) * page] for p in range(n_pages + 1)], axis=0)
        sel = jnp.dot(sel, spread, preferred_element_type=F32)
        pieces = []
        for p in range(n_pages + 1):
            kp = kn_ref[g] if p == n_pages else k_refs[g * n_pages + p][...].astype(BF16)
            w = kp.shape[0]
            s = lax.dot_general(q, kp, NT_DIMS, preferred_element_type=F32)
            bias = jnp.concatenate([(sel[p * 8:(p + 1) * 8, :w] - 1.0) * -NEG] * (rows // 8), axis=0)
            pieces.append(s + (bias + head_bias[:, :w]))
        m = functools.reduce(jnp.maximum, [jnp.max(s, axis=1, keepdims=True) for s in pieces])
        l = jnp.zeros((rows, 1), F32)
        o = jnp.zeros((rows, HEAD_DIM), F32)
        for p in range(n_pages + 1):
            e = jnp.exp2(pieces[p] - m)
            l = l + jnp.sum(e, axis=1, keepdims=True)
            vp = vn_ref[g] if p == n_pages else v_refs[g * n_pages + p][...].astype(BF16)
            o = o + jnp.dot(e.astype(BF16), vp, preferred_element_type=F32)
        o_ref[g] = (o / l).astype(BF16)


S_ATTN_GROUP = 2


def _s_attn(page_table, q_ht, sel8, k_new2d, v_new2d, cache_k2d, cache_v2d, n_grp, n_new):
    nb, n_pages = page_table.shape
    pr = cache_k2d.shape[1]
    pr_new = k_new2d.shape[1]
    rows = q_ht.shape[1]
    width = sel8.shape[-1]
    group = S_ATTN_GROUP
    assert nb % group == 0
    return pl.pallas_call(
        functools.partial(_s_attn_kernel, n_pages=n_pages, n_grp=n_grp, n_new=n_new, group=group),
        out_shape=jax.ShapeDtypeStruct((nb, rows, HEAD_DIM), BF16),
        grid_spec=pltpu.PrefetchScalarGridSpec(
            num_scalar_prefetch=1,
            grid=(nb // group,),
            in_specs=[pl.BlockSpec((group, rows, HEAD_DIM), lambda b, pt: (b, 0, 0)),
                      pl.BlockSpec((group, 8, width), lambda b, pt: (b, 0, 0)),
                      pl.BlockSpec((group, pr_new, HEAD_DIM), lambda b, pt: (b, 0, 0)),
                      pl.BlockSpec((group, pr_new, HEAD_DIM), lambda b, pt: (b, 0, 0))]
            + _page_specs(n_pages, (pr, HEAD_DIM), group) + _page_specs(n_pages, (pr, HEAD_DIM), group),
            out_specs=pl.BlockSpec((group, rows, HEAD_DIM), lambda b, pt: (b, 0, 0))),
        compiler_params=_cparams(1),
        name="s_attn",
    )(page_table, q_ht, sel8, k_new2d, v_new2d,
      *([cache_k2d] * (group * n_pages)), *([cache_v2d] * (group * n_pages)))


def _mix_kernel(*refs, tiles_per_seq):
    if tiles_per_seq is None:
        d_ref, a_ref, ga_ref, gb_ref, wg_ref, ps_ref, wup_ref, wua_ref, o_ref = refs
    else:
        pin_ref, halo_ref, a_ref, ga_ref, gb_ref, wg_ref, ps_ref, wup_ref, wua_ref, o_ref, buf_ref = refs
    ya = jnp.dot(a_ref[...], wua_ref[...], preferred_element_type=F32)
    if tiles_per_seq is None:
        group = wg_ref.shape[1]
        d = [d_ref[:, g * group:(g + 1) * group] for g in range(wg_ref.shape[0])]
    else:
        t = pl.program_id(0) % tiles_per_seq
        tm = pin_ref.shape[0]
        halo = jnp.where(t > 0, halo_ref[...], 0.0)
        pos = t * tm + lax.broadcasted_iota(jnp.int32, (tm, 1), 0)
        d = _pool_d_tile(pin_ref[...], halo, pos, buf_ref)
    y = jnp.concatenate([jnp.dot(d[g], wg_ref[g], preferred_element_type=F32) for g in range(len(d))],
                        axis=1) * ps_ref[...]
    yp = jnp.dot(y.astype(BF16), wup_ref[...], preferred_element_type=F32)
    o_ref[...] = (jax.nn.sigmoid(ga_ref[...].astype(F32)) * yp
                  + jax.nn.sigmoid(gb_ref[...].astype(F32)) * ya).astype(BF16)


def _mix(pool_in, attn, p16, w_grp, pool_scale, w_up_pool, w_up_attn, tm, tiles_per_seq=None):
    m, dm = attn.shape
    pw = pool_scale.shape[1]
    const = lambda *shape: pl.BlockSpec(shape, lambda i: (0,) * len(shape))
    if tiles_per_seq is None:
        pool_specs, pool_args, scratch = [pl.BlockSpec((tm, pw), lambda i: (i, 0))], [pool_in], []
    else:
        halo_blocks = tm // POOL_HALO
        pool_specs = [pl.BlockSpec((tm, pw), lambda i: (i, 0)),
                      pl.BlockSpec((POOL_HALO, pw), lambda i: (jnp.maximum(i * halo_blocks - 1, 0), 0))]
        pool_args, scratch = [pool_in, pool_in], [pltpu.VMEM((tm + POOL_HALO, pw), F32)]
    return pl.pallas_call(
        functools.partial(_mix_kernel, tiles_per_seq=tiles_per_seq),
        out_shape=jax.ShapeDtypeStruct((m, dm), BF16),
        grid=(m // tm,),
        in_specs=pool_specs + [
            pl.BlockSpec((tm, dm), lambda i: (i, 0)),
            pl.BlockSpec((tm, dm), lambda i: (i, 1)),
            pl.BlockSpec((tm, dm), lambda i: (i, 2)),
            const(*w_grp.shape), const(1, pw), const(*w_up_pool.shape), const(*w_up_attn.shape)],
        out_specs=pl.BlockSpec((tm, dm), lambda i: (i, 0)),
        scratch_shapes=scratch,
        compiler_params=_cparams(1),
        name="mix",
    )(*pool_args, attn, p16, p16, w_grp, pool_scale, w_up_pool, w_up_attn)


def _out_kernel(mix_ref, x_ref, gate_ref, sh_ref, sc_ref, g_ref, w_ref, h_ref, hn_ref):
    h = x_ref[...] + gate_ref[...] * jnp.dot(mix_ref[...], w_ref[...], preferred_element_type=F32)
    h_ref[...] = h
    hn_ref[...] = (_rms(h, g_ref[...]) * (1.0 + sc_ref[...]) + sh_ref[...]).astype(BF16)


def _out(mix, x, gate, shift, scale, g2, w_out, tm, rows_per_mod):
    m, d = x.shape
    return pl.pallas_call(
        _out_kernel,
        out_shape=(jax.ShapeDtypeStruct((m, d), F32), jax.ShapeDtypeStruct((m, d), BF16)),
        grid=(m // tm,),
        in_specs=[pl.BlockSpec((tm, d), lambda i: (i, 0)),
                  pl.BlockSpec((tm, d), lambda i: (i, 0)),
                  _mod_spec(gate, tm, rows_per_mod),
                  _mod_spec(shift, tm, rows_per_mod),
                  _mod_spec(scale, tm, rows_per_mod),
                  pl.BlockSpec((1, d), lambda i: (0, 0)),
                  pl.BlockSpec((d, d), lambda i: (0, 0))],
        out_specs=(pl.BlockSpec((tm, d), lambda i: (i, 0)), pl.BlockSpec((tm, d), lambda i: (i, 0))),
        compiler_params=_cparams(1),
        name="out",
    )(mix, x, gate, shift, scale, g2, w_out)


FFN_TF = 512


def _ffn_kernel(hn_ref, h_ref, gate_ref, gf_ref, wg_ref, wu_ref, wo_ref, o_ref, acc_ref):
    f = pl.program_id(1)

    @pl.when(f == 0)
    def _():
        acc_ref[...] = jnp.zeros_like(acc_ref)

    hn = hn_ref[...]
    a = jnp.dot(hn, wg_ref[...], preferred_element_type=F32)
    u = jnp.dot(hn, wu_ref[...], preferred_element_type=F32)
    z = (a * jax.nn.sigmoid(a) * u).astype(BF16)
    acc_ref[...] += jnp.dot(z, wo_ref[...], preferred_element_type=F32)

    @pl.when(f == pl.num_programs(1) - 1)
    def _():
        o_ref[...] = _rms(h_ref[...] + gate_ref[...] * acc_ref[...], gf_ref[...])


def _ffn(hn, h, gate, g_final, w_ffn_in, w_ffn_out, tm, rows_per_mod):
    m, d = h.shape
    d_ff = w_ffn_out.shape[0]
    n_f = d_ff // FFN_TF
    return pl.pallas_call(
        _ffn_kernel,
        out_shape=jax.ShapeDtypeStruct((m, d), F32),
        grid=(m // tm, n_f),
        in_specs=[pl.BlockSpec((tm, d), lambda i, f: (i, 0)),
                  pl.BlockSpec((tm, d), lambda i, f: (i, 0)),
                  _mod_spec(gate, tm, rows_per_mod),
                  pl.BlockSpec((1, d), lambda i, f: (0, 0)),
                  pl.BlockSpec((d, FFN_TF), lambda i, f: (0, f)),
                  pl.BlockSpec((d, FFN_TF), lambda i, f: (0, f + n_f)),
                  pl.BlockSpec((FFN_TF, d), lambda i, f: (f, 0))],
        out_specs=pl.BlockSpec((tm, d), lambda i, f: (i, 0)),
        scratch_shapes=[pltpu.VMEM((tm, d), F32)],
        compiler_params=_cparams(2),
        name="ffn",
    )(hn, h, gate, g_final, w_ffn_in, w_ffn_in, w_ffn_out)


def _back(x2, p16, pool_in, attn, mods, lw, tm, rows_per_mod, tiles_per_seq=None):
    mix = _mix(pool_in, attn, p16, lw["w_grp"], lw["pool_scale"], lw["w_up_pool"], lw["w_up_attn"], tm,
               tiles_per_seq)
    h, hn = _out(mix, x2, mods[2], mods[3], mods[4], lw["g2"], lw["w_out"], tm, rows_per_mod)
    return _ffn(hn, h, mods[5], lw["g_final"], lw["w_ffn_in"], lw["w_ffn_out"], tm, rows_per_mod)


def kernel(x_prompt, x_sample, cache_k, cache_v, cache_idx_k, state_pool, page_table, c_prompt, c_sample,
           w_ada, b_ada, g_norm1, w_in, w_pool_grp, pool_scale, w_up_pool, w_up_attn, w_out, g_norm2,
           w_ffn_in, w_ffn_out, g_final):
    batch, seq, dm = x_prompt.shape
    nb, n_new, _ = x_sample.shape
    depth = w_ada.shape[0]
    assert depth == 1, "single-layer step"
    n_phys, page = cache_k.shape[1], cache_k.shape[2]
    n_pages = page_table.shape[1]
    past = n_pages * page
    pool_w = state_pool.shape[-1]
    kv_w = N_KV_HEADS * HEAD_DIM
    n_heads = dm // HEAD_DIM
    n_grp = n_heads // N_KV_HEADS
    idx_w = N_IDX_HEADS * IDX_DIM
    assert page == LANES and n_new == 4 and seq % PA_CHUNK == 0

    w_t = jnp.swapaxes(w_in[0], 0, 1)
    assert (pool_w, dm, kv_w, idx_w) == (2 * FRONT_TN, 4 * FRONT_TN, FRONT_TN, 2 * FRONT_TN)
    o_q, o_k, o_v, o_qi = pool_w, pool_w + dm, pool_w + dm + kv_w, pool_w + dm + 2 * kv_w
    o_ki = o_qi + idx_w
    o_ga = o_ki + IDX_DIM + N_IDX_HEADS
    o_gb = o_ga + dm
    assert o_gb + dm == w_t.shape[0] and o_ki + FRONT_TN <= w_t.shape[0]
    tile_rows = ([o_q + i * FRONT_TN for i in range(4)] + [o_ga + i * FRONT_TN for i in range(4)]
                 + [o_gb + i * FRONT_TN for i in range(4)] + [o_qi, o_qi + FRONT_TN, 0, FRONT_TN, o_k, o_v, o_ki])
    assert len(tile_rows) == N_FRONT_TILES and all(r % 8 == 0 for r in tile_rows)
    lw = {"w_grp": w_pool_grp[0].astype(BF16), "pool_scale": pool_scale[0].reshape(1, pool_w),
          "w_up_pool": w_up_pool[0].astype(BF16), "w_up_attn": w_up_attn[0].astype(BF16),
          "w_out": w_out[0].astype(BF16), "g2": g_norm2[0].reshape(1, dm), "g_final": g_final.reshape(1, dm),
          "w_ffn_in": w_ffn_in[0].astype(BF16), "w_ffn_out": w_ffn_out[0].astype(BF16)}
    g1 = g_norm1[0].reshape(1, dm)

    n_c = batch + nb
    c_all = jnp.concatenate([c_prompt, c_sample, jnp.zeros((-n_c % 8, dm), F32)], axis=0)
    mod = _ada(c_all, w_ada[0], b_ada[0])
    mods_p = [mod[:batch, i * dm:(i + 1) * dm].reshape(batch, 1, dm) for i in range(6)]
    mods_s = [jnp.repeat(mod[batch:n_c, i * dm:(i + 1) * dm], n_new, axis=0) for i in range(6)]

    xp = x_prompt.reshape(batch * seq, dm)
    p16, p32 = _front(xp, g1, mods_p[0], mods_p[1], w_t, tile_rows, FRONT_ROWS, seq)
    p32b = p32.reshape(batch, seq, W32)
    k_prompt = p32b[:, :, pool_w:pool_w + kv_w].reshape(1, batch, seq, N_KV_HEADS, HEAD_DIM)
    v_prompt = p32b[:, :, pool_w + kv_w:pool_w + 2 * kv_w].reshape(1, batch, seq, N_KV_HEADS, HEAD_DIM)
    o_small = pool_w + 2 * kv_w
    ki_p = p32b[:, :, o_small:o_small + IDX_DIM]
    wi_p = p32b[:, :, o_small + IDX_DIM:o_small + IDX_DIM + N_IDX_HEADS]
    pool_prompt = p32b[:, seq - POOL_BUF:, :pool_w][None]
    ki16 = ki_p.astype(BF16)
    zeros = jnp.zeros_like(ki16)
    kia = jnp.concatenate([ki16, zeros], axis=-1)
    kib = jnp.concatenate([zeros, ki16], axis=-1)
    wit_p = jnp.swapaxes(wi_p, 1, 2)
    vt_p = jnp.swapaxes(p16.reshape(batch, seq, W16)[:, :, W16 - kv_w:], 1, 2)
    vt_p = jnp.concatenate([vt_p.reshape(batch, N_KV_HEADS, HEAD_DIM, seq),
                            jnp.ones((batch, N_KV_HEADS, V_ROWS - HEAD_DIM, seq), BF16)], axis=2)
    vt_p = vt_p.reshape(batch, N_KV_HEADS * V_ROWS, seq)
    attn_p = _p_attn(p16, wit_p, kia, kib, vt_p, batch, seq, min(TOPK_MAX, seq // 4), dm).reshape(batch * seq, dm)
    assert seq % ROW_TILE == 0 and ROW_TILE % POOL_HALO == 0 and pool_w == FRONT_TN * 2
    y_prompt = _back(xp, p16, p32, attn_p, mods_p, lw, ROW_TILE, seq, seq // ROW_TILE).reshape(batch, seq, dm)

    xs = x_sample.reshape(nb * n_new, dm)
    s16, s32 = _front(xs, g1, mods_s[0], mods_s[1], w_t, tile_rows, nb * n_new, 0)
    s32b = s32.reshape(nb, n_new, W32)
    k_new = s32b[:, :, pool_w:pool_w + kv_w]
    v_new = s32b[:, :, pool_w + kv_w:pool_w + 2 * kv_w]
    ki_new = s32b[:, :, o_small:o_small + IDX_DIM]
    wi_s = s32b[:, :, o_small + IDX_DIM:o_small + IDX_DIM + N_IDX_HEADS]
    seq_s = jnp.concatenate([state_pool[0], s32b[:, :, :pool_w]], axis=1)
    pool_sample = seq_s[:, n_new:][None]
    d_s = _pool_d_sample(jnp.swapaxes(seq_s, 0, 1), n_new)
    d_s = jnp.swapaxes(d_s, 0, 1).reshape(nb * n_new, pool_w)

    s16b = s16.reshape(nb, n_new, W16)
    hx = N_IDX_HEADS * n_new
    qi_th = s16b[:, :, 3 * dm:3 * dm + idx_w].reshape(nb, hx, IDX_DIM)
    w_th = (wi_s * (IDX_DIM ** -0.5 * N_IDX_HEADS ** -0.5)).reshape(nb, hx, 1)
    ki_new_t = jnp.swapaxes(jnp.pad(ki_new, ((0, 0), (0, page - n_new), (0, 0))), 1, 2)
    cache_ik_t = jnp.swapaxes(cache_idx_k[0], 1, 2)
    scores = _s_score(page_table, qi_th, w_th, ki_new_t, cache_ik_t, n_new)
    topk_s = min(TOPK_MAX, (past + n_new) // 4)
    width = scores.shape[-1]
    sel_t = _s_sel(scores.reshape(nb * n_new, width).T, topk_s)
    sel4 = sel_t.T.reshape(nb, n_new, width)
    sel8 = jnp.concatenate([sel4, sel4], axis=1)

    q_ht = jnp.swapaxes(s16b[:, :, :dm].reshape(nb, n_new, n_heads, HEAD_DIM), 1, 2)
    q_ht = q_ht.reshape(nb, n_heads * n_new, HEAD_DIM)
    pr = page * N_KV_HEADS
    k_new2d = jnp.pad(k_new.astype(BF16), ((0, 0), (0, 8 - n_new), (0, 0))).reshape(nb, 8 * N_KV_HEADS, HEAD_DIM)
    v_new2d = jnp.pad(v_new.astype(BF16), ((0, 0), (0, 8 - n_new), (0, 0))).reshape(nb, 8 * N_KV_HEADS, HEAD_DIM)
    ck2d = cache_k[0].reshape(n_phys, pr, HEAD_DIM)
    cv2d = cache_v[0].reshape(n_phys, pr, HEAD_DIM)
    o_s = _s_attn(page_table, q_ht, sel8, k_new2d, v_new2d, ck2d, cv2d, n_grp, n_new)
    attn_s = jnp.swapaxes(o_s.reshape(nb, n_heads, n_new, HEAD_DIM), 1, 2).reshape(nb * n_new, dm)
    y_sample = _back(xs, s16, d_s, attn_s, mods_s, lw, nb * n_new, 0).reshape(nb, n_new, dm)

    k_sample = k_new.reshape(1, nb, n_new, N_KV_HEADS, HEAD_DIM)
    v_sample = v_new.reshape(1, nb, n_new, N_KV_HEADS, HEAD_DIM)
    return (y_prompt, y_sample, k_prompt, v_prompt, ki_p[None], pool_prompt,
            k_sample, v_sample, ki_new[None], pool_sample)
```
